```python
import math
import jax
import jax.numpy as jnp
from jax import lax
import numpy as np


D_MODEL = 2048
BATCH = 2
SEQ = 4096
DEPTH = 2

CHUNK = 64
N_EVEN = (DEPTH + 1) // 2
N_ODD = DEPTH // 2
D_FF = 5632
EPS = 1e-6
S5_WIDTH = 1024
S5_GROUP = 16
S5_GROUPS = S5_WIDTH // S5_GROUP
S5_STATE = 64
GDN_HEADS = 8
GDN_DK = 128
GDN_DV = 128
GDN_CONV = 4
GDN_QKV = GDN_HEADS * (2 * GDN_DK + GDN_DV)
MLSTM_HEADS = 4
MLSTM_DK = 128
MLSTM_DV = 256
RET_HEADS = 4
RET_DK = 128
RET_DV = 256
ROPE_BASE = 10000.0
EVEN_IN = S5_WIDTH + GDN_HEADS * (2 * GDN_DK + 2 * GDN_DV) + 2 * GDN_HEADS
EVEN_MIX = S5_WIDTH + GDN_HEADS * GDN_DV
ODD_IN = MLSTM_HEADS * (2 * MLSTM_DK + 2 * MLSTM_DV) + 2 * MLSTM_HEADS + RET_HEADS * (2 * RET_DK + 2 * RET_DV)
ODD_MIX = MLSTM_HEADS * MLSTM_DV + RET_HEADS * RET_DV

kernel_name = 'chunk_causal_hybrid_s5_deltanet_mlstm_retention'


def rms_norm(x, g):
    xf = x.astype(jnp.float32)
    y = xf * lax.rsqrt(jnp.mean(xf * xf, axis=-1, keepdims=True) + EPS)
    return (y * g.astype(jnp.float32)).astype(x.dtype)


def head_group_norm(y, g):
    mu = jnp.mean(y, axis=-1, keepdims=True)
    var = jnp.mean(jnp.square(y - mu), axis=-1, keepdims=True)
    return (y - mu) * lax.rsqrt(var + EPS) * g.astype(jnp.float32)


def l2_normalize(x):
    return x * lax.rsqrt(jnp.sum(x * x, axis=-1, keepdims=True) + EPS)


def swiglu(x, w1, w3, w2):
    return (jax.nn.silu(x @ w1) * (x @ w3)) @ w2


def split_cols(t, sizes):
    idx = np.cumsum(sizes)[:-1].tolist()
    return jnp.split(t, idx, axis=-1)


def chunk_heads(t):
    b, l, h = t.shape[:3]
    t = t.reshape((b, l // CHUNK, CHUNK, h) + t.shape[3:])
    return t.transpose((1, 0, 3, 2) + tuple(range(4, t.ndim)))


def unchunk_heads(t):
    n, b, h, c = t.shape[:4]
    t = t.transpose((1, 0, 3, 2) + tuple(range(4, t.ndim)))
    return t.reshape((b, n * c, h) + t.shape[4:])


def causal_depthwise_conv(x, w):
    k, c = w.shape
    return lax.conv_general_dilated(x, w[:, None, :].astype(x.dtype), window_strides=(1,), padding=[(k - 1, 0)], dimension_numbers=('NWC', 'WIO', 'NWC'), feature_group_count=c)


def apply_rotary(x, positions):
    half = x.shape[-1] // 2
    inv_freq = ROPE_BASE ** (-jnp.arange(half, dtype=jnp.float32) / half)
    ang = positions.astype(jnp.float32)[:, :, None, None] * inv_freq
    cos, sin = jnp.cos(ang), jnp.sin(ang)
    x1, x2 = x[..., :half], x[..., half:]
    return jnp.concatenate([x1 * cos - x2 * sin, x2 * cos + x1 * sin], axis=-1)


def _complex_linear_combine(e1, e2):
    a1r, a1i, b1r, b1i = e1
    a2r, a2i, b2r, b2i = e2
    return (a2r * a1r - a2i * a1i, a2r * a1i + a2i * a1r, a2r * b1r - a2i * b1i + b2r, a2r * b1i + a2i * b1r + b2i)


def s5_mixer(u, a_re, a_im, log_step, b_re, b_im, c_re, c_im, d_skip, w_glu, b_glu):
    bsz, seq, _ = u.shape
    f32 = jnp.float32
    uf = u.astype(f32).reshape(bsz, seq, S5_GROUPS, S5_GROUP)
    step = jnp.exp(log_step.astype(f32))[:, None]
    ar, ai = a_re.astype(f32), a_im.astype(f32)
    mag = jnp.exp(ar * step)
    lr, li = mag * jnp.cos(ai * step), mag * jnp.sin(ai * step)
    den = ar * ar + ai * ai
    fr = ((lr - 1.0) * ar + li * ai) / den
    fi = (li * ar - (lr - 1.0) * ai) / den
    br, bi = b_re.astype(f32), b_im.astype(f32)
    bbr = fr[..., None] * br - fi[..., None] * bi
    bbi = fr[..., None] * bi + fi[..., None] * br
    bu_r = jnp.einsum('blgh,gph->lbgp', uf, bbr)
    bu_i = jnp.einsum('blgh,gph->lbgp', uf, bbi)
    lam_r = jnp.broadcast_to(lr[None, None], (seq, 1) + lr.shape)
    lam_i = jnp.broadcast_to(li[None, None], (seq, 1) + li.shape)
    _, _, xr, xi = lax.associative_scan(_complex_linear_combine, (lam_r, lam_i, bu_r, bu_i), axis=0)
    y = (jnp.einsum('lbgp,ghp->blgh', xr, c_re.astype(f32)) - jnp.einsum('lbgp,ghp->blgh', xi, c_im.astype(f32)) + d_skip.astype(f32) * uf)
    y = jax.nn.gelu(y.reshape(bsz, seq, S5_WIDTH))
    return y * jax.nn.sigmoid(y @ w_glu.astype(f32) + b_glu.astype(f32))


def gated_deltanet(q, k, v, z, beta_pre, a_pre, conv_w, a_log, dt_bias, norm_g):
    bsz, seq, _ = q.shape
    f32 = jnp.float32
    qkv = jax.nn.silu(causal_depthwise_conv(jnp.concatenate([q, k, v], axis=-1), conv_w)).astype(f32)
    q, k, v = split_cols(qkv, [GDN_HEADS * GDN_DK, GDN_HEADS * GDN_DK, GDN_HEADS * GDN_DV])
    q = l2_normalize(q.reshape(bsz, seq, GDN_HEADS, GDN_DK)) * GDN_DK ** -0.5
    k = l2_normalize(k.reshape(bsz, seq, GDN_HEADS, GDN_DK))
    v = v.reshape(bsz, seq, GDN_HEADS, GDN_DV)
    beta = jax.nn.sigmoid(beta_pre.astype(f32))
    g = -jnp.exp(a_log.astype(f32)) * jax.nn.softplus(a_pre.astype(f32) + dt_bias.astype(f32))
    qc, kc, vc = chunk_heads(q), chunk_heads(k), chunk_heads(v)
    bc = chunk_heads(beta)
    gcum = jnp.cumsum(chunk_heads(g), axis=-1)
    tri = jnp.tril(jnp.ones((CHUNK, CHUNK), dtype=bool))
    strict = jnp.tril(jnp.ones((CHUNK, CHUNK), dtype=bool), -1)
    decay = jnp.exp(jnp.where(tri, gcum[..., :, None] - gcum[..., None, :], -jnp.inf))
    kb = kc * bc[..., None]
    lower = jnp.where(strict, jnp.einsum('nbhcd,nbhsd->nbhcs', kb, kc) * decay, 0.0)
    unit = lower + jnp.eye(CHUNK, dtype=f32)
    rhs = jnp.concatenate([vc * bc[..., None], kb * jnp.exp(gcum)[..., None]], axis=-1)
    sol = lax.linalg.triangular_solve(unit, rhs, left_side=True, lower=True, unit_diagonal=True)
    u_c, w_c = sol[..., :GDN_DV], sol[..., GDN_DV:]
    qk = jnp.where(tri, jnp.einsum('nbhcd,nbhsd->nbhcs', qc, kc) * decay, 0.0)
    q_dec = qc * jnp.exp(gcum)[..., None]
    k_dec = kc * jnp.exp(gcum[..., -1:] - gcum)[..., None]
    g_last = jnp.exp(gcum[..., -1])

    def step(state, xs):
        u_i, w_i, qk_i, qd_i, kd_i, gl_i = xs
        v_new = u_i - jnp.einsum('bhcd,bhde->bhce', w_i, state)
        out = jnp.einsum('bhcd,bhde->bhce', qd_i, state) + jnp.einsum('bhcs,bhse->bhce', qk_i, v_new)
        state = state * gl_i[..., None, None] + jnp.einsum('bhcd,bhce->bhde', kd_i, v_new)
        return state, out

    s0 = jnp.zeros((bsz, GDN_HEADS, GDN_DK, GDN_DV), f32)
    _, o = lax.scan(step, s0, (u_c, w_c, qk, q_dec, k_dec, g_last))
    o = unchunk_heads(o)
    zf = z.astype(f32).reshape(bsz, seq, GDN_HEADS, GDN_DV)
    o = rms_norm(o, norm_g) * jax.nn.silu(zf)
    return o.reshape(bsz, seq, GDN_HEADS * GDN_DV)


def mlstm(q, k, v, o_pre, i_pre, f_pre, gate_bias, norm_g):
    bsz, seq, _ = q.shape
    f32 = jnp.float32
    nh = MLSTM_HEADS
    qc = chunk_heads(q.astype(f32).reshape(bsz, seq, nh, MLSTM_DK) * MLSTM_DK ** -0.5)
    kc = chunk_heads(k.astype(f32).reshape(bsz, seq, nh, MLSTM_DK))
    vc = chunk_heads(v.astype(f32).reshape(bsz, seq, nh, MLSTM_DV))
    ig = chunk_heads(i_pre.astype(f32) + gate_bias[0].astype(f32))
    logf = chunk_heads(jax.nn.log_sigmoid(f_pre.astype(f32) + gate_bias[1].astype(f32)))
    bcum = jnp.cumsum(logf, axis=-1)
    b_last = bcum[..., -1]
    tri = jnp.tril(jnp.ones((CHUNK, CHUNK), dtype=bool))
    intra_log = jnp.where(tri, bcum[..., :, None] - bcum[..., None, :] + ig[..., None, :], -jnp.inf)
    intra_max = jnp.max(intra_log, axis=-1)
    upd_log = b_last[..., None] - bcum + ig
    upd_max = jnp.max(upd_log, axis=-1)
    qk = jnp.einsum('nbhcd,nbhsd->nbhcs', qc, kc)

    def step(carry, xs):
        c_mat, n_vec, m = carry
        q_i, k_i, v_i, qk_i, b_i, bl_i, il_i, im_i, ul_i, um_i = xs
        inter_log = b_i + m[..., None]
        m_s = jnp.maximum(inter_log, im_i)
        inter_w = jnp.exp(inter_log - m_s)
        s = qk_i * jnp.exp(il_i - m_s[..., None])
        num = inter_w[..., None] * jnp.einsum('bhcd,bhde->bhce', q_i, c_mat) + jnp.einsum('bhcs,bhse->bhce', s, v_i)
        den = inter_w * jnp.einsum('bhcd,bhd->bhc', q_i, n_vec) + jnp.sum(s, axis=-1)
        h = num / jnp.maximum(jnp.abs(den), jnp.exp(-m_s))[..., None]
        m_new = jnp.maximum(bl_i + m, um_i)
        carry_w = jnp.exp(bl_i + m - m_new)
        kw = k_i * jnp.exp(ul_i - m_new[..., None])[..., None]
        c_mat = carry_w[..., None, None] * c_mat + jnp.einsum('bhcd,bhce->bhde', kw, v_i)
        n_vec = carry_w[..., None] * n_vec + jnp.sum(kw, axis=-2)
        return (c_mat, n_vec, m_new), h

    init = (jnp.zeros((bsz, nh, MLSTM_DK, MLSTM_DV), f32), jnp.zeros((bsz, nh, MLSTM_DK), f32), jnp.zeros((bsz, nh), f32))
    _, h = lax.scan(step, init, (qc, kc, vc, qk, bcum, b_last, intra_log, intra_max, upd_log, upd_max))
    h = unchunk_heads(h)
    o = jax.nn.sigmoid(o_pre.astype(f32).reshape(bsz, seq, nh, MLSTM_DV))
    return (o * rms_norm(h, norm_g)).reshape(bsz, seq, nh * MLSTM_DV)


def retention(q, k, v, g_pre, positions, norm_g):
    bsz, seq, _ = q.shape
    f32 = jnp.float32
    nh = RET_HEADS
    qh = apply_rotary(q.astype(f32).reshape(bsz, seq, nh, RET_DK), positions) * RET_DK ** -0.5
    kh = apply_rotary(k.astype(f32).reshape(bsz, seq, nh, RET_DK), positions)
    vh = v.astype(f32).reshape(bsz, seq, nh, RET_DV)
    log_gamma = jnp.log1p(-jnp.exp2(-5.0 - jnp.arange(nh, dtype=f32)))
    idx = jnp.arange(CHUNK, dtype=f32)
    tri = jnp.tril(jnp.ones((CHUNK, CHUNK), dtype=bool))
    diff = jnp.where(tri, idx[:, None] - idx[None, :], 0.0)
    dmat = jnp.where(tri, jnp.exp(diff * log_gamma[:, None, None]), 0.0)
    xi = jnp.exp((idx + 1.0) * log_gamma[:, None])
    zeta = jnp.exp((CHUNK - 1.0 - idx) * log_gamma[:, None])
    gamma_c = jnp.exp(CHUNK * log_gamma)
    qc, kc, vc = chunk_heads(qh), chunk_heads(kh), chunk_heads(vh)
    intra = jnp.einsum('nbhcs,nbhse->nbhce', jnp.einsum('nbhcd,nbhsd->nbhcs', qc, kc) * dmat, vc)
    kz = kc * zeta[:, :, None]

    def step(state, xs):
        q_i, kz_i, v_i = xs
        out = jnp.einsum('bhcd,bhde->bhce', q_i, state) * xi[:, :, None]
        state = state * gamma_c[:, None, None] + jnp.einsum('bhcd,bhce->bhde', kz_i, v_i)
        return state, out

    s0 = jnp.zeros((bsz, nh, RET_DK, RET_DV), f32)
    _, inter = lax.scan(step, s0, (qc, kz, vc))
    y = unchunk_heads(intra + inter)
    y = head_group_norm(y, norm_g) * jax.nn.silu(g_pre.astype(f32).reshape(bsz, seq, nh, RET_DV))
    return y.reshape(bsz, seq, nh * RET_DV)


def even_mixer(h, w_in, w_out, a_re, a_im, log_step, b_re, b_im, c_re, c_im, d_skip, w_glu, b_glu, conv_w, a_log, dt_bias, gdn_g):
    zin = h @ w_in
    u, q, k, v, z, beta_pre, a_pre = split_cols(zin, [S5_WIDTH, GDN_HEADS * GDN_DK, GDN_HEADS * GDN_DK, GDN_HEADS * GDN_DV, GDN_HEADS * GDN_DV, GDN_HEADS, GDN_HEADS])
    ya = s5_mixer(u, a_re, a_im, log_step, b_re, b_im, c_re, c_im, d_skip, w_glu, b_glu)
    yb = gated_deltanet(q, k, v, z, beta_pre, a_pre, conv_w, a_log, dt_bias, gdn_g)
    return jnp.concatenate([ya, yb], axis=-1).astype(h.dtype) @ w_out


def odd_mixer(h, positions, w_in, w_out, gate_bias, mlstm_g, ret_g):
    zin = h @ w_in
    cq, ck, cv, co, ci, cf, rq, rk, rv, rg = split_cols(zin, [MLSTM_HEADS * MLSTM_DK, MLSTM_HEADS * MLSTM_DK, MLSTM_HEADS * MLSTM_DV, MLSTM_HEADS * MLSTM_DV, MLSTM_HEADS, MLSTM_HEADS, RET_HEADS * RET_DK, RET_HEADS * RET_DK, RET_HEADS * RET_DV, RET_HEADS * RET_DV])
    yc = mlstm(cq, ck, cv, co, ci, cf, gate_bias, mlstm_g)
    yd = retention(rq, rk, rv, rg, positions, ret_g)
    return jnp.concatenate([yc, yd], axis=-1).astype(h.dtype) @ w_out


def setup_inputs(seed: int = 0) -> dict:
    key = jax.random.key(seed)
    ks = jax.random.split(key, 32)
    f32 = jnp.float32

    def nrm(i, shape, scale):
        return jax.random.normal(ks[i], shape, f32) * scale

    x = nrm(0, (BATCH, SEQ, D_MODEL), 1.0)
    offset = jax.random.randint(ks[1], (BATCH, 1), 0, 64, dtype=jnp.int32) * CHUNK
    positions = offset + jnp.arange(SEQ, dtype=jnp.int32)[None, :]
    ffn_norm = 1.0 + nrm(2, (DEPTH, 2, D_MODEL), 0.02)
    ffn_w1 = nrm(3, (DEPTH, 2, D_MODEL, D_FF), D_MODEL ** -0.5)
    ffn_w3 = nrm(4, (DEPTH, 2, D_MODEL, D_FF), D_MODEL ** -0.5)
    ffn_w2 = nrm(5, (DEPTH, 2, D_FF, D_MODEL), D_FF ** -0.5)
    mix_norm = 1.0 + nrm(6, (DEPTH, D_MODEL), 0.02)
    even_w_in = nrm(7, (N_EVEN, D_MODEL, EVEN_IN), D_MODEL ** -0.5)
    even_w_out = nrm(8, (N_EVEN, EVEN_MIX, D_MODEL), EVEN_MIX ** -0.5)
    n_idx = jnp.arange(S5_STATE, dtype=f32)
    s5_a_re = -0.5 + nrm(9, (N_EVEN, S5_GROUPS, S5_STATE), 0.01)
    s5_a_im = math.pi * n_idx + nrm(10, (N_EVEN, S5_GROUPS, S5_STATE), 0.01)
    s5_log_step = jax.random.uniform(ks[11], (N_EVEN, S5_GROUPS), f32, math.log(1e-3), math.log(1e-1))
    s5_b_re = nrm(12, (N_EVEN, S5_GROUPS, S5_STATE, S5_GROUP), (2 * S5_GROUP) ** -0.5)
    s5_b_im = nrm(13, (N_EVEN, S5_GROUPS, S5_STATE, S5_GROUP), (2 * S5_GROUP) ** -0.5)
    s5_c_re = nrm(14, (N_EVEN, S5_GROUPS, S5_GROUP, S5_STATE), S5_STATE ** -0.5)
    s5_c_im = nrm(15, (N_EVEN, S5_GROUPS, S5_GROUP, S5_STATE), S5_STATE ** -0.5)
    s5_d = nrm(16, (N_EVEN, S5_GROUPS, S5_GROUP), 1.0)
    s5_w_glu = nrm(17, (N_EVEN, S5_WIDTH, S5_WIDTH), S5_WIDTH ** -0.5)
    s5_b_glu = nrm(18, (N_EVEN, S5_WIDTH), 0.02)
    gdn_conv_w = nrm(19, (N_EVEN, GDN_CONV, GDN_QKV), GDN_CONV ** -0.5)
    gdn_a_log = jnp.log(jax.random.uniform(ks[20], (N_EVEN, GDN_HEADS), f32, 1.0, 16.0))
    dt = jnp.exp(jax.random.uniform(ks[21], (N_EVEN, GDN_HEADS), f32, math.log(1e-3), math.log(1e-1)))
    gdn_dt_bias = dt + jnp.log(-jnp.expm1(-dt))
    gdn_norm = 1.0 + nrm(22, (N_EVEN, GDN_DV), 0.02)
    odd_w_in = nrm(23, (N_ODD, D_MODEL, ODD_IN), D_MODEL ** -0.5)
    odd_w_out = nrm(24, (N_ODD, ODD_MIX, D_MODEL), ODD_MIX ** -0.5)
    i_bias = -1.0 + nrm(25, (N_ODD, MLSTM_HEADS), 0.1)
    f_bias = jnp.linspace(3.0, 6.0, MLSTM_HEADS, dtype=f32)[None, :] + nrm(26, (N_ODD, MLSTM_HEADS), 0.1)
    mlstm_gate_bias = jnp.stack([i_bias, f_bias], axis=1)
    mlstm_norm = 1.0 + nrm(27, (N_ODD, MLSTM_DV), 0.02)
    ret_norm = 1.0 + nrm(28, (N_ODD, RET_DV), 0.02)
    final_norm = 1.0 + nrm(29, (D_MODEL,), 0.02)
    return {'x': x, 'positions': positions, 'ffn_norm': ffn_norm, 'ffn_w1': ffn_w1, 'ffn_w3': ffn_w3, 'ffn_w2': ffn_w2, 'mix_norm': mix_norm, 'even_w_in': even_w_in, 'even_w_out': even_w_out, 's5_a_re': s5_a_re, 's5_a_im': s5_a_im, 's5_log_step': s5_log_step, 's5_b_re': s5_b_re, 's5_b_im': s5_b_im, 's5_c_re': s5_c_re, 's5_c_im': s5_c_im, 's5_d': s5_d, 's5_w_glu': s5_w_glu, 's5_b_glu': s5_b_glu, 'gdn_conv_w': gdn_conv_w, 'gdn_a_log': gdn_a_log, 'gdn_dt_bias': gdn_dt_bias, 'gdn_norm': gdn_norm, 'odd_w_in': odd_w_in, 'odd_w_out': odd_w_out, 'mlstm_gate_bias': mlstm_gate_bias, 'mlstm_norm': mlstm_norm, 'ret_norm': ret_norm, 'final_norm': final_norm}


def reference(x, positions, ffn_norm, ffn_w1, ffn_w3, ffn_w2, mix_norm, even_w_in, even_w_out, s5_a_re, s5_a_im, s5_log_step, s5_b_re, s5_b_im, s5_c_re, s5_c_im, s5_d, s5_w_glu, s5_b_glu, gdn_conv_w, gdn_a_log, gdn_dt_bias, gdn_norm, odd_w_in, odd_w_out, mlstm_gate_bias, mlstm_norm, ret_norm, final_norm):
    for layer in range(DEPTH):
        x = x + 0.5 * swiglu(rms_norm(x, ffn_norm[layer, 0]), ffn_w1[layer, 0], ffn_w3[layer, 0], ffn_w2[layer, 0])
        h = rms_norm(x, mix_norm[layer])
        j = layer // 2
        if layer % 2 == 0:
            x = x + even_mixer(h, even_w_in[j], even_w_out[j], s5_a_re[j], s5_a_im[j], s5_log_step[j], s5_b_re[j], s5_b_im[j], s5_c_re[j], s5_c_im[j], s5_d[j], s5_w_glu[j], s5_b_glu[j], gdn_conv_w[j], gdn_a_log[j], gdn_dt_bias[j], gdn_norm[j])
        else:
            x = x + odd_mixer(h, positions, odd_w_in[j], odd_w_out[j], mlstm_gate_bias[j], mlstm_norm[j], ret_norm[j])
        x = x + 0.5 * swiglu(rms_norm(x, ffn_norm[layer, 1]), ffn_w1[layer, 1], ffn_w3[layer, 1], ffn_w2[layer, 1])
    return rms_norm(x, final_norm)
```

```python
import functools
import math

import jax
import jax.numpy as jnp
from jax import lax
from jax.experimental import pallas as pl
from jax.experimental.pallas import tpu as pltpu

F32 = jnp.float32
BF16 = jnp.bfloat16

EPS = 1e-6
CHUNK = 64
ROPE_BASE = 10000.0
LANES = 128
SUBLANES = 8
VMEM_LIMIT_BYTES = 56 * 1024 * 1024

S5_GROUP = 16
S5_STATE = 64
S5_TILE_GROUPS = LANES // S5_GROUP
S5_PAIRS = S5_TILE_GROUPS // 2
S5_TIME = 128
S5_PITCH = S5_TIME + SUBLANES

GDN_HEADS = 8
GDN_DK = 128
GDN_DV = 128
GDN_CONV = 4
MLSTM_HEADS = 4
MLSTM_DK = 128
MLSTM_DV = 256
RET_HEADS = 4
RET_DK = 128
RET_DV = 256


def _params(*semantics):
    return pltpu.CompilerParams(dimension_semantics=semantics, vmem_limit_bytes=VMEM_LIMIT_BYTES)


def _mm(a, b):
    return jnp.dot(a.astype(BF16), b.astype(BF16), preferred_element_type=F32)


def _mm_nt(a, b):
    return lax.dot_general(a.astype(BF16), b.astype(BF16), (((1,), (1,)), ((), ())), preferred_element_type=F32)


def _mm_tn(a, b):
    return lax.dot_general(a.astype(BF16), b.astype(BF16), (((0,), (0,)), ((), ())), preferred_element_type=F32)


def _split3(x):
    x1 = x.astype(BF16)
    r1 = x - x1.astype(F32)
    x2 = r1.astype(BF16)
    x3 = (r1 - x2.astype(F32)).astype(BF16)
    return x1, x2, x3


def _cumsum_rows(tri_lower, x):
    t = tri_lower.astype(BF16)
    return sum(jnp.dot(t, p, preferred_element_type=F32) for p in _split3(x))


def _cumsum_lanes(x, tri_upper):
    t = tri_upper.astype(BF16)
    return sum(jnp.dot(p, t, preferred_element_type=F32) for p in _split3(x))


def _rms_norm(x, g):
    return x * lax.rsqrt(jnp.mean(x * x, axis=-1, keepdims=True) + EPS) * g


def _silu(x):
    return x * jax.nn.sigmoid(x)


def _softplus(x):
    return jnp.maximum(x, 0.0) + jnp.log1p(jnp.exp(-jnp.abs(x)))


def _tri_masks(n):
    r = lax.broadcasted_iota(jnp.int32, (n, n), 0)
    c = lax.broadcasted_iota(jnp.int32, (n, n), 1)
    return r >= c, r > c, r <= c


def _ffn_kernel(x_ref, g_ref, w1_ref, w3_ref, w2_ref, fg_ref, o_ref, h_ref, acc_ref, *, final_norm):
    j = pl.program_id(1)

    @pl.when(j == 0)
    def _():
        h_ref[...] = _rms_norm(x_ref[...], g_ref[...]).astype(BF16)
        acc_ref[...] = jnp.zeros_like(acc_ref)

    h = h_ref[...]
    a = jnp.dot(h, w1_ref[...], preferred_element_type=F32)
    b = jnp.dot(h, w3_ref[...], preferred_element_type=F32)
    acc_ref[...] += jnp.dot((_silu(a) * b).astype(BF16), w2_ref[...], preferred_element_type=F32)

    @pl.when(j == pl.num_programs(1) - 1)
    def _():
        y = x_ref[...] + 0.5 * acc_ref[...]
        if final_norm:
            y = _rms_norm(y, fg_ref[...])
        o_ref[...] = y


def _ffn(x, g, w1, w3, w2, final_g=None, *, tm=512, tf=512):
    m, d = x.shape
    f = w1.shape[1]
    final_norm = final_g is not None
    fg = final_g if final_norm else g
    return pl.pallas_call(
        functools.partial(_ffn_kernel, final_norm=final_norm),
        grid=(m // tm, f // tf),
        in_specs=[
            pl.BlockSpec((tm, d), lambda i, j: (i, 0)),
            pl.BlockSpec((1, d), lambda i, j: (0, 0)),
            pl.BlockSpec((d, tf), lambda i, j: (0, j)),
            pl.BlockSpec((d, tf), lambda i, j: (0, j)),
            pl.BlockSpec((tf, d), lambda i, j: (j, 0)),
            pl.BlockSpec((1, d), lambda i, j: (0, 0)),
        ],
        out_specs=pl.BlockSpec((tm, d), lambda i, j: (i, 0)),
        out_shape=jax.ShapeDtypeStruct((m, d), F32),
        scratch_shapes=[pltpu.VMEM((tm, d), BF16), pltpu.VMEM((tm, d), F32)],
        compiler_params=_params("parallel", "arbitrary"),
        name="ffn",
    )(x, g.reshape(1, d), w1, w3, w2, fg.reshape(1, d))


def _inproj_kernel(x_ref, g_ref, w_ref, wg_ref, o_ref, og_ref, h_ref):
    @pl.when(pl.program_id(1) == 0)
    def _():
        h = _rms_norm(x_ref[...], g_ref[...]).astype(BF16)
        h_ref[...] = h
        og_ref[...] = jnp.dot(h, wg_ref[...], preferred_element_type=F32)

    o_ref[...] = jnp.dot(h_ref[...], w_ref[...], preferred_element_type=F32)


def _inproj(x, g, w_main, w_gate, *, tm=512, tn=1024):
    m, d = x.shape
    n = w_main.shape[1]
    return pl.pallas_call(
        _inproj_kernel,
        grid=(m // tm, n // tn),
        in_specs=[
            pl.BlockSpec((tm, d), lambda i, j: (i, 0)),
            pl.BlockSpec((1, d), lambda i, j: (0, 0)),
            pl.BlockSpec((d, tn), lambda i, j: (0, j)),
            pl.BlockSpec((d, LANES), lambda i, j: (0, 0)),
        ],
        out_specs=[
            pl.BlockSpec((tm, tn), lambda i, j: (i, j)),
            pl.BlockSpec((tm, LANES), lambda i, j: (i, 0)),
        ],
        out_shape=[jax.ShapeDtypeStruct((m, n), F32), jax.ShapeDtypeStruct((m, LANES), F32)],
        scratch_shapes=[pltpu.VMEM((tm, d), BF16)],
        compiler_params=_params("parallel", "arbitrary"),
        name="inproj",
    )(x, g.reshape(1, d), w_main, w_gate)


def _outproj_kernel(x_ref, ya_ref, yb_ref, wa_ref, wb_ref, o_ref):
    o_ref[...] = (x_ref[...] + jnp.dot(ya_ref[...], wa_ref[...], preferred_element_type=F32)
                  + jnp.dot(yb_ref[...], wb_ref[...], preferred_element_type=F32))


def _outproj(x, ya, yb, wa, wb, *, tm=1024, tn=1024):
    m, d = x.shape
    ka, kb = ya.shape[1], yb.shape[1]
    return pl.pallas_call(
        _outproj_kernel,
        grid=(m // tm, d // tn),
        in_specs=[
            pl.BlockSpec((tm, tn), lambda i, j: (i, j)),
            pl.BlockSpec((tm, ka), lambda i, j: (i, 0)),
            pl.BlockSpec((tm, kb), lambda i, j: (i, 0)),
            pl.BlockSpec((ka, tn), lambda i, j: (0, j)),
            pl.BlockSpec((kb, tn), lambda i, j: (0, j)),
        ],
        out_specs=pl.BlockSpec((tm, tn), lambda i, j: (i, j)),
        out_shape=jax.ShapeDtypeStruct((m, d), F32),
        compiler_params=_params("parallel", "arbitrary"),
        name="outproj",
    )(x, ya, yb, wa, wb)


def _s5_disc_kernel(are_ref, aim_ref, step_ref, bre_ref, bim_ref, lr_ref, li_ref, bbr_ref, bbi_ref):
    ar, ai = are_ref[...], aim_ref[...]
    step = jnp.exp(step_ref[...])
    mag = jnp.exp(ar * step)
    lr, li = mag * jnp.cos(ai * step), mag * jnp.sin(ai * step)
    den = ar * ar + ai * ai
    fr = ((lr - 1.0) * ar + li * ai) / den
    fi = (li * ar - (lr - 1.0) * ai) / den
    lr_ref[...] = lr
    li_ref[...] = li
    br, bi = bre_ref[...], bim_ref[...]
    bbr_ref[...] = fr[None] * br - fi[None] * bi
    bbi_ref[...] = fr[None] * bi + fi[None] * br


def _s5_discretise(a_re, a_im, log_step, b_re, b_im):
    g, p = a_re.shape
    h = b_re.shape[-1]
    sd = jax.ShapeDtypeStruct
    return pl.pallas_call(
        _s5_disc_kernel,
        out_shape=[sd((g, p), F32), sd((g, p), F32), sd((h, g, p), F32), sd((h, g, p), F32)],
        name="s5_discretise",
    )(a_re, a_im, log_step.reshape(g, 1), b_re.transpose(2, 0, 1), b_im.transpose(2, 0, 1))


def _s5_kernel(u_ref, wb_ref, lr_ref, li_ref, cs_ref, dsk_ref, wglu_ref, bglu_ref, o_ref,
               rre_ref, rim_ref, xr_ref, xi_ref, y_ref, *, batch, tiles):
    chains = tiles * S5_PAIRS * batch

    @pl.when(pl.program_id(0) == 0)
    def _():
        xr_ref[...] = jnp.zeros_like(xr_ref)
        xi_ref[...] = jnp.zeros_like(xi_ref)

    def chain_rows(j, pair, b):
        return pl.ds(((j * S5_PAIRS + pair) * batch + b) * S5_PITCH, S5_TIME)

    for b in range(batch):
        for j in range(tiles):
            drive = jnp.dot(u_ref[b, :, j * LANES:(j + 1) * LANES].astype(BF16), wb_ref[j],
                            preferred_element_type=F32)
            for pair in range(S5_PAIRS):
                rre_ref[chain_rows(j, pair, b), :] = drive[:, pair * LANES:(pair + 1) * LANES]
                rim_ref[chain_rows(j, pair, b), :] = drive[:, (S5_PAIRS + pair) * LANES:(S5_PAIRS + pair + 1) * LANES]

    lr, li = lr_ref[...], li_ref[...]

    def step(t, carry):
        xr, xi = carry
        rows = pl.ds(t, chains, stride=S5_PITCH)
        nxr = lr * xr - li * xi + rre_ref[rows, :]
        nxi = lr * xi + li * xr + rim_ref[rows, :]
        rre_ref[rows, :] = nxr
        rim_ref[rows, :] = nxi
        return nxr, nxi

    xr, xi = lax.fori_loop(0, S5_TIME, step, (xr_ref[...], xi_ref[...]), unroll=2)
    xr_ref[...] = xr
    xi_ref[...] = xi

    for b in range(batch):
        for j in range(tiles):
            states = jnp.concatenate(
                [rre_ref[chain_rows(j, pair, b), :].astype(BF16) for pair in range(S5_PAIRS)]
                + [rim_ref[chain_rows(j, pair, b), :].astype(BF16) for pair in range(S5_PAIRS)], axis=1)
            cols = slice(j * LANES, (j + 1) * LANES)
            y = jnp.dot(states, cs_ref[j], preferred_element_type=F32) + dsk_ref[:, cols] * u_ref[b, :, cols]
            y_ref[b, :, cols] = jax.nn.gelu(y)

    for b in range(batch):
        y = y_ref[b]
        gate = jnp.dot(y.astype(BF16), wglu_ref[...], preferred_element_type=F32) + bglu_ref[...]
        o_ref[b] = (y * jax.nn.sigmoid(gate)).astype(BF16)


def _s5_mixer(zin, a_re, a_im, log_step, b_re, b_im, c_re, c_im, d_skip, w_glu, b_glu):
    batch, seq, _ = zin.shape
    groups, states = a_re.shape
    width = groups * S5_GROUP
    tiles = width // LANES
    chains = tiles * S5_PAIRS * batch
    lr, li, bbr, bbi = _s5_discretise(a_re, a_im, log_step, b_re, b_im)

    eye = jnp.eye(S5_TILE_GROUPS, dtype=F32)
    bb = jnp.stack([bbr, bbi]).reshape(2, S5_GROUP, tiles, S5_TILE_GROUPS, states)
    wb = jnp.einsum("rhjgp,gk->jghrkp", bb, eye).reshape(tiles, LANES, 2 * S5_TILE_GROUPS * states).astype(BF16)
    cc = jnp.stack([c_re, -c_im]).reshape(2, tiles, S5_TILE_GROUPS, S5_GROUP, states)
    cs = jnp.einsum("rjghp,gk->jrkpgh", cc, eye).reshape(tiles, 2 * S5_TILE_GROUPS * states, LANES).astype(BF16)

    def per_chain(lam):
        t = lam.reshape(tiles * S5_PAIRS, 1, 2 * states)
        return jnp.broadcast_to(t, (tiles * S5_PAIRS, batch, 2 * states)).reshape(chains, 2 * states)

    const = lambda *shape: pl.BlockSpec(shape, lambda i: (0,) * len(shape))
    return pl.pallas_call(
        functools.partial(_s5_kernel, batch=batch, tiles=tiles),
        grid=(seq // S5_TIME,),
        in_specs=[
            pl.BlockSpec((batch, S5_TIME, width), lambda i: (0, i, 0)),
            const(tiles, LANES, 2 * S5_TILE_GROUPS * states),
            const(chains, LANES),
            const(chains, LANES),
            const(tiles, 2 * S5_TILE_GROUPS * states, LANES),
            const(1, width),
            const(width, width),
            const(1, width),
        ],
        out_specs=pl.BlockSpec((batch, S5_TIME, width), lambda i: (0, i, 0)),
        out_shape=jax.ShapeDtypeStruct((batch, seq, width), BF16),
        scratch_shapes=[
            pltpu.VMEM((chains * S5_PITCH, LANES), F32),
            pltpu.VMEM((chains * S5_PITCH, LANES), F32),
            pltpu.VMEM((chains, LANES), F32),
            pltpu.VMEM((chains, LANES), F32),
            pltpu.VMEM((batch, S5_TIME, width), F32),
        ],
        compiler_params=_params("arbitrary"),
        name="s5",
    )(zin, wb, per_chain(lr), per_chain(li), cs, d_skip.reshape(1, width), w_glu.astype(BF16),
      b_glu.reshape(1, width))


def _unit_lower_inverse(a):
    n = a.shape[0]
    eye = (lax.broadcasted_iota(jnp.int32, (n, n), 0) == lax.broadcasted_iota(jnp.int32, (n, n), 1)).astype(F32)
    p = -a
    t = eye + p
    for _ in range(int(math.log2(n)) - 1):
        p = _mm(p, p)
        t = t + _mm(t, p)
    return t


def _shift_rows(cur, tail, s):
    rolled = pltpu.roll(cur, s, 0)
    rolled_tail = pltpu.roll(tail, s, 0)
    row = lax.broadcasted_iota(jnp.int32, tail.shape, 0)
    top = jnp.where(row < s, rolled_tail, rolled[:SUBLANES])
    return jnp.concatenate([top, rolled[SUBLANES:]], axis=0)


def _gdn_kernel(q_ref, k_ref, v_ref, z_ref, gc_ref, gr_ref, cw_ref, alc_ref, dtc_ref, alr_ref, dtr_ref,
                ng_ref, o_ref, s_ref, tail_ref):
    heads, dk, dv = GDN_HEADS, GDN_DK, GDN_DV

    @pl.when(pl.program_id(1) == 0)
    def _():
        s_ref[...] = jnp.zeros_like(s_ref)
        tail_ref[...] = jnp.zeros_like(tail_ref)

    tri, strict, tri_u = _tri_masks(CHUNK)

    conv = []
    for idx, ref in enumerate((q_ref, k_ref, v_ref)):
        cur = ref[...]
        width = cur.shape[1]
        tail = tail_ref[idx]
        cw = cw_ref[:, idx * width:(idx + 1) * width]
        acc = cur * cw[GDN_CONV - 1:GDN_CONV]
        for s in range(1, GDN_CONV):
            acc = acc + _shift_rows(cur, tail, s) * cw[GDN_CONV - 1 - s:GDN_CONV - s]
        tail_ref[idx] = cur[CHUNK - SUBLANES:]
        conv.append(_silu(acc))
    q_all, k_all, v_all = conv

    gc = gc_ref[...]
    beta_c = jax.nn.sigmoid(gc)
    g_c = -jnp.exp(alc_ref[...]) * _softplus(gc + dtc_ref[...])
    gcum_c = _cumsum_rows(tri, g_c)
    gr = gr_ref[...]
    g_r = -jnp.exp(alr_ref[...]) * _softplus(gr + dtr_ref[...])
    gcum_r = _cumsum_lanes(g_r, tri_u)

    for h in range(heads):
        q = q_all[:, h * dk:(h + 1) * dk]
        k = k_all[:, h * dk:(h + 1) * dk]
        v = v_all[:, h * dv:(h + 1) * dv]
        q = q * lax.rsqrt(jnp.sum(q * q, axis=-1, keepdims=True) + EPS) * dk ** -0.5
        k = k * lax.rsqrt(jnp.sum(k * k, axis=-1, keepdims=True) + EPS)
        beta = beta_c[:, h:h + 1]
        gcol = gcum_c[:, heads + h:heads + h + 1]
        grow = gcum_r[heads + h:heads + h + 1, :]
        glast = gcol[CHUNK - 1:CHUNK, :]
        decay = jnp.where(tri, jnp.exp(gcol - grow), 0.0)
        egc = jnp.exp(gcol)
        kb = k * beta
        lower = jnp.where(strict, _mm_nt(kb, k) * decay, 0.0)
        t_inv = _unit_lower_inverse(lower)
        sol = _mm(t_inv, jnp.concatenate([v * beta, kb * egc], axis=1))
        u_c, w_c = sol[:, :dv], sol[:, dv:]
        qk = jnp.where(tri, _mm_nt(q, k) * decay, 0.0)
        state = s_ref[h]
        v_new = u_c - _mm(w_c, state)
        out = _mm(q * egc, state) + _mm(qk, v_new)
        s_ref[h] = state * jnp.exp(glast) + _mm_tn(k * jnp.exp(glast - gcol), v_new)
        o = _rms_norm(out, ng_ref[...]) * _silu(z_ref[:, h * dv:(h + 1) * dv])
        o_ref[:, h * dv:(h + 1) * dv] = o.astype(BF16)


def _gate_rows(gates, count):
    b, l, _ = gates.shape
    return gates[:, :, :count].reshape(b, l // CHUNK, CHUNK, count).transpose(0, 1, 3, 2)


def _lane_row(vec, offset):
    return jnp.zeros((1, LANES), F32).at[0, offset:offset + vec.shape[0]].set(vec)


def _gdn_mixer(zin, gates, conv_w, a_log, dt_bias, norm_g, *, col0):
    batch, seq, _ = zin.shape
    heads = GDN_HEADS
    qk_w, v_w = heads * GDN_DK, heads * GDN_DV
    assert qk_w == v_w and col0 % qk_w == 0
    c0 = col0 // qk_w
    zeros = jnp.zeros((heads,), F32)
    al_c, dt_c = _lane_row(a_log, heads), _lane_row(dt_bias, heads)
    al_r = jnp.concatenate([zeros, a_log]).reshape(2 * heads, 1)
    dt_r = jnp.concatenate([zeros, dt_bias]).reshape(2 * heads, 1)
    tok = lambda blk: pl.BlockSpec((None, CHUNK, qk_w), lambda b, n, blk=blk: (b, n, c0 + blk))
    const = lambda *shape: pl.BlockSpec(shape, lambda b, n: (0,) * len(shape))
    return pl.pallas_call(
        _gdn_kernel,
        grid=(batch, seq // CHUNK),
        in_specs=[
            tok(0), tok(1), tok(2), tok(3),
            pl.BlockSpec((None, CHUNK, LANES), lambda b, n: (b, n, 0)),
            pl.BlockSpec((None, None, 2 * heads, CHUNK), lambda b, n: (b, n, 0, 0)),
            const(GDN_CONV, 2 * qk_w + v_w),
            const(1, LANES), const(1, LANES), const(2 * heads, 1), const(2 * heads, 1),
            const(1, GDN_DV),
        ],
        out_specs=pl.BlockSpec((None, CHUNK, v_w), lambda b, n: (b, n, 0)),
        out_shape=jax.ShapeDtypeStruct((batch, seq, v_w), BF16),
        scratch_shapes=[
            pltpu.VMEM((heads, GDN_DK, GDN_DV), F32),
            pltpu.VMEM((3, SUBLANES, qk_w), F32),
        ],
        compiler_params=_params("parallel", "arbitrary"),
        name="gdn",
    )(zin, zin, zin, zin, gates, _gate_rows(gates, 2 * heads), conv_w, al_c, dt_c, al_r, dt_r,
      norm_g.reshape(1, GDN_DV))


def _mlstm_kernel(q_ref, k_ref, v_ref, op_ref, gc_ref, gr_ref, bc_ref, br_ref, ng_ref, o_ref,
                  c_ref, n_ref, m_ref):
    heads, dk, dv = MLSTM_HEADS, MLSTM_DK, MLSTM_DV

    @pl.when(pl.program_id(1) == 0)
    def _():
        c_ref[...] = jnp.zeros_like(c_ref)
        n_ref[...] = jnp.zeros_like(n_ref)
        m_ref[...] = jnp.zeros_like(m_ref)

    tri, _, tri_u = _tri_masks(CHUNK)
    pre_c = gc_ref[...] + bc_ref[...]
    logf_c = -_softplus(-pre_c)
    bcum_c = _cumsum_rows(tri, logf_c)
    pre_r = gr_ref[...] + br_ref[...]
    logf_r = -_softplus(-pre_r)
    bcum_r = _cumsum_lanes(logf_r, tri_u)

    for h in range(heads):
        q = q_ref[:, h * dk:(h + 1) * dk] * dk ** -0.5
        k = k_ref[:, h * dk:(h + 1) * dk]
        v = v_ref[:, h * dv:(h + 1) * dv]
        ig_col = pre_c[:, h:h + 1]
        ig_row = pre_r[h:h + 1, :]
        b_col = bcum_c[:, heads + h:heads + h + 1]
        b_row = bcum_r[heads + h:heads + h + 1, :]
        b_last = b_col[CHUNK - 1:CHUNK, :]
        m = m_ref[h][0:1, 0:1]
        intra_log = jnp.where(tri, b_col - b_row + ig_row, -jnp.inf)
        intra_max = jnp.max(intra_log, axis=-1, keepdims=True)
        upd_log = b_last - b_col + ig_col
        upd_max = jnp.max(upd_log, axis=0, keepdims=True)
        inter_log = b_col + m
        m_s = jnp.maximum(inter_log, intra_max)
        inter_w = jnp.exp(inter_log - m_s)
        s = _mm_nt(q, k) * jnp.exp(intra_log - m_s)
        c_mat = c_ref[h]
        n_vec = n_ref[h][0:1, :]
        num = inter_w * _mm(q, c_mat) + _mm(s, v)
        den = inter_w * jnp.sum(q * n_vec, axis=-1, keepdims=True) + jnp.sum(s, axis=-1, keepdims=True)
        hid = num / jnp.maximum(jnp.abs(den), jnp.exp(-m_s))
        m_new = jnp.maximum(b_last + m, upd_max)
        carry_w = jnp.exp(b_last + m - m_new)
        kw = k * jnp.exp(upd_log - m_new)
        c_ref[h] = carry_w * c_mat + _mm_tn(kw, v)
        n_ref[h] = jnp.broadcast_to(carry_w * n_vec + jnp.sum(kw, axis=0, keepdims=True), (SUBLANES, dk))
        m_ref[h] = jnp.broadcast_to(m_new, (SUBLANES, LANES))
        gate = jax.nn.sigmoid(op_ref[:, h * dv:(h + 1) * dv])
        o_ref[:, h * dv:(h + 1) * dv] = (gate * _rms_norm(hid, ng_ref[...])).astype(BF16)


def _mlstm_mixer(zin, gates, gate_bias, norm_g):
    batch, seq, _ = zin.shape
    heads = MLSTM_HEADS
    qk_w, v_w = heads * MLSTM_DK, heads * MLSTM_DV
    bias = gate_bias.reshape(2 * heads)
    tok = lambda w, blk: pl.BlockSpec((None, CHUNK, w), lambda b, n: (b, n, blk))
    const = lambda *shape: pl.BlockSpec(shape, lambda b, n: (0,) * len(shape))
    return pl.pallas_call(
        _mlstm_kernel,
        grid=(batch, seq // CHUNK),
        in_specs=[
            tok(qk_w, 0), tok(qk_w, 1), tok(v_w, 1), tok(v_w, 2),
            pl.BlockSpec((None, CHUNK, LANES), lambda b, n: (b, n, 0)),
            pl.BlockSpec((None, None, 2 * heads, CHUNK), lambda b, n: (b, n, 0, 0)),
            const(1, LANES), const(2 * heads, 1), const(1, MLSTM_DV),
        ],
        out_specs=pl.BlockSpec((None, CHUNK, v_w), lambda b, n: (b, n, 0)),
        out_shape=jax.ShapeDtypeStruct((batch, seq, v_w), BF16),
        scratch_shapes=[
            pltpu.VMEM((heads, MLSTM_DK, MLSTM_DV), F32),
            pltpu.VMEM((heads, SUBLANES, MLSTM_DK), F32),
            pltpu.VMEM((heads, SUBLANES, LANES), F32),
        ],
        compiler_params=_params("parallel", "arbitrary"),
        name="mlstm",
    )(zin, zin, zin, zin, gates, _gate_rows(gates, 2 * heads), _lane_row(bias, 0),
      bias.reshape(2 * heads, 1), norm_g.reshape(1, MLSTM_DV))


def _ret_kernel(q_ref, k_ref, v_ref, g_ref, pos_ref, freq_ref, dmat_ref, xi_ref, zeta_ref, gam_ref, ng_ref,
                o_ref, s_ref):
    heads, dk, dv = RET_HEADS, RET_DK, RET_DV
    half = dk // 2

    @pl.when(pl.program_id(1) == 0)
    def _():
        s_ref[...] = jnp.zeros_like(s_ref)

    ang = pos_ref[...].astype(F32) * freq_ref[...]
    cos, sin = jnp.cos(ang), jnp.sin(ang)
    lane = lax.broadcasted_iota(jnp.int32, (CHUNK, dk), 1)
    sin_signed = jnp.where(lane < half, -sin, sin)

    def rotary(x):
        return x * cos + pltpu.roll(x, half, 1) * sin_signed

    for h in range(heads):
        q = rotary(q_ref[:, h * dk:(h + 1) * dk]) * dk ** -0.5
        k = rotary(k_ref[:, h * dk:(h + 1) * dk])
        v = v_ref[:, h * dv:(h + 1) * dv]
        state = s_ref[h]
        intra = _mm(_mm_nt(q, k) * dmat_ref[h], v)
        inter = _mm(q, state) * xi_ref[h]
        s_ref[h] = state * gam_ref[h] + _mm_tn(k * zeta_ref[h], v)
        y = intra + inter
        mu = jnp.mean(y, axis=-1, keepdims=True)
        yc = y - mu
        var = jnp.mean(yc * yc, axis=-1, keepdims=True)
        y = yc * lax.rsqrt(var + EPS) * ng_ref[...]
        o_ref[:, h * dv:(h + 1) * dv] = (y * _silu(g_ref[:, h * dv:(h + 1) * dv])).astype(BF16)


def _ret_mixer(zin, positions, norm_g, *, col0):
    batch, seq, _ = zin.shape
    heads, dk = RET_HEADS, RET_DK
    qk_w, v_w = heads * RET_DK, heads * RET_DV
    assert col0 % v_w == 0
    cq, cv = col0 // qk_w, col0 // v_w
    half = dk // 2
    inv_freq = ROPE_BASE ** (-jnp.arange(half, dtype=F32) / half)
    freq = jnp.concatenate([inv_freq, inv_freq]).reshape(1, dk)
    log_gamma = jnp.log1p(-jnp.exp2(-5.0 - jnp.arange(heads, dtype=F32)))
    idx = jnp.arange(CHUNK, dtype=F32)
    tri = jnp.tril(jnp.ones((CHUNK, CHUNK), dtype=bool))
    diff = jnp.where(tri, idx[:, None] - idx[None, :], 0.0)
    dmat = jnp.where(tri, jnp.exp(diff * log_gamma[:, None, None]), 0.0)
    xi = jnp.exp((idx + 1.0) * log_gamma[:, None])[:, :, None]
    zeta = jnp.exp((CHUNK - 1.0 - idx) * log_gamma[:, None])[:, :, None]
    gamma_c = jnp.broadcast_to(jnp.exp(CHUNK * log_gamma)[:, None, None], (heads, 1, RET_DV))
    pos = positions.reshape(batch, seq // CHUNK, CHUNK, 1)
    tok = lambda w, blk: pl.BlockSpec((None, CHUNK, w), lambda b, n: (b, n, blk))
    const = lambda *shape: pl.BlockSpec(shape, lambda b, n: (0,) * len(shape))
    return pl.pallas_call(
        _ret_kernel,
        grid=(batch, seq // CHUNK),
        in_specs=[
            tok(qk_w, cq), tok(qk_w, cq + 1), tok(v_w, cv + 1), tok(v_w, cv + 2),
            pl.BlockSpec((None, None, CHUNK, 1), lambda b, n: (b, n, 0, 0)),
            const(1, dk), const(heads, CHUNK, CHUNK), const(heads, CHUNK, 1), const(heads, CHUNK, 1),
            const(heads, 1, RET_DV), const(1, RET_DV),
        ],
        out_specs=pl.BlockSpec((None, CHUNK, v_w), lambda b, n: (b, n, 0)),
        out_shape=jax.ShapeDtypeStruct((batch, seq, v_w), BF16),
        scratch_shapes=[pltpu.VMEM((heads, RET_DK, RET_DV), F32)],
        compiler_params=_params("parallel", "arbitrary"),
        name="retention",
    )(zin, zin, zin, zin, pos, freq, dmat, xi, zeta, gamma_c, norm_g.reshape(1, RET_DV))


def _pad_lanes(w):
    return jnp.pad(w, ((0, 0), (0, LANES - w.shape[1])))


def _even_mixer(x, batch, norm_g, w_in, w_out, a_re, a_im, log_step, b_re, b_im, c_re, c_im, d_skip, w_glu,
                b_glu, conv_w, a_log, dt_bias, gdn_g):
    s5_w = a_re.shape[0] * S5_GROUP
    main_w = s5_w + GDN_HEADS * (2 * GDN_DK + 2 * GDN_DV)
    zin, gates = _inproj(x, norm_g, w_in[:, :main_w].astype(BF16), _pad_lanes(w_in[:, main_w:]).astype(BF16))
    zin = zin.reshape(batch, -1, main_w)
    gates = gates.reshape(batch, -1, LANES)
    ya = _s5_mixer(zin, a_re, a_im, log_step, b_re, b_im, c_re, c_im, d_skip, w_glu, b_glu)
    yb = _gdn_mixer(zin, gates, conv_w, a_log, dt_bias, gdn_g, col0=s5_w)
    m = x.shape[0]
    w_out = w_out.astype(BF16)
    return _outproj(x, ya.reshape(m, -1), yb.reshape(m, -1), w_out[:s5_w], w_out[s5_w:])


def _odd_mixer(x, batch, positions, norm_g, w_in, w_out, gate_bias, mlstm_g, ret_g):
    c_main = MLSTM_HEADS * (2 * MLSTM_DK + 2 * MLSTM_DV)
    n_gate = 2 * MLSTM_HEADS
    w_main = jnp.concatenate([w_in[:, :c_main], w_in[:, c_main + n_gate:]], axis=1).astype(BF16)
    w_gate = _pad_lanes(w_in[:, c_main:c_main + n_gate]).astype(BF16)
    zin, gates = _inproj(x, norm_g, w_main, w_gate)
    zin = zin.reshape(batch, -1, w_main.shape[1])
    gates = gates.reshape(batch, -1, LANES)
    yc = _mlstm_mixer(zin, gates, gate_bias, mlstm_g)
    yd = _ret_mixer(zin, positions, ret_g, col0=c_main)
    m = x.shape[0]
    w_out = w_out.astype(BF16)
    split = MLSTM_HEADS * MLSTM_DV
    return _outproj(x, yc.reshape(m, -1), yd.reshape(m, -1), w_out[:split], w_out[split:])


def kernel(x, positions, ffn_norm, ffn_w1, ffn_w3, ffn_w2, mix_norm, even_w_in, even_w_out, s5_a_re, s5_a_im, s5_log_step, s5_b_re, s5_b_im, s5_c_re, s5_c_im, s5_d, s5_w_glu, s5_b_glu, gdn_conv_w, gdn_a_log, gdn_dt_bias, gdn_norm, odd_w_in, odd_w_out, mlstm_gate_bias, mlstm_norm, ret_norm, final_norm):
    batch, seq, d = x.shape
    depth = ffn_norm.shape[0]
    x = x.reshape(batch * seq, d)

    def ffn(x, layer, which, final_g=None):
        return _ffn(x, ffn_norm[layer, which], ffn_w1[layer, which].astype(BF16), ffn_w3[layer, which].astype(BF16),
                    ffn_w2[layer, which].astype(BF16), final_g)

    for layer in range(depth):
        x = ffn(x, layer, 0)
        j = layer // 2
        if layer % 2 == 0:
            x = _even_mixer(x, batch, mix_norm[layer], even_w_in[j], even_w_out[j], s5_a_re[j], s5_a_im[j],
                            s5_log_step[j], s5_b_re[j], s5_b_im[j], s5_c_re[j], s5_c_im[j], s5_d[j].reshape(-1),
                            s5_w_glu[j], s5_b_glu[j], gdn_conv_w[j], gdn_a_log[j], gdn_dt_bias[j], gdn_norm[j])
        else:
            x = _odd_mixer(x, batch, positions, mix_norm[layer], odd_w_in[j], odd_w_out[j], mlstm_gate_bias[j],
                           mlstm_norm[j], ret_norm[j])
        x = ffn(x, layer, 1, final_norm if layer == depth - 1 else None)
    return x.reshape(batch, seq, d)
```

```python
import functools
import math

import jax
import jax.numpy as jnp
from jax import lax
from jax.experimental import pallas as pl
from jax.experimental.pallas import tpu as pltpu

F32 = jnp.float32
BF16 = jnp.bfloat16

EPS = 1e-6
CHUNK = 64
ROPE_BASE = 10000.0
LANES = 128
SUBLANES = 8
VMEM_LIMIT_BYTES = 56 * 1024 * 1024

S5_GROUP = 16
S5_STATE = 64
S5_TILE_GROUPS = LANES // S5_GROUP
S5_PAIRS = S5_TILE_GROUPS // 2
S5_TIME = 128
S5_PITCH = S5_TIME + SUBLANES

GDN_HEADS = 8
GDN_DK = 128
GDN_DV = 128
GDN_CONV = 4
MIXER_CHUNKS_PER_STEP = 4
MLSTM_HEADS = 4
MLSTM_DK = 128
MLSTM_DV = 256
RET_HEADS = 4
RET_DK = 128
RET_DV = 256


def _params(*semantics):
    return pltpu.CompilerParams(dimension_semantics=semantics, vmem_limit_bytes=VMEM_LIMIT_BYTES)


def _mm(a, b):
    return jnp.dot(a.astype(BF16), b.astype(BF16), preferred_element_type=F32)


def _mm_nt(a, b):
    return lax.dot_general(a.astype(BF16), b.astype(BF16), (((1,), (1,)), ((), ())), preferred_element_type=F32)


def _mm_tn(a, b):
    return lax.dot_general(a.astype(BF16), b.astype(BF16), (((0,), (0,)), ((), ())), preferred_element_type=F32)


def _split3(x):
    x1 = x.astype(BF16)
    r1 = x - x1.astype(F32)
    x2 = r1.astype(BF16)
    x3 = (r1 - x2.astype(F32)).astype(BF16)
    return x1, x2, x3


def _cumsum_rows(tri_lower, x):
    t = tri_lower.astype(BF16)
    return sum(jnp.dot(t, p, preferred_element_type=F32) for p in _split3(x))


def _cumsum_lanes(x, tri_upper):
    t = tri_upper.astype(BF16)
    return sum(jnp.dot(p, t, preferred_element_type=F32) for p in _split3(x))


def _rms_norm(x, g):
    return x * lax.rsqrt(jnp.mean(x * x, axis=-1, keepdims=True) + EPS) * g


def _silu(x):
    return x * jax.nn.sigmoid(x)


def _softplus(x):
    return jnp.maximum(x, 0.0) + jnp.log1p(jnp.exp(-jnp.abs(x)))


def _tri_masks(n):
    r = lax.broadcasted_iota(jnp.int32, (n, n), 0)
    c = lax.broadcasted_iota(jnp.int32, (n, n), 1)
    return r >= c, r > c, r <= c


def _ffn_kernel(x_ref, g_ref, w1_ref, w3_ref, w2_ref, fg_ref, o_ref, h_ref, acc_ref, *, final_norm):
    j = pl.program_id(1)

    @pl.when(j == 0)
    def _():
        h_ref[...] = _rms_norm(x_ref[...], g_ref[...]).astype(BF16)
        acc_ref[...] = jnp.zeros_like(acc_ref)

    h = h_ref[...]
    a = jnp.dot(h, w1_ref[...], preferred_element_type=F32)
    b = jnp.dot(h, w3_ref[...], preferred_element_type=F32)
    acc_ref[...] += jnp.dot((_silu(a) * b).astype(BF16), w2_ref[...], preferred_element_type=F32)

    @pl.when(j == pl.num_programs(1) - 1)
    def _():
        y = x_ref[...] + 0.5 * acc_ref[...]
        if final_norm:
            y = _rms_norm(y, fg_ref[...])
        o_ref[...] = y


def _ffn(x, g, w1, w3, w2, final_g=None, *, tm=512, tf=512):
    m, d = x.shape
    f = w1.shape[1]
    final_norm = final_g is not None
    fg = final_g if final_norm else g
    return pl.pallas_call(
        functools.partial(_ffn_kernel, final_norm=final_norm),
        grid=(m // tm, f // tf),
        in_specs=[
            pl.BlockSpec((tm, d), lambda i, j: (i, 0)),
            pl.BlockSpec((1, d), lambda i, j: (0, 0)),
            pl.BlockSpec((d, tf), lambda i, j: (0, j)),
            pl.BlockSpec((d, tf), lambda i, j: (0, j)),
            pl.BlockSpec((tf, d), lambda i, j: (j, 0)),
            pl.BlockSpec((1, d), lambda i, j: (0, 0)),
        ],
        out_specs=pl.BlockSpec((tm, d), lambda i, j: (i, 0)),
        out_shape=jax.ShapeDtypeStruct((m, d), F32),
        scratch_shapes=[pltpu.VMEM((tm, d), BF16), pltpu.VMEM((tm, d), F32)],
        compiler_params=_params("parallel", "arbitrary"),
        name="ffn",
    )(x, g.reshape(1, d), w1, w3, w2, fg.reshape(1, d))


def _inproj_kernel(x_ref, g_ref, w_ref, wg_ref, o_ref, og_ref, h_ref):
    @pl.when(pl.program_id(1) == 0)
    def _():
        h = _rms_norm(x_ref[...], g_ref[...]).astype(BF16)
        h_ref[...] = h
        og_ref[...] = jnp.dot(h, wg_ref[...], preferred_element_type=F32)

    o_ref[...] = jnp.dot(h_ref[...], w_ref[...], preferred_element_type=F32)


def _inproj(x, g, w_main, w_gate, *, tm=512, tn=1024):
    m, d = x.shape
    n = w_main.shape[1]
    return pl.pallas_call(
        _inproj_kernel,
        grid=(m // tm, n // tn),
        in_specs=[
            pl.BlockSpec((tm, d), lambda i, j: (i, 0)),
            pl.BlockSpec((1, d), lambda i, j: (0, 0)),
            pl.BlockSpec((d, tn), lambda i, j: (0, j)),
            pl.BlockSpec((d, LANES), lambda i, j: (0, 0)),
        ],
        out_specs=[
            pl.BlockSpec((tm, tn), lambda i, j: (i, j)),
            pl.BlockSpec((tm, LANES), lambda i, j: (i, 0)),
        ],
        out_shape=[jax.ShapeDtypeStruct((m, n), F32), jax.ShapeDtypeStruct((m, LANES), F32)],
        scratch_shapes=[pltpu.VMEM((tm, d), BF16)],
        compiler_params=_params("parallel", "arbitrary"),
        name="inproj",
    )(x, g.reshape(1, d), w_main, w_gate)


def _outproj_kernel(x_ref, ya_ref, yb_ref, wa_ref, wb_ref, o_ref):
    o_ref[...] = (x_ref[...] + jnp.dot(ya_ref[...], wa_ref[...], preferred_element_type=F32)
                  + jnp.dot(yb_ref[...], wb_ref[...], preferred_element_type=F32))


def _outproj(x, ya, yb, wa, wb, *, tm=1024, tn=1024):
    m, d = x.shape
    ka, kb = ya.shape[1], yb.shape[1]
    return pl.pallas_call(
        _outproj_kernel,
        grid=(m // tm, d // tn),
        in_specs=[
            pl.BlockSpec((tm, tn), lambda i, j: (i, j)),
            pl.BlockSpec((tm, ka), lambda i, j: (i, 0)),
            pl.BlockSpec((tm, kb), lambda i, j: (i, 0)),
            pl.BlockSpec((ka, tn), lambda i, j: (0, j)),
            pl.BlockSpec((kb, tn), lambda i, j: (0, j)),
        ],
        out_specs=pl.BlockSpec((tm, tn), lambda i, j: (i, j)),
        out_shape=jax.ShapeDtypeStruct((m, d), F32),
        compiler_params=_params("parallel", "arbitrary"),
        name="outproj",
    )(x, ya, yb, wa, wb)


def _s5_disc_kernel(are_ref, aim_ref, step_ref, bre_ref, bim_ref, lr_ref, li_ref, bbr_ref, bbi_ref):
    ar, ai = are_ref[...], aim_ref[...]
    step = jnp.exp(step_ref[...])
    mag = jnp.exp(ar * step)
    lr, li = mag * jnp.cos(ai * step), mag * jnp.sin(ai * step)
    den = ar * ar + ai * ai
    fr = ((lr - 1.0) * ar + li * ai) / den
    fi = (li * ar - (lr - 1.0) * ai) / den
    lr_ref[...] = lr
    li_ref[...] = li
    br, bi = bre_ref[...], bim_ref[...]
    bbr_ref[...] = fr[None] * br - fi[None] * bi
    bbi_ref[...] = fr[None] * bi + fi[None] * br


def _s5_discretise(a_re, a_im, log_step, b_re, b_im):
    g, p = a_re.shape
    h = b_re.shape[-1]
    sd = jax.ShapeDtypeStruct
    return pl.pallas_call(
        _s5_disc_kernel,
        out_shape=[sd((g, p), F32), sd((g, p), F32), sd((h, g, p), F32), sd((h, g, p), F32)],
        name="s5_discretise",
    )(a_re, a_im, log_step.reshape(g, 1), b_re.transpose(2, 0, 1), b_im.transpose(2, 0, 1))


def _s5_kernel(u_ref, wb_ref, lr_ref, li_ref, cs_ref, dsk_ref, wglu_ref, bglu_ref, o_ref,
               rre_ref, rim_ref, xr_ref, xi_ref, y_ref, *, batch, tiles):
    chains = tiles * S5_PAIRS * batch

    @pl.when(pl.program_id(0) == 0)
    def _():
        xr_ref[...] = jnp.zeros_like(xr_ref)
        xi_ref[...] = jnp.zeros_like(xi_ref)

    def chain_rows(j, pair, b):
        return pl.ds(((j * S5_PAIRS + pair) * batch + b) * S5_PITCH, S5_TIME)

    for b in range(batch):
        for j in range(tiles):
            drive = jnp.dot(u_ref[b, :, j * LANES:(j + 1) * LANES].astype(BF16), wb_ref[j],
                            preferred_element_type=F32)
            for pair in range(S5_PAIRS):
                rre_ref[chain_rows(j, pair, b), :] = drive[:, pair * LANES:(pair + 1) * LANES]
                rim_ref[chain_rows(j, pair, b), :] = drive[:, (S5_PAIRS + pair) * LANES:(S5_PAIRS + pair + 1) * LANES]

    lr, li = lr_ref[...], li_ref[...]

    def step(t, carry):
        xr, xi = carry
        rows = pl.ds(t, chains, stride=S5_PITCH)
        nxr = lr * xr - li * xi + rre_ref[rows, :]
        nxi = lr * xi + li * xr + rim_ref[rows, :]
        rre_ref[rows, :] = nxr
        rim_ref[rows, :] = nxi
        return nxr, nxi

    xr, xi = lax.fori_loop(0, S5_TIME, step, (xr_ref[...], xi_ref[...]), unroll=2)
    xr_ref[...] = xr
    xi_ref[...] = xi

    for b in range(batch):
        for j in range(tiles):
            states = jnp.concatenate(
                [rre_ref[chain_rows(j, pair, b), :].astype(BF16) for pair in range(S5_PAIRS)]
                + [rim_ref[chain_rows(j, pair, b), :].astype(BF16) for pair in range(S5_PAIRS)], axis=1)
            cols = slice(j * LANES, (j + 1) * LANES)
            y = jnp.dot(states, cs_ref[j], preferred_element_type=F32) + dsk_ref[:, cols] * u_ref[b, :, cols]
            y_ref[b, :, cols] = jax.nn.gelu(y)

    for b in range(batch):
        y = y_ref[b]
        gate = jnp.dot(y.astype(BF16), wglu_ref[...], preferred_element_type=F32) + bglu_ref[...]
        o_ref[b] = (y * jax.nn.sigmoid(gate)).astype(BF16)


def _s5_mixer(zin, a_re, a_im, log_step, b_re, b_im, c_re, c_im, d_skip, w_glu, b_glu):
    batch, seq, _ = zin.shape
    groups, states = a_re.shape
    width = groups * S5_GROUP
    tiles = width // LANES
    chains = tiles * S5_PAIRS * batch
    lr, li, bbr, bbi = _s5_discretise(a_re, a_im, log_step, b_re, b_im)

    eye = jnp.eye(S5_TILE_GROUPS, dtype=F32)
    bb = jnp.stack([bbr, bbi]).reshape(2, S5_GROUP, tiles, S5_TILE_GROUPS, states)
    wb = jnp.einsum("rhjgp,gk->jghrkp", bb, eye).reshape(tiles, LANES, 2 * S5_TILE_GROUPS * states).astype(BF16)
    cc = jnp.stack([c_re, -c_im]).reshape(2, tiles, S5_TILE_GROUPS, S5_GROUP, states)
    cs = jnp.einsum("rjghp,gk->jrkpgh", cc, eye).reshape(tiles, 2 * S5_TILE_GROUPS * states, LANES).astype(BF16)

    def per_chain(lam):
        t = lam.reshape(tiles * S5_PAIRS, 1, 2 * states)
        return jnp.broadcast_to(t, (tiles * S5_PAIRS, batch, 2 * states)).reshape(chains, 2 * states)

    const = lambda *shape: pl.BlockSpec(shape, lambda i: (0,) * len(shape))
    return pl.pallas_call(
        functools.partial(_s5_kernel, batch=batch, tiles=tiles),
        grid=(seq // S5_TIME,),
        in_specs=[
            pl.BlockSpec((batch, S5_TIME, width), lambda i: (0, i, 0)),
            const(tiles, LANES, 2 * S5_TILE_GROUPS * states),
            const(chains, LANES),
            const(chains, LANES),
            const(tiles, 2 * S5_TILE_GROUPS * states, LANES),
            const(1, width),
            const(width, width),
            const(1, width),
        ],
        out_specs=pl.BlockSpec((batch, S5_TIME, width), lambda i: (0, i, 0)),
        out_shape=jax.ShapeDtypeStruct((batch, seq, width), BF16),
        scratch_shapes=[
            pltpu.VMEM((chains * S5_PITCH, LANES), F32),
            pltpu.VMEM((chains * S5_PITCH, LANES), F32),
            pltpu.VMEM((chains, LANES), F32),
            pltpu.VMEM((chains, LANES), F32),
            pltpu.VMEM((batch, S5_TIME, width), F32),
        ],
        compiler_params=_params("arbitrary"),
        name="s5",
    )(zin, wb, per_chain(lr), per_chain(li), cs, d_skip.reshape(1, width), w_glu.astype(BF16),
      b_glu.reshape(1, width))


def _shift_rows(cur, tail, s):
    rolled = pltpu.roll(cur, s, 0)
    rolled_tail = pltpu.roll(tail, s, 0)
    row = lax.broadcasted_iota(jnp.int32, tail.shape, 0)
    top = jnp.where(row < s, rolled_tail, rolled[:SUBLANES])
    return jnp.concatenate([top, rolled[SUBLANES:]], axis=0)


def _gdn_kernel(q_ref, k_ref, v_ref, z_ref, gc_ref, gr_ref, cw_ref, alc_ref, dtc_ref, alr_ref, dtr_ref,
                ng_ref, o_ref, s_ref, tail_ref, *, cps):
    heads, dk, dv = GDN_HEADS, GDN_DK, GDN_DV
    steps = cps * CHUNK

    @pl.when(pl.program_id(1) == 0)
    def _():
        s_ref[...] = jnp.zeros_like(s_ref)
        tail_ref[...] = jnp.zeros_like(tail_ref)

    tri, strict, tri_u = _tri_masks(CHUNK)
    row = lax.broadcasted_iota(jnp.int32, (steps, steps), 0)
    col = lax.broadcasted_iota(jnp.int32, (steps, steps), 1)
    tri_chunks = (row >= col) & (row // CHUNK == col // CHUNK)
    eye = (lax.broadcasted_iota(jnp.int32, (CHUNK, CHUNK), 0)
           == lax.broadcasted_iota(jnp.int32, (CHUNK, CHUNK), 1)).astype(F32)

    conv = []
    for idx, ref in enumerate((q_ref, k_ref, v_ref)):
        cur = ref[...]
        width = cur.shape[1]
        tail = tail_ref[idx]
        cw = cw_ref[:, idx * width:(idx + 1) * width]
        acc = cur * cw[GDN_CONV - 1:GDN_CONV]
        for s in range(1, GDN_CONV):
            acc = acc + _shift_rows(cur, tail, s) * cw[GDN_CONV - 1 - s:GDN_CONV - s]
        tail_ref[idx] = cur[steps - SUBLANES:]
        conv.append(_silu(acc))
    q_all, k_all, v_all = conv

    gc = gc_ref[...]
    beta_c = jax.nn.sigmoid(gc)
    g_c = -jnp.exp(alc_ref[...]) * _softplus(gc + dtc_ref[...])
    gcum_c = _cumsum_rows(tri_chunks, g_c)
    gr = gr_ref[...].reshape(cps * 2 * heads, CHUNK)
    g_r = -jnp.exp(alr_ref[...]) * _softplus(gr + dtr_ref[...])
    gcum_r = _cumsum_lanes(g_r, tri_u)

    items = [(c, h) for c in range(cps) for h in range(heads)]
    pre = {}
    for c, h in items:
        rows = slice(c * CHUNK, (c + 1) * CHUNK)
        q = q_all[rows, h * dk:(h + 1) * dk]
        k = k_all[rows, h * dk:(h + 1) * dk]
        v = v_all[rows, h * dv:(h + 1) * dv]
        q = q * lax.rsqrt(jnp.sum(q * q, axis=-1, keepdims=True) + EPS) * dk ** -0.5
        k = k * lax.rsqrt(jnp.sum(k * k, axis=-1, keepdims=True) + EPS)
        beta = beta_c[rows, h:h + 1]
        gcol = gcum_c[rows, heads + h:heads + h + 1]
        grow = gcum_r[c * 2 * heads + heads + h:c * 2 * heads + heads + h + 1, :]
        glast = gcol[CHUNK - 1:CHUNK, :]
        decay = jnp.where(tri, jnp.exp(gcol - grow), 0.0)
        egc = jnp.exp(gcol)
        kb = k * beta
        pre[c, h] = dict(
            lower=jnp.where(strict, _mm_nt(kb, k) * decay, 0.0),
            rhs=jnp.concatenate([v * beta, kb * egc], axis=1).astype(BF16),
            qk=jnp.where(tri, _mm_nt(q, k) * decay, 0.0).astype(BF16),
            q_dec=(q * egc).astype(BF16),
            k_dec=(k * jnp.exp(glast - gcol)).astype(BF16),
            carry=jnp.exp(glast))

    power = {it: -pre[it]["lower"] for it in items}
    inv = {it: eye + power[it] for it in items}
    for _ in range(int(math.log2(CHUNK)) - 1):
        for it in items:
            power[it] = _mm(power[it], power[it])
        for it in items:
            inv[it] = inv[it] + _mm(inv[it], power[it])
    sol = {it: _mm(inv[it], pre[it]["rhs"]) for it in items}

    for c in range(cps):
        rows = slice(c * CHUNK, (c + 1) * CHUNK)
        state = [s_ref[h] for h in range(heads)]
        v_new = [sol[c, h][:, :dv] - _mm(sol[c, h][:, dv:], state[h]) for h in range(heads)]
        out = [_mm(pre[c, h]["q_dec"], state[h]) + _mm(pre[c, h]["qk"], v_new[h]) for h in range(heads)]
        for h in range(heads):
            s_ref[h] = state[h] * pre[c, h]["carry"] + _mm_tn(pre[c, h]["k_dec"], v_new[h])
        for h in range(heads):
            o = _rms_norm(out[h], ng_ref[...]) * _silu(z_ref[rows, h * dv:(h + 1) * dv])
            o_ref[rows, h * dv:(h + 1) * dv] = o.astype(BF16)


def _gate_rows(gates, count):
    b, l, _ = gates.shape
    return gates[:, :, :count].reshape(b, l // CHUNK, CHUNK, count).transpose(0, 1, 3, 2)


def _lane_row(vec, offset):
    return jnp.zeros((1, LANES), F32).at[0, offset:offset + vec.shape[0]].set(vec)


def _gdn_mixer(zin, gates, conv_w, a_log, dt_bias, norm_g, *, col0, cps=MIXER_CHUNKS_PER_STEP):
    batch, seq, _ = zin.shape
    heads = GDN_HEADS
    qk_w, v_w = heads * GDN_DK, heads * GDN_DV
    assert qk_w == v_w and col0 % qk_w == 0
    c0 = col0 // qk_w
    steps = cps * CHUNK
    zeros = jnp.zeros((heads,), F32)
    al_c, dt_c = _lane_row(a_log, heads), _lane_row(dt_bias, heads)
    al_r = jnp.tile(jnp.concatenate([zeros, a_log]), cps).reshape(cps * 2 * heads, 1)
    dt_r = jnp.tile(jnp.concatenate([zeros, dt_bias]), cps).reshape(cps * 2 * heads, 1)
    tok = lambda blk: pl.BlockSpec((None, steps, qk_w), lambda b, n, blk=blk: (b, n, c0 + blk))
    const = lambda *shape: pl.BlockSpec(shape, lambda b, n: (0,) * len(shape))
    return pl.pallas_call(
        functools.partial(_gdn_kernel, cps=cps),
        grid=(batch, seq // steps),
        in_specs=[
            tok(0), tok(1), tok(2), tok(3),
            pl.BlockSpec((None, steps, LANES), lambda b, n: (b, n, 0)),
            pl.BlockSpec((None, cps, 2 * heads, CHUNK), lambda b, n: (b, n, 0, 0)),
            const(GDN_CONV, 2 * qk_w + v_w),
            const(1, LANES), const(1, LANES), const(cps * 2 * heads, 1), const(cps * 2 * heads, 1),
            const(1, GDN_DV),
        ],
        out_specs=pl.BlockSpec((None, steps, v_w), lambda b, n: (b, n, 0)),
        out_shape=jax.ShapeDtypeStruct((batch, seq, v_w), BF16),
        scratch_shapes=[
            pltpu.VMEM((heads, GDN_DK, GDN_DV), F32),
            pltpu.VMEM((3, SUBLANES, qk_w), F32),
        ],
        compiler_params=_params("parallel", "arbitrary"),
        name="gdn",
    )(zin, zin, zin, zin, gates, _gate_rows(gates, 2 * heads), conv_w, al_c, dt_c, al_r, dt_r,
      norm_g.reshape(1, GDN_DV))


def _mlstm_kernel(q_ref, k_ref, v_ref, op_ref, gc_ref, gr_ref, bc_ref, br_ref, ng_ref, o_ref,
                  c_ref, n_ref, m_ref, *, cps):
    heads, dk, dv = MLSTM_HEADS, MLSTM_DK, MLSTM_DV
    steps = cps * CHUNK

    @pl.when(pl.program_id(1) == 0)
    def _():
        c_ref[...] = jnp.zeros_like(c_ref)
        n_ref[...] = jnp.zeros_like(n_ref)
        m_ref[...] = jnp.zeros_like(m_ref)

    tri, _, tri_u = _tri_masks(CHUNK)
    row = lax.broadcasted_iota(jnp.int32, (steps, steps), 0)
    col = lax.broadcasted_iota(jnp.int32, (steps, steps), 1)
    tri_chunks = (row >= col) & (row // CHUNK == col // CHUNK)
    pre_c = gc_ref[...] + bc_ref[...]
    bcum_c = _cumsum_rows(tri_chunks, -_softplus(-pre_c))
    pre_r = gr_ref[...].reshape(cps * 2 * heads, CHUNK) + br_ref[...]
    bcum_r = _cumsum_lanes(-_softplus(-pre_r), tri_u)

    items = [(c, h) for c in range(cps) for h in range(heads)]
    gate = {}
    for c, h in items:
        rows = slice(c * CHUNK, (c + 1) * CHUNK)
        ig_col = pre_c[rows, h:h + 1]
        ig_row = pre_r[c * 2 * heads + h:c * 2 * heads + h + 1, :]
        b_col = bcum_c[rows, heads + h:heads + h + 1]
        b_row = bcum_r[c * 2 * heads + heads + h:c * 2 * heads + heads + h + 1, :]
        b_last = b_col[CHUNK - 1:CHUNK, :]
        intra_log = jnp.where(tri, b_col - b_row + ig_row, -jnp.inf)
        upd_log = b_last - b_col + ig_col
        gate[c, h] = dict(b_col=b_col, b_last=b_last, intra_log=intra_log, upd_log=upd_log,
                          intra_max=jnp.max(intra_log, axis=-1, keepdims=True),
                          upd_max=jnp.max(upd_log, axis=0, keepdims=True))

    m_run = [m_ref[h][0:1, 0:1] for h in range(heads)]
    for c, h in items:
        g = gate[c, h]
        m_new = jnp.maximum(g["b_last"] + m_run[h], g["upd_max"])
        g.update(m_in=m_run[h], m_out=m_new, carry=jnp.exp(g["b_last"] + m_run[h] - m_new))
        m_run[h] = m_new

    q_s = {it: q_ref[it[0] * CHUNK:(it[0] + 1) * CHUNK, it[1] * dk:(it[1] + 1) * dk] * dk ** -0.5 for it in items}
    k_s = {it: k_ref[it[0] * CHUNK:(it[0] + 1) * CHUNK, it[1] * dk:(it[1] + 1) * dk] for it in items}
    v_s = {it: v_ref[it[0] * CHUNK:(it[0] + 1) * CHUNK, it[1] * dv:(it[1] + 1) * dv].astype(BF16) for it in items}
    qk = {it: _mm_nt(q_s[it], k_s[it]) for it in items}
    m_s, inter_w, s_mat = {}, {}, {}
    for it in items:
        g = gate[it]
        inter_log = g["b_col"] + g["m_in"]
        m_s[it] = jnp.maximum(inter_log, g["intra_max"])
        inter_w[it] = jnp.exp(inter_log - m_s[it])
        s_mat[it] = qk[it] * jnp.exp(g["intra_log"] - m_s[it])
    s_v = {it: _mm(s_mat[it], v_s[it]) for it in items}
    s_sum = {it: jnp.sum(s_mat[it], axis=-1, keepdims=True) for it in items}
    kw = {it: k_s[it] * jnp.exp(gate[it]["upd_log"] - gate[it]["m_out"]) for it in items}
    kw_v = {it: _mm_tn(kw[it], v_s[it]) for it in items}
    kw_sum = {it: jnp.sum(kw[it], axis=0, keepdims=True) for it in items}

    for c in range(cps):
        rows = slice(c * CHUNK, (c + 1) * CHUNK)
        c_mat = [c_ref[h] for h in range(heads)]
        n_vec = [n_ref[h][0:1, :] for h in range(heads)]
        q_c = [_mm(q_s[c, h], c_mat[h]) for h in range(heads)]
        for h in range(heads):
            c_ref[h] = gate[c, h]["carry"] * c_mat[h] + kw_v[c, h]
            n_ref[h] = jnp.broadcast_to(gate[c, h]["carry"] * n_vec[h] + kw_sum[c, h], (SUBLANES, dk))
        for h in range(heads):
            it = (c, h)
            num = inter_w[it] * q_c[h] + s_v[it]
            den = inter_w[it] * jnp.sum(q_s[it] * n_vec[h], axis=-1, keepdims=True) + s_sum[it]
            hid = num / jnp.maximum(jnp.abs(den), jnp.exp(-m_s[it]))
            out_gate = jax.nn.sigmoid(op_ref[rows, h * dv:(h + 1) * dv])
            o_ref[rows, h * dv:(h + 1) * dv] = (out_gate * _rms_norm(hid, ng_ref[...])).astype(BF16)
    for h in range(heads):
        m_ref[h] = jnp.broadcast_to(m_run[h], (SUBLANES, LANES))


def _mlstm_mixer(zin, gates, gate_bias, norm_g, *, cps=MIXER_CHUNKS_PER_STEP):
    batch, seq, _ = zin.shape
    heads = MLSTM_HEADS
    qk_w, v_w = heads * MLSTM_DK, heads * MLSTM_DV
    steps = cps * CHUNK
    bias = gate_bias.reshape(2 * heads)
    tok = lambda w, blk: pl.BlockSpec((None, steps, w), lambda b, n: (b, n, blk))
    const = lambda *shape: pl.BlockSpec(shape, lambda b, n: (0,) * len(shape))
    return pl.pallas_call(
        functools.partial(_mlstm_kernel, cps=cps),
        grid=(batch, seq // steps),
        in_specs=[
            tok(qk_w, 0), tok(qk_w, 1), tok(v_w, 1), tok(v_w, 2),
            pl.BlockSpec((None, steps, LANES), lambda b, n: (b, n, 0)),
            pl.BlockSpec((None, cps, 2 * heads, CHUNK), lambda b, n: (b, n, 0, 0)),
            const(1, LANES), const(cps * 2 * heads, 1), const(1, MLSTM_DV),
        ],
        out_specs=pl.BlockSpec((None, steps, v_w), lambda b, n: (b, n, 0)),
        out_shape=jax.ShapeDtypeStruct((batch, seq, v_w), BF16),
        scratch_shapes=[
            pltpu.VMEM((heads, MLSTM_DK, MLSTM_DV), F32),
            pltpu.VMEM((heads, SUBLANES, MLSTM_DK), F32),
            pltpu.VMEM((heads, SUBLANES, LANES), F32),
        ],
        compiler_params=_params("parallel", "arbitrary"),
        name="mlstm",
    )(zin, zin, zin, zin, gates, _gate_rows(gates, 2 * heads), _lane_row(bias, 0),
      jnp.tile(bias, cps).reshape(cps * 2 * heads, 1), norm_g.reshape(1, MLSTM_DV))


def _ret_kernel(q_ref, k_ref, v_ref, g_ref, pos_ref, freq_ref, dmat_ref, xi_ref, zeta_ref, gam_ref, ng_ref,
                o_ref, s_ref, *, cps):
    heads, dk, dv = RET_HEADS, RET_DK, RET_DV
    half = dk // 2
    steps = cps * CHUNK

    @pl.when(pl.program_id(1) == 0)
    def _():
        s_ref[...] = jnp.zeros_like(s_ref)

    ang = pos_ref[...].reshape(steps, 1).astype(F32) * freq_ref[...]
    cos, sin = jnp.cos(ang), jnp.sin(ang)
    lane = lax.broadcasted_iota(jnp.int32, (steps, dk), 1)
    sin_signed = jnp.where(lane < half, -sin, sin)

    def rotary(x):
        return x * cos + pltpu.roll(x, half, 1) * sin_signed

    q_rot = [rotary(q_ref[:, h * dk:(h + 1) * dk]) * dk ** -0.5 for h in range(heads)]
    k_rot = [rotary(k_ref[:, h * dk:(h + 1) * dk]) for h in range(heads)]
    items = [(c, h) for c in range(cps) for h in range(heads)]
    rows = {c: slice(c * CHUNK, (c + 1) * CHUNK) for c in range(cps)}
    q_s = {(c, h): q_rot[h][rows[c]].astype(BF16) for c, h in items}
    k_s = {(c, h): k_rot[h][rows[c]] for c, h in items}
    v_s = {(c, h): v_ref[rows[c], h * dv:(h + 1) * dv].astype(BF16) for c, h in items}
    qk = {it: _mm_nt(q_s[it], k_s[it]) * dmat_ref[it[1]] for it in items}
    intra = {it: _mm(qk[it], v_s[it]) for it in items}
    k_v = {it: _mm_tn(k_s[it] * zeta_ref[it[1]], v_s[it]) for it in items}

    for c in range(cps):
        state = [s_ref[h] for h in range(heads)]
        inter = [_mm(q_s[c, h], state[h]) for h in range(heads)]
        for h in range(heads):
            s_ref[h] = state[h] * gam_ref[h] + k_v[c, h]
        for h in range(heads):
            y = intra[c, h] + inter[h] * xi_ref[h]
            mu = jnp.mean(y, axis=-1, keepdims=True)
            yc = y - mu
            var = jnp.mean(yc * yc, axis=-1, keepdims=True)
            y = yc * lax.rsqrt(var + EPS) * ng_ref[...]
            o_ref[rows[c], h * dv:(h + 1) * dv] = (y * _silu(g_ref[rows[c], h * dv:(h + 1) * dv])).astype(BF16)


def _ret_mixer(zin, positions, norm_g, *, col0, cps=MIXER_CHUNKS_PER_STEP):
    batch, seq, _ = zin.shape
    heads, dk = RET_HEADS, RET_DK
    qk_w, v_w = heads * RET_DK, heads * RET_DV
    assert col0 % v_w == 0
    cq, cv = col0 // qk_w, col0 // v_w
    half = dk // 2
    steps = cps * CHUNK
    inv_freq = ROPE_BASE ** (-jnp.arange(half, dtype=F32) / half)
    freq = jnp.concatenate([inv_freq, inv_freq]).reshape(1, dk)
    log_gamma = jnp.log1p(-jnp.exp2(-5.0 - jnp.arange(heads, dtype=F32)))
    idx = jnp.arange(CHUNK, dtype=F32)
    tri = jnp.tril(jnp.ones((CHUNK, CHUNK), dtype=bool))
    diff = jnp.where(tri, idx[:, None] - idx[None, :], 0.0)
    dmat = jnp.where(tri, jnp.exp(diff * log_gamma[:, None, None]), 0.0)
    xi = jnp.exp((idx + 1.0) * log_gamma[:, None])[:, :, None]
    zeta = jnp.exp((CHUNK - 1.0 - idx) * log_gamma[:, None])[:, :, None]
    gamma_c = jnp.broadcast_to(jnp.exp(CHUNK * log_gamma)[:, None, None], (heads, 1, RET_DV))
    pos = positions.reshape(batch, seq // CHUNK, CHUNK, 1)
    tok = lambda w, blk: pl.BlockSpec((None, steps, w), lambda b, n: (b, n, blk))
    const = lambda *shape: pl.BlockSpec(shape, lambda b, n: (0,) * len(shape))
    return pl.pallas_call(
        functools.partial(_ret_kernel, cps=cps),
        grid=(batch, seq // steps),
        in_specs=[
            tok(qk_w, cq), tok(qk_w, cq + 1), tok(v_w, cv + 1), tok(v_w, cv + 2),
            pl.BlockSpec((None, cps, CHUNK, 1), lambda b, n: (b, n, 0, 0)),
            const(1, dk), const(heads, CHUNK, CHUNK), const(heads, CHUNK, 1), const(heads, CHUNK, 1),
            const(heads, 1, RET_DV), const(1, RET_DV),
        ],
        out_specs=pl.BlockSpec((None, steps, v_w), lambda b, n: (b, n, 0)),
        out_shape=jax.ShapeDtypeStruct((batch, seq, v_w), BF16),
        scratch_shapes=[pltpu.VMEM((heads, RET_DK, RET_DV), F32)],
        compiler_params=_params("parallel", "arbitrary"),
        name="retention",
    )(zin, zin, zin, zin, pos, freq, dmat, xi, zeta, gamma_c, norm_g.reshape(1, RET_DV))


def _pad_lanes(w):
    return jnp.pad(w, ((0, 0), (0, LANES - w.shape[1])))


def _even_mixer(x, batch, norm_g, w_in, w_out, a_re, a_im, log_step, b_re, b_im, c_re, c_im, d_skip, w_glu,
                b_glu, conv_w, a_log, dt_bias, gdn_g):
    s5_w = a_re.shape[0] * S5_GROUP
    main_w = s5_w + GDN_HEADS * (2 * GDN_DK + 2 * GDN_DV)
    zin, gates = _inproj(x, norm_g, w_in[:, :main_w].astype(BF16), _pad_lanes(w_in[:, main_w:]).astype(BF16))
    zin = zin.reshape(batch, -1, main_w)
    gates = gates.reshape(batch, -1, LANES)
    ya = _s5_mixer(zin, a_re, a_im, log_step, b_re, b_im, c_re, c_im, d_skip, w_glu, b_glu)
    yb = _gdn_mixer(zin, gates, conv_w, a_log, dt_bias, gdn_g, col0=s5_w)
    m = x.shape[0]
    w_out = w_out.astype(BF16)
    return _outproj(x, ya.reshape(m, -1), yb.reshape(m, -1), w_out[:s5_w], w_out[s5_w:])


def _odd_mixer(x, batch, positions, norm_g, w_in, w_out, gate_bias, mlstm_g, ret_g):
    c_main = MLSTM_HEADS * (2 * MLSTM_DK + 2 * MLSTM_DV)
    n_gate = 2 * MLSTM_HEADS
    w_main = jnp.concatenate([w_in[:, :c_main], w_in[:, c_main + n_gate:]], axis=1).astype(BF16)
    w_gate = _pad_lanes(w_in[:, c_main:c_main + n_gate]).astype(BF16)
    zin, gates = _inproj(x, norm_g, w_main, w_gate)
    zin = zin.reshape(batch, -1, w_main.shape[1])
    gates = gates.reshape(batch, -1, LANES)
    yc = _mlstm_mixer(zin, gates, gate_bias, mlstm_g)
    yd = _ret_mixer(zin, positions, ret_g, col0=c_main)
    m = x.shape[0]
    w_out = w_out.astype(BF16)
    split = MLSTM_HEADS * MLSTM_DV
    return _outproj(x, yc.reshape(m, -1), yd.reshape(m, -1), w_out[:split], w_out[split:])


def kernel(x, positions, ffn_norm, ffn_w1, ffn_w3, ffn_w2, mix_norm, even_w_in, even_w_out, s5_a_re, s5_a_im, s5_log_step, s5_b_re, s5_b_im, s5_c_re, s5_c_im, s5_d, s5_w_glu, s5_b_glu, gdn_conv_w, gdn_a_log, gdn_dt_bias, gdn_norm, odd_w_in, odd_w_out, mlstm_gate_bias, mlstm_norm, ret_norm, final_norm):
    batch, seq, d = x.shape
    depth = ffn_norm.shape[0]
    x = x.reshape(batch * seq, d)

    def ffn(x, layer, which, final_g=None):
        return _ffn(x, ffn_norm[layer, which], ffn_w1[layer, which].astype(BF16), ffn_w3[layer, which].astype(BF16),
                    ffn_w2[layer, which].astype(BF16), final_g)

    for layer in range(depth):
        x = ffn(x, layer, 0)
        j = layer // 2
        if layer % 2 == 0:
            x = _even_mixer(x, batch, mix_norm[layer], even_w_in[j], even_w_out[j], s5_a_re[j], s5_a_im[j],
                            s5_log_step[j], s5_b_re[j], s5_b_im[j], s5_c_re[j], s5_c_im[j], s5_d[j].reshape(-1),
                            s5_w_glu[j], s5_b_glu[j], gdn_conv_w[j], gdn_a_log[j], gdn_dt_bias[j], gdn_norm[j])
        else:
            x = _odd_mixer(x, batch, positions, mix_norm[layer], odd_w_in[j], odd_w_out[j], mlstm_gate_bias[j],
                           mlstm_norm[j], ret_norm[j])
        x = ffn(x, layer, 1, final_norm if layer == depth - 1 else None)
    return x.reshape(batch, seq, d)
```

```python
import functools
import math

import jax
import jax.numpy as jnp
from jax import lax
from jax.experimental import pallas as pl
from jax.experimental.pallas import tpu as pltpu

F32 = jnp.float32
BF16 = jnp.bfloat16

EPS = 1e-6
CHUNK = 64
ROPE_BASE = 10000.0
LANES = 128
SUBLANES = 8
VMEM_LIMIT_BYTES = 60 * 1024 * 1024

S5_GROUP = 16
S5_STATE = 64
S5_TILE_GROUPS = LANES // S5_GROUP
S5_PAIRS = S5_TILE_GROUPS // 2
S5_TIME = 128
S5_PITCH = S5_TIME + SUBLANES

GDN_HEADS = 8
GDN_DK = 128
GDN_DV = 128
GDN_CONV = 4
MIXER_CHUNKS_PER_STEP = 4
MLSTM_HEADS = 4
MLSTM_DK = 128
MLSTM_DV = 256
RET_HEADS = 4
RET_DK = 128
RET_DV = 256


def _params(*semantics):
    return pltpu.CompilerParams(dimension_semantics=semantics, vmem_limit_bytes=VMEM_LIMIT_BYTES)


def _mm(a, b):
    return jnp.dot(a.astype(BF16), b.astype(BF16), preferred_element_type=F32)


def _mm_nt(a, b):
    return lax.dot_general(a.astype(BF16), b.astype(BF16), (((1,), (1,)), ((), ())), preferred_element_type=F32)


def _mm_tn(a, b):
    return lax.dot_general(a.astype(BF16), b.astype(BF16), (((0,), (0,)), ((), ())), preferred_element_type=F32)


def _split3(x):
    x1 = x.astype(BF16)
    r1 = x - x1.astype(F32)
    x2 = r1.astype(BF16)
    x3 = (r1 - x2.astype(F32)).astype(BF16)
    return x1, x2, x3


def _cumsum_rows(tri_lower, x):
    t = tri_lower.astype(BF16)
    return sum(jnp.dot(t, p, preferred_element_type=F32) for p in _split3(x))


def _cumsum_lanes(x, tri_upper):
    t = tri_upper.astype(BF16)
    return sum(jnp.dot(p, t, preferred_element_type=F32) for p in _split3(x))


def _rms_norm(x, g):
    return x * lax.rsqrt(jnp.mean(x * x, axis=-1, keepdims=True) + EPS) * g


def _silu(x):
    return x * jax.nn.sigmoid(x)


def _softplus(x):
    return jnp.maximum(x, 0.0) + jnp.log1p(jnp.exp(-jnp.abs(x)))


def _tri_masks(n):
    r = lax.broadcasted_iota(jnp.int32, (n, n), 0)
    c = lax.broadcasted_iota(jnp.int32, (n, n), 1)
    return r >= c, r > c, r <= c


def _ffn_kernel(x_ref, g_ref, w1_ref, w3_ref, w2_ref, fg_ref, o_ref, h_ref, *, final_norm):
    j = pl.program_id(1)

    @pl.when(j == 0)
    def _():
        x = x_ref[...]
        h_ref[...] = _rms_norm(x, g_ref[...]).astype(BF16)
        o_ref[...] = x

    h = h_ref[...]
    a = jnp.dot(h, w1_ref[...].astype(BF16), preferred_element_type=F32)
    b = jnp.dot(h, w3_ref[...].astype(BF16), preferred_element_type=F32)
    o_ref[...] += 0.5 * jnp.dot((_silu(a) * b).astype(BF16), w2_ref[...].astype(BF16), preferred_element_type=F32)

    if final_norm:
        @pl.when(j == pl.num_programs(1) - 1)
        def _():
            o_ref[...] = _rms_norm(o_ref[...], fg_ref[...])


def _ffn(x, g, w1, w3, w2, layer, which, final_g=None, *, tm=1024, tf=256):
    m, d = x.shape
    f = w1.shape[-1]
    final_norm = final_g is not None
    fg = final_g if final_norm else g
    return pl.pallas_call(
        functools.partial(_ffn_kernel, final_norm=final_norm),
        grid=(m // tm, f // tf),
        in_specs=[
            pl.BlockSpec((tm, d), lambda i, j: (i, 0)),
            pl.BlockSpec((1, d), lambda i, j: (0, 0)),
            pl.BlockSpec((None, None, d, tf), lambda i, j: (layer, which, 0, j)),
            pl.BlockSpec((None, None, d, tf), lambda i, j: (layer, which, 0, j)),
            pl.BlockSpec((None, None, tf, d), lambda i, j: (layer, which, j, 0)),
            pl.BlockSpec((1, d), lambda i, j: (0, 0)),
        ],
        out_specs=pl.BlockSpec((tm, d), lambda i, j: (i, 0)),
        out_shape=jax.ShapeDtypeStruct((m, d), F32),
        scratch_shapes=[pltpu.VMEM((tm, d), BF16)],
        compiler_params=_params("parallel", "arbitrary"),
        name="ffn",
    )(x, g.reshape(1, d), w1, w3, w2, fg.reshape(1, d))


def _inproj_kernel(*refs, bounds):
    x_ref, g_ref = refs[:2]
    w_refs = refs[2:2 + len(bounds)]
    wg_ref, o_ref, og_ref, h_ref = refs[2 + len(bounds):]
    j = pl.program_id(1)

    @pl.when(j == 0)
    def _():
        h = _rms_norm(x_ref[...], g_ref[...]).astype(BF16)
        h_ref[...] = h
        og_ref[...] = jnp.dot(h, wg_ref[...], preferred_element_type=F32)

    for w_ref, (start, count) in zip(w_refs, bounds):
        @pl.when((j >= start) & (j < start + count))
        def _(w_ref=w_ref):
            o_ref[...] = jnp.dot(h_ref[...], w_ref[...].astype(BF16), preferred_element_type=F32)


def _inproj(x, g, segments, w_gate, *, tm=1024, tn=512):
    m, d = x.shape
    bounds, start = [], 0
    for _, _, width in segments:
        bounds.append((start, width // tn))
        start += width // tn
    n = start * tn

    def w_spec(lead, first, count):
        return pl.BlockSpec((None, d, tn), lambda i, j: (lead, 0, jnp.clip(j - first, 0, count - 1)))

    return pl.pallas_call(
        functools.partial(_inproj_kernel, bounds=tuple(bounds)),
        grid=(m // tm, n // tn),
        in_specs=[
            pl.BlockSpec((tm, d), lambda i, j: (i, 0)),
            pl.BlockSpec((1, d), lambda i, j: (0, 0)),
            *[w_spec(lead, first, count) for (_, lead, _), (first, count) in zip(segments, bounds)],
            pl.BlockSpec((d, LANES), lambda i, j: (0, 0)),
        ],
        out_specs=[
            pl.BlockSpec((tm, tn), lambda i, j: (i, j)),
            pl.BlockSpec((tm, LANES), lambda i, j: (i, 0)),
        ],
        out_shape=[jax.ShapeDtypeStruct((m, n), F32), jax.ShapeDtypeStruct((m, LANES), F32)],
        scratch_shapes=[pltpu.VMEM((tm, d), BF16)],
        compiler_params=_params("parallel", "arbitrary"),
        name="inproj",
    )(x, g.reshape(1, d), *[w for w, _, _ in segments], w_gate)


def _outproj_kernel(x_ref, ya_ref, yb_ref, wa_ref, wb_ref, o_ref):
    o_ref[...] = (x_ref[...] + jnp.dot(ya_ref[...], wa_ref[...], preferred_element_type=F32)
                  + jnp.dot(yb_ref[...], wb_ref[...], preferred_element_type=F32))


def _outproj(x, ya, yb, wa, wb, *, tm=1024, tn=1024):
    m, d = x.shape
    ka, kb = ya.shape[1], yb.shape[1]
    return pl.pallas_call(
        _outproj_kernel,
        grid=(m // tm, d // tn),
        in_specs=[
            pl.BlockSpec((tm, tn), lambda i, j: (i, j)),
            pl.BlockSpec((tm, ka), lambda i, j: (i, 0)),
            pl.BlockSpec((tm, kb), lambda i, j: (i, 0)),
            pl.BlockSpec((ka, tn), lambda i, j: (0, j)),
            pl.BlockSpec((kb, tn), lambda i, j: (0, j)),
        ],
        out_specs=pl.BlockSpec((tm, tn), lambda i, j: (i, j)),
        out_shape=jax.ShapeDtypeStruct((m, d), F32),
        compiler_params=_params("parallel", "arbitrary"),
        name="outproj",
    )(x, ya, yb, wa, wb)


def _s5_disc_kernel(are_ref, aim_ref, step_ref, bre_ref, bim_ref, lr_ref, li_ref, bbr_ref, bbi_ref):
    ar, ai = are_ref[...], aim_ref[...]
    step = jnp.exp(step_ref[...])
    mag = jnp.exp(ar * step)
    lr, li = mag * jnp.cos(ai * step), mag * jnp.sin(ai * step)
    den = ar * ar + ai * ai
    fr = ((lr - 1.0) * ar + li * ai) / den
    fi = (li * ar - (lr - 1.0) * ai) / den
    lr_ref[...] = lr
    li_ref[...] = li
    br, bi = bre_ref[...], bim_ref[...]
    bbr_ref[...] = fr[None] * br - fi[None] * bi
    bbi_ref[...] = fr[None] * bi + fi[None] * br


def _s5_discretise(a_re, a_im, log_step, b_re, b_im):
    g, p = a_re.shape
    h = b_re.shape[-1]
    sd = jax.ShapeDtypeStruct
    return pl.pallas_call(
        _s5_disc_kernel,
        out_shape=[sd((g, p), F32), sd((g, p), F32), sd((h, g, p), F32), sd((h, g, p), F32)],
        name="s5_discretise",
    )(a_re, a_im, log_step.reshape(g, 1), b_re.transpose(2, 0, 1), b_im.transpose(2, 0, 1))


def _s5_kernel(u_ref, wb_ref, lr_ref, li_ref, cs_ref, dsk_ref, wglu_ref, bglu_ref, o_ref,
               rre_ref, rim_ref, xr_ref, xi_ref, y_ref, *, batch, tiles):
    chains = tiles * S5_PAIRS * batch

    @pl.when(pl.program_id(0) == 0)
    def _():
        xr_ref[...] = jnp.zeros_like(xr_ref)
        xi_ref[...] = jnp.zeros_like(xi_ref)

    def chain_rows(j, pair, b):
        return pl.ds(((j * S5_PAIRS + pair) * batch + b) * S5_PITCH, S5_TIME)

    for b in range(batch):
        for j in range(tiles):
            drive = jnp.dot(u_ref[b, :, j * LANES:(j + 1) * LANES].astype(BF16), wb_ref[j],
                            preferred_element_type=F32)
            for pair in range(S5_PAIRS):
                rre_ref[chain_rows(j, pair, b), :] = drive[:, pair * LANES:(pair + 1) * LANES]
                rim_ref[chain_rows(j, pair, b), :] = drive[:, (S5_PAIRS + pair) * LANES:(S5_PAIRS + pair + 1) * LANES]

    lr, li = lr_ref[...], li_ref[...]

    def step(t, carry):
        xr, xi = carry
        rows = pl.ds(t, chains, stride=S5_PITCH)
        nxr = lr * xr - li * xi + rre_ref[rows, :]
        nxi = lr * xi + li * xr + rim_ref[rows, :]
        rre_ref[rows, :] = nxr
        rim_ref[rows, :] = nxi
        return nxr, nxi

    xr, xi = lax.fori_loop(0, S5_TIME, step, (xr_ref[...], xi_ref[...]), unroll=2)
    xr_ref[...] = xr
    xi_ref[...] = xi

    for b in range(batch):
        for j in range(tiles):
            states = jnp.concatenate(
                [rre_ref[chain_rows(j, pair, b), :].astype(BF16) for pair in range(S5_PAIRS)]
                + [rim_ref[chain_rows(j, pair, b), :].astype(BF16) for pair in range(S5_PAIRS)], axis=1)
            cols = slice(j * LANES, (j + 1) * LANES)
            y = jnp.dot(states, cs_ref[j], preferred_element_type=F32) + dsk_ref[:, cols] * u_ref[b, :, cols]
            y_ref[b, :, cols] = jax.nn.gelu(y)

    for b in range(batch):
        y = y_ref[b]
        gate = jnp.dot(y.astype(BF16), wglu_ref[...], preferred_element_type=F32) + bglu_ref[...]
        o_ref[b] = (y * jax.nn.sigmoid(gate)).astype(BF16)


def _s5_mixer(zin, a_re, a_im, log_step, b_re, b_im, c_re, c_im, d_skip, w_glu, b_glu):
    batch, seq, _ = zin.shape
    groups, states = a_re.shape
    width = groups * S5_GROUP
    tiles = width // LANES
    chains = tiles * S5_PAIRS * batch
    lr, li, bbr, bbi = _s5_discretise(a_re, a_im, log_step, b_re, b_im)

    eye = jnp.eye(S5_TILE_GROUPS, dtype=F32)
    bb = jnp.stack([bbr, bbi]).reshape(2, S5_GROUP, tiles, S5_TILE_GROUPS, states)
    wb = jnp.einsum("rhjgp,gk->jghrkp", bb, eye).reshape(tiles, LANES, 2 * S5_TILE_GROUPS * states).astype(BF16)
    cc = jnp.stack([c_re, -c_im]).reshape(2, tiles, S5_TILE_GROUPS, S5_GROUP, states)
    cs = jnp.einsum("rjghp,gk->jrkpgh", cc, eye).reshape(tiles, 2 * S5_TILE_GROUPS * states, LANES).astype(BF16)

    def per_chain(lam):
        t = lam.reshape(tiles * S5_PAIRS, 1, 2 * states)
        return jnp.broadcast_to(t, (tiles * S5_PAIRS, batch, 2 * states)).reshape(chains, 2 * states)

    const = lambda *shape: pl.BlockSpec(shape, lambda i: (0,) * len(shape))
    return pl.pallas_call(
        functools.partial(_s5_kernel, batch=batch, tiles=tiles),
        grid=(seq // S5_TIME,),
        in_specs=[
            pl.BlockSpec((batch, S5_TIME, width), lambda i: (0, i, 0)),
            const(tiles, LANES, 2 * S5_TILE_GROUPS * states),
            const(chains, LANES),
            const(chains, LANES),
            const(tiles, 2 * S5_TILE_GROUPS * states, LANES),
            const(1, width),
            const(width, width),
            const(1, width),
        ],
        out_specs=pl.BlockSpec((batch, S5_TIME, width), lambda i: (0, i, 0)),
        out_shape=jax.ShapeDtypeStruct((batch, seq, width), BF16),
        scratch_shapes=[
            pltpu.VMEM((chains * S5_PITCH, LANES), F32),
            pltpu.VMEM((chains * S5_PITCH, LANES), F32),
            pltpu.VMEM((chains, LANES), F32),
            pltpu.VMEM((chains, LANES), F32),
            pltpu.VMEM((batch, S5_TIME, width), F32),
        ],
        compiler_params=_params("arbitrary"),
        name="s5",
    )(zin, wb, per_chain(lr), per_chain(li), cs, d_skip.reshape(1, width), w_glu.astype(BF16),
      b_glu.reshape(1, width))


def _shift_rows(cur, tail, s):
    rolled = pltpu.roll(cur, s, 0)
    rolled_tail = pltpu.roll(tail, s, 0)
    row = lax.broadcasted_iota(jnp.int32, tail.shape, 0)
    top = jnp.where(row < s, rolled_tail, rolled[:SUBLANES])
    return jnp.concatenate([top, rolled[SUBLANES:]], axis=0)


def _gdn_kernel(q_ref, k_ref, v_ref, z_ref, gc_ref, gr_ref, cw_ref, alc_ref, dtc_ref, alr_ref, dtr_ref,
                ng_ref, o_ref, s_ref, tail_ref, *, cps):
    heads, dk, dv = GDN_HEADS, GDN_DK, GDN_DV
    steps = cps * CHUNK

    @pl.when(pl.program_id(1) == 0)
    def _():
        s_ref[...] = jnp.zeros_like(s_ref)
        tail_ref[...] = jnp.zeros_like(tail_ref)

    tri, strict, tri_u = _tri_masks(CHUNK)
    row = lax.broadcasted_iota(jnp.int32, (steps, steps), 0)
    col = lax.broadcasted_iota(jnp.int32, (steps, steps), 1)
    tri_chunks = (row >= col) & (row // CHUNK == col // CHUNK)
    eye = (lax.broadcasted_iota(jnp.int32, (CHUNK, CHUNK), 0)
           == lax.broadcasted_iota(jnp.int32, (CHUNK, CHUNK), 1)).astype(F32)

    conv = []
    for idx, ref in enumerate((q_ref, k_ref, v_ref)):
        cur = ref[...]
        width = cur.shape[1]
        tail = tail_ref[idx]
        cw = cw_ref[:, idx * width:(idx + 1) * width]
        acc = cur * cw[GDN_CONV - 1:GDN_CONV]
        for s in range(1, GDN_CONV):
            acc = acc + _shift_rows(cur, tail, s) * cw[GDN_CONV - 1 - s:GDN_CONV - s]
        tail_ref[idx] = cur[steps - SUBLANES:]
        conv.append(_silu(acc))
    q_all, k_all, v_all = conv

    gc = gc_ref[...]
    beta_c = jax.nn.sigmoid(gc)
    g_c = -jnp.exp(alc_ref[...]) * _softplus(gc + dtc_ref[...])
    gcum_c = _cumsum_rows(tri_chunks, g_c)
    gr = gr_ref[...].reshape(cps * 2 * heads, CHUNK)
    g_r = -jnp.exp(alr_ref[...]) * _softplus(gr + dtr_ref[...])
    gcum_r = _cumsum_lanes(g_r, tri_u)

    items = [(c, h) for c in range(cps) for h in range(heads)]
    pre = {}
    for c, h in items:
        rows = slice(c * CHUNK, (c + 1) * CHUNK)
        q = q_all[rows, h * dk:(h + 1) * dk]
        k = k_all[rows, h * dk:(h + 1) * dk]
        v = v_all[rows, h * dv:(h + 1) * dv]
        q = q * lax.rsqrt(jnp.sum(q * q, axis=-1, keepdims=True) + EPS) * dk ** -0.5
        k = k * lax.rsqrt(jnp.sum(k * k, axis=-1, keepdims=True) + EPS)
        beta = beta_c[rows, h:h + 1]
        gcol = gcum_c[rows, heads + h:heads + h + 1]
        grow = gcum_r[c * 2 * heads + heads + h:c * 2 * heads + heads + h + 1, :]
        glast = gcol[CHUNK - 1:CHUNK, :]
        decay = jnp.where(tri, jnp.exp(gcol - grow), 0.0)
        egc = jnp.exp(gcol)
        kb = k * beta
        pre[c, h] = dict(
            lower=jnp.where(strict, _mm_nt(kb, k) * decay, 0.0),
            rhs=jnp.concatenate([v * beta, kb * egc], axis=1).astype(BF16),
            qk=jnp.where(tri, _mm_nt(q, k) * decay, 0.0).astype(BF16),
            q_dec=(q * egc).astype(BF16),
            k_dec=(k * jnp.exp(glast - gcol)).astype(BF16),
            carry=jnp.exp(glast))

    power = {it: -pre[it]["lower"] for it in items}
    inv = {it: eye + power[it] for it in items}
    for _ in range(int(math.log2(CHUNK)) - 1):
        for it in items:
            power[it] = _mm(power[it], power[it])
        for it in items:
            inv[it] = inv[it] + _mm(inv[it], power[it])
    sol = {it: _mm(inv[it], pre[it]["rhs"]) for it in items}

    for c in range(cps):
        rows = slice(c * CHUNK, (c + 1) * CHUNK)
        state = [s_ref[h] for h in range(heads)]
        v_new = [sol[c, h][:, :dv] - _mm(sol[c, h][:, dv:], state[h]) for h in range(heads)]
        out = [_mm(pre[c, h]["q_dec"], state[h]) + _mm(pre[c, h]["qk"], v_new[h]) for h in range(heads)]
        for h in range(heads):
            s_ref[h] = state[h] * pre[c, h]["carry"] + _mm_tn(pre[c, h]["k_dec"], v_new[h])
        for h in range(heads):
            o = _rms_norm(out[h], ng_ref[...]) * _silu(z_ref[rows, h * dv:(h + 1) * dv])
            o_ref[rows, h * dv:(h + 1) * dv] = o.astype(BF16)


def _gate_rows(gates, count):
    b, l, _ = gates.shape
    return gates[:, :, :count].reshape(b, l // CHUNK, CHUNK, count).transpose(0, 1, 3, 2)


def _lane_row(vec, offset):
    return jnp.zeros((1, LANES), F32).at[0, offset:offset + vec.shape[0]].set(vec)


def _gdn_mixer(zin, gates, conv_w, a_log, dt_bias, norm_g, *, col0, cps=MIXER_CHUNKS_PER_STEP):
    batch, seq, _ = zin.shape
    heads = GDN_HEADS
    qk_w, v_w = heads * GDN_DK, heads * GDN_DV
    assert qk_w == v_w and col0 % qk_w == 0
    c0 = col0 // qk_w
    steps = cps * CHUNK
    zeros = jnp.zeros((heads,), F32)
    al_c, dt_c = _lane_row(a_log, heads), _lane_row(dt_bias, heads)
    al_r = jnp.tile(jnp.concatenate([zeros, a_log]), cps).reshape(cps * 2 * heads, 1)
    dt_r = jnp.tile(jnp.concatenate([zeros, dt_bias]), cps).reshape(cps * 2 * heads, 1)
    tok = lambda blk: pl.BlockSpec((None, steps, qk_w), lambda b, n, blk=blk: (b, n, c0 + blk))
    const = lambda *shape: pl.BlockSpec(shape, lambda b, n: (0,) * len(shape))
    return pl.pallas_call(
        functools.partial(_gdn_kernel, cps=cps),
        grid=(batch, seq // steps),
        in_specs=[
            tok(0), tok(1), tok(2), tok(3),
            pl.BlockSpec((None, steps, LANES), lambda b, n: (b, n, 0)),
            pl.BlockSpec((None, cps, 2 * heads, CHUNK), lambda b, n: (b, n, 0, 0)),
            const(GDN_CONV, 2 * qk_w + v_w),
            const(1, LANES), const(1, LANES), const(cps * 2 * heads, 1), const(cps * 2 * heads, 1),
            const(1, GDN_DV),
        ],
        out_specs=pl.BlockSpec((None, steps, v_w), lambda b, n: (b, n, 0)),
        out_shape=jax.ShapeDtypeStruct((batch, seq, v_w), BF16),
        scratch_shapes=[
            pltpu.VMEM((heads, GDN_DK, GDN_DV), F32),
            pltpu.VMEM((3, SUBLANES, qk_w), F32),
        ],
        compiler_params=_params("parallel", "arbitrary"),
        name="gdn",
    )(zin, zin, zin, zin, gates, _gate_rows(gates, 2 * heads), conv_w, al_c, dt_c, al_r, dt_r,
      norm_g.reshape(1, GDN_DV))


def _mlstm_kernel(q_ref, k_ref, v_ref, op_ref, gc_ref, gr_ref, bc_ref, br_ref, ng_ref, o_ref,
                  c_ref, n_ref, m_ref, *, cps):
    heads, dk, dv = MLSTM_HEADS, MLSTM_DK, MLSTM_DV
    steps = cps * CHUNK

    @pl.when(pl.program_id(1) == 0)
    def _():
        c_ref[...] = jnp.zeros_like(c_ref)
        n_ref[...] = jnp.zeros_like(n_ref)
        m_ref[...] = jnp.zeros_like(m_ref)

    tri, _, tri_u = _tri_masks(CHUNK)
    row = lax.broadcasted_iota(jnp.int32, (steps, steps), 0)
    col = lax.broadcasted_iota(jnp.int32, (steps, steps), 1)
    tri_chunks = (row >= col) & (row // CHUNK == col // CHUNK)
    pre_c = gc_ref[...] + bc_ref[...]
    bcum_c = _cumsum_rows(tri_chunks, -_softplus(-pre_c))
    pre_r = gr_ref[...].reshape(cps * 2 * heads, CHUNK) + br_ref[...]
    bcum_r = _cumsum_lanes(-_softplus(-pre_r), tri_u)

    items = [(c, h) for c in range(cps) for h in range(heads)]
    gate = {}
    for c, h in items:
        rows = slice(c * CHUNK, (c + 1) * CHUNK)
        ig_col = pre_c[rows, h:h + 1]
        ig_row = pre_r[c * 2 * heads + h:c * 2 * heads + h + 1, :]
        b_col = bcum_c[rows, heads + h:heads + h + 1]
        b_row = bcum_r[c * 2 * heads + heads + h:c * 2 * heads + heads + h + 1, :]
        b_last = b_col[CHUNK - 1:CHUNK, :]
        intra_log = jnp.where(tri, b_col - b_row + ig_row, -jnp.inf)
        upd_log = b_last - b_col + ig_col
        gate[c, h] = dict(b_col=b_col, b_last=b_last, intra_log=intra_log, upd_log=upd_log,
                          intra_max=jnp.max(intra_log, axis=-1, keepdims=True),
                          upd_max=jnp.max(upd_log, axis=0, keepdims=True))

    m_run = [m_ref[h][0:1, 0:1] for h in range(heads)]
    for c, h in items:
        g = gate[c, h]
        m_new = jnp.maximum(g["b_last"] + m_run[h], g["upd_max"])
        g.update(m_in=m_run[h], m_out=m_new, carry=jnp.exp(g["b_last"] + m_run[h] - m_new))
        m_run[h] = m_new

    q_s = {it: q_ref[it[0] * CHUNK:(it[0] + 1) * CHUNK, it[1] * dk:(it[1] + 1) * dk] * dk ** -0.5 for it in items}
    k_s = {it: k_ref[it[0] * CHUNK:(it[0] + 1) * CHUNK, it[1] * dk:(it[1] + 1) * dk] for it in items}
    v_s = {it: v_ref[it[0] * CHUNK:(it[0] + 1) * CHUNK, it[1] * dv:(it[1] + 1) * dv].astype(BF16) for it in items}
    qk = {it: _mm_nt(q_s[it], k_s[it]) for it in items}
    m_s, inter_w, s_mat = {}, {}, {}
    for it in items:
        g = gate[it]
        inter_log = g["b_col"] + g["m_in"]
        m_s[it] = jnp.maximum(inter_log, g["intra_max"])
        inter_w[it] = jnp.exp(inter_log - m_s[it])
        s_mat[it] = qk[it] * jnp.exp(g["intra_log"] - m_s[it])
    s_v = {it: _mm(s_mat[it], v_s[it]) for it in items}
    s_sum = {it: jnp.sum(s_mat[it], axis=-1, keepdims=True) for it in items}
    kw = {it: k_s[it] * jnp.exp(gate[it]["upd_log"] - gate[it]["m_out"]) for it in items}
    kw_v = {it: _mm_tn(kw[it], v_s[it]) for it in items}
    kw_sum = {it: jnp.sum(kw[it], axis=0, keepdims=True) for it in items}

    for c in range(cps):
        rows = slice(c * CHUNK, (c + 1) * CHUNK)
        c_mat = [c_ref[h] for h in range(heads)]
        n_vec = [n_ref[h][0:1, :] for h in range(heads)]
        q_c = [_mm(q_s[c, h], c_mat[h]) for h in range(heads)]
        for h in range(heads):
            c_ref[h] = gate[c, h]["carry"] * c_mat[h] + kw_v[c, h]
            n_ref[h] = jnp.broadcast_to(gate[c, h]["carry"] * n_vec[h] + kw_sum[c, h], (SUBLANES, dk))
        for h in range(heads):
            it = (c, h)
            num = inter_w[it] * q_c[h] + s_v[it]
            den = inter_w[it] * jnp.sum(q_s[it] * n_vec[h], axis=-1, keepdims=True) + s_sum[it]
            hid = num / jnp.maximum(jnp.abs(den), jnp.exp(-m_s[it]))
            out_gate = jax.nn.sigmoid(op_ref[rows, h * dv:(h + 1) * dv])
            o_ref[rows, h * dv:(h + 1) * dv] = (out_gate * _rms_norm(hid, ng_ref[...])).astype(BF16)
    for h in range(heads):
        m_ref[h] = jnp.broadcast_to(m_run[h], (SUBLANES, LANES))


def _mlstm_mixer(zin, gates, gate_bias, norm_g, *, cps=MIXER_CHUNKS_PER_STEP):
    batch, seq, _ = zin.shape
    heads = MLSTM_HEADS
    qk_w, v_w = heads * MLSTM_DK, heads * MLSTM_DV
    steps = cps * CHUNK
    bias = gate_bias.reshape(2 * heads)
    tok = lambda w, blk: pl.BlockSpec((None, steps, w), lambda b, n: (b, n, blk))
    const = lambda *shape: pl.BlockSpec(shape, lambda b, n: (0,) * len(shape))
    return pl.pallas_call(
        functools.partial(_mlstm_kernel, cps=cps),
        grid=(batch, seq // steps),
        in_specs=[
            tok(qk_w, 0), tok(qk_w, 1), tok(v_w, 1), tok(v_w, 2),
            pl.BlockSpec((None, steps, LANES), lambda b, n: (b, n, 0)),
            pl.BlockSpec((None, cps, 2 * heads, CHUNK), lambda b, n: (b, n, 0, 0)),
            const(1, LANES), const(cps * 2 * heads, 1), const(1, MLSTM_DV),
        ],
        out_specs=pl.BlockSpec((None, steps, v_w), lambda b, n: (b, n, 0)),
        out_shape=jax.ShapeDtypeStruct((batch, seq, v_w), BF16),
        scratch_shapes=[
            pltpu.VMEM((heads, MLSTM_DK, MLSTM_DV), F32),
            pltpu.VMEM((heads, SUBLANES, MLSTM_DK), F32),
            pltpu.VMEM((heads, SUBLANES, LANES), F32),
        ],
        compiler_params=_params("parallel", "arbitrary"),
        name="mlstm",
    )(zin, zin, zin, zin, gates, _gate_rows(gates, 2 * heads), _lane_row(bias, 0),
      jnp.tile(bias, cps).reshape(cps * 2 * heads, 1), norm_g.reshape(1, MLSTM_DV))


def _ret_kernel(q_ref, k_ref, v_ref, g_ref, pos_ref, freq_ref, dmat_ref, xi_ref, zeta_ref, gam_ref, ng_ref,
                o_ref, s_ref, *, cps):
    heads, dk, dv = RET_HEADS, RET_DK, RET_DV
    half = dk // 2
    steps = cps * CHUNK

    @pl.when(pl.program_id(1) == 0)
    def _():
        s_ref[...] = jnp.zeros_like(s_ref)

    ang = pos_ref[...].reshape(steps, 1).astype(F32) * freq_ref[...]
    cos, sin = jnp.cos(ang), jnp.sin(ang)
    lane = lax.broadcasted_iota(jnp.int32, (steps, dk), 1)
    sin_signed = jnp.where(lane < half, -sin, sin)

    def rotary(x):
        return x * cos + pltpu.roll(x, half, 1) * sin_signed

    q_rot = [rotary(q_ref[:, h * dk:(h + 1) * dk]) * dk ** -0.5 for h in range(heads)]
    k_rot = [rotary(k_ref[:, h * dk:(h + 1) * dk]) for h in range(heads)]
    items = [(c, h) for c in range(cps) for h in range(heads)]
    rows = {c: slice(c * CHUNK, (c + 1) * CHUNK) for c in range(cps)}
    q_s = {(c, h): q_rot[h][rows[c]].astype(BF16) for c, h in items}
    k_s = {(c, h): k_rot[h][rows[c]] for c, h in items}
    v_s = {(c, h): v_ref[rows[c], h * dv:(h + 1) * dv].astype(BF16) for c, h in items}
    qk = {it: _mm_nt(q_s[it], k_s[it]) * dmat_ref[it[1]] for it in items}
    intra = {it: _mm(qk[it], v_s[it]) for it in items}
    k_v = {it: _mm_tn(k_s[it] * zeta_ref[it[1]], v_s[it]) for it in items}

    for c in range(cps):
        state = [s_ref[h] for h in range(heads)]
        inter = [_mm(q_s[c, h], state[h]) for h in range(heads)]
        for h in range(heads):
            s_ref[h] = state[h] * gam_ref[h] + k_v[c, h]
        for h in range(heads):
            y = intra[c, h] + inter[h] * xi_ref[h]
            mu = jnp.mean(y, axis=-1, keepdims=True)
            yc = y - mu
            var = jnp.mean(yc * yc, axis=-1, keepdims=True)
            y = yc * lax.rsqrt(var + EPS) * ng_ref[...]
            o_ref[rows[c], h * dv:(h + 1) * dv] = (y * _silu(g_ref[rows[c], h * dv:(h + 1) * dv])).astype(BF16)


def _ret_mixer(zin, positions, norm_g, *, col0, cps=MIXER_CHUNKS_PER_STEP):
    batch, seq, _ = zin.shape
    heads, dk = RET_HEADS, RET_DK
    qk_w, v_w = heads * RET_DK, heads * RET_DV
    assert col0 % v_w == 0
    cq, cv = col0 // qk_w, col0 // v_w
    half = dk // 2
    steps = cps * CHUNK
    inv_freq = ROPE_BASE ** (-jnp.arange(half, dtype=F32) / half)
    freq = jnp.concatenate([inv_freq, inv_freq]).reshape(1, dk)
    log_gamma = jnp.log1p(-jnp.exp2(-5.0 - jnp.arange(heads, dtype=F32)))
    idx = jnp.arange(CHUNK, dtype=F32)
    tri = jnp.tril(jnp.ones((CHUNK, CHUNK), dtype=bool))
    diff = jnp.where(tri, idx[:, None] - idx[None, :], 0.0)
    dmat = jnp.where(tri, jnp.exp(diff * log_gamma[:, None, None]), 0.0)
    xi = jnp.exp((idx + 1.0) * log_gamma[:, None])[:, :, None]
    zeta = jnp.exp((CHUNK - 1.0 - idx) * log_gamma[:, None])[:, :, None]
    gamma_c = jnp.broadcast_to(jnp.exp(CHUNK * log_gamma)[:, None, None], (heads, 1, RET_DV))
    pos = positions.reshape(batch, seq // CHUNK, CHUNK, 1)
    tok = lambda w, blk: pl.BlockSpec((None, steps, w), lambda b, n: (b, n, blk))
    const = lambda *shape: pl.BlockSpec(shape, lambda b, n: (0,) * len(shape))
    return pl.pallas_call(
        functools.partial(_ret_kernel, cps=cps),
        grid=(batch, seq // steps),
        in_specs=[
            tok(qk_w, cq), tok(qk_w, cq + 1), tok(v_w, cv + 1), tok(v_w, cv + 2),
            pl.BlockSpec((None, cps, CHUNK, 1), lambda b, n: (b, n, 0, 0)),
            const(1, dk), const(heads, CHUNK, CHUNK), const(heads, CHUNK, 1), const(heads, CHUNK, 1),
            const(heads, 1, RET_DV), const(1, RET_DV),
        ],
        out_specs=pl.BlockSpec((None, steps, v_w), lambda b, n: (b, n, 0)),
        out_shape=jax.ShapeDtypeStruct((batch, seq, v_w), BF16),
        scratch_shapes=[pltpu.VMEM((heads, RET_DK, RET_DV), F32)],
        compiler_params=_params("parallel", "arbitrary"),
        name="retention",
    )(zin, zin, zin, zin, pos, freq, dmat, xi, zeta, gamma_c, norm_g.reshape(1, RET_DV))


def _pad_lanes(w):
    return jnp.pad(w, ((0, 0), (0, LANES - w.shape[1])))


def _even_mixer(x, batch, norm_g, w_in_all, j, w_out, a_re, a_im, log_step, b_re, b_im, c_re, c_im, d_skip, w_glu,
                b_glu, conv_w, a_log, dt_bias, gdn_g):
    s5_w = a_re.shape[0] * S5_GROUP
    main_w = s5_w + GDN_HEADS * (2 * GDN_DK + 2 * GDN_DV)
    w_gate = _pad_lanes(w_in_all[j, :, main_w:]).astype(BF16)
    zin, gates = _inproj(x, norm_g, [(w_in_all, j, main_w)], w_gate)
    zin = zin.reshape(batch, -1, main_w)
    gates = gates.reshape(batch, -1, LANES)
    ya = _s5_mixer(zin, a_re, a_im, log_step, b_re, b_im, c_re, c_im, d_skip, w_glu, b_glu)
    yb = _gdn_mixer(zin, gates, conv_w, a_log, dt_bias, gdn_g, col0=s5_w)
    m = x.shape[0]
    w_out = w_out.astype(BF16)
    return _outproj(x, ya.reshape(m, -1), yb.reshape(m, -1), w_out[:s5_w], w_out[s5_w:])


def _odd_mixer(x, batch, positions, norm_g, w_in_all, j, w_out, gate_bias, mlstm_g, ret_g):
    c_main = MLSTM_HEADS * (2 * MLSTM_DK + 2 * MLSTM_DV)
    r_main = RET_HEADS * (2 * RET_DK + 2 * RET_DV)
    n_gate = 2 * MLSTM_HEADS
    w_ret = w_in_all[j, :, c_main + n_gate:][None]
    w_gate = _pad_lanes(w_in_all[j, :, c_main:c_main + n_gate]).astype(BF16)
    zin, gates = _inproj(x, norm_g, [(w_in_all, j, c_main), (w_ret, 0, r_main)], w_gate)
    zin = zin.reshape(batch, -1, c_main + r_main)
    gates = gates.reshape(batch, -1, LANES)
    yc = _mlstm_mixer(zin, gates, gate_bias, mlstm_g)
    yd = _ret_mixer(zin, positions, ret_g, col0=c_main)
    m = x.shape[0]
    w_out = w_out.astype(BF16)
    split = MLSTM_HEADS * MLSTM_DV
    return _outproj(x, yc.reshape(m, -1), yd.reshape(m, -1), w_out[:split], w_out[split:])


def kernel(x, positions, ffn_norm, ffn_w1, ffn_w3, ffn_w2, mix_norm, even_w_in, even_w_out, s5_a_re, s5_a_im, s5_log_step, s5_b_re, s5_b_im, s5_c_re, s5_c_im, s5_d, s5_w_glu, s5_b_glu, gdn_conv_w, gdn_a_log, gdn_dt_bias, gdn_norm, odd_w_in, odd_w_out, mlstm_gate_bias, mlstm_norm, ret_norm, final_norm):
    batch, seq, d = x.shape
    depth = ffn_norm.shape[0]
    x = x.reshape(batch * seq, d)

    def ffn(x, layer, which, final_g=None):
        return _ffn(x, ffn_norm[layer, which], ffn_w1, ffn_w3, ffn_w2, layer, which, final_g)

    for layer in range(depth):
        x = ffn(x, layer, 0)
        j = layer // 2
        if layer % 2 == 0:
            x = _even_mixer(x, batch, mix_norm[layer], even_w_in, j, even_w_out[j], s5_a_re[j], s5_a_im[j],
                            s5_log_step[j], s5_b_re[j], s5_b_im[j], s5_c_re[j], s5_c_im[j], s5_d[j].reshape(-1),
                            s5_w_glu[j], s5_b_glu[j], gdn_conv_w[j], gdn_a_log[j], gdn_dt_bias[j], gdn_norm[j])
        else:
            x = _odd_mixer(x, batch, positions, mix_norm[layer], odd_w_in, j, odd_w_out[j], mlstm_gate_bias[j],
                           mlstm_norm[j], ret_norm[j])
        x = ffn(x, layer, 1, final_norm if layer == depth - 1 else None)
    return x.reshape(batch, seq, d)
```

```python
import functools
import math

import jax
import jax.numpy as jnp
from jax import lax
from jax.experimental import pallas as pl
from jax.experimental.pallas import tpu as pltpu

F32 = jnp.float32
BF16 = jnp.bfloat16

EPS = 1e-6
CHUNK = 64
ROPE_BASE = 10000.0
LANES = 128
SUBLANES = 8
VMEM_LIMIT_BYTES = 60 * 1024 * 1024

S5_GROUP = 16
S5_STATE = 64
S5_TILE_GROUPS = LANES // S5_GROUP
S5_PAIRS = S5_TILE_GROUPS // 2
S5_TIME = 256
S5_PITCH = S5_TIME + SUBLANES

GDN_HEADS = 8
GDN_DK = 128
GDN_DV = 128
GDN_CONV = 4
MIXER_CHUNKS_PER_STEP = 4
MLSTM_HEADS = 4
MLSTM_DK = 128
MLSTM_DV = 256
RET_HEADS = 4
RET_DK = 128
RET_DV = 256


def _params(*semantics):
    return pltpu.CompilerParams(dimension_semantics=semantics, vmem_limit_bytes=VMEM_LIMIT_BYTES)


def _mm(a, b):
    return jnp.dot(a.astype(BF16), b.astype(BF16), preferred_element_type=F32)


def _mm_nt(a, b):
    return lax.dot_general(a.astype(BF16), b.astype(BF16), (((1,), (1,)), ((), ())), preferred_element_type=F32)


def _mm_tn(a, b):
    return lax.dot_general(a.astype(BF16), b.astype(BF16), (((0,), (0,)), ((), ())), preferred_element_type=F32)


def _split3(x):
    x1 = x.astype(BF16)
    r1 = x - x1.astype(F32)
    x2 = r1.astype(BF16)
    x3 = (r1 - x2.astype(F32)).astype(BF16)
    return x1, x2, x3


def _cumsum_rows(tri_lower, x):
    t = tri_lower.astype(BF16)
    return sum(jnp.dot(t, p, preferred_element_type=F32) for p in _split3(x))


def _cumsum_lanes(x, tri_upper):
    t = tri_upper.astype(BF16)
    return sum(jnp.dot(p, t, preferred_element_type=F32) for p in _split3(x))


def _rms_norm(x, g):
    return x * lax.rsqrt(jnp.mean(x * x, axis=-1, keepdims=True) + EPS) * g


def _silu(x):
    return x * jax.nn.sigmoid(x)


def _softplus(x):
    return jnp.maximum(x, 0.0) + jnp.log1p(jnp.exp(-jnp.abs(x)))


def _tri_masks(n):
    r = lax.broadcasted_iota(jnp.int32, (n, n), 0)
    c = lax.broadcasted_iota(jnp.int32, (n, n), 1)
    return r >= c, r > c, r <= c


def _ffn_kernel(x_ref, g_ref, w1_ref, w3_ref, w2_ref, fg_ref, o_ref, h_ref, *, final_norm):
    j = pl.program_id(1)

    @pl.when(j == 0)
    def _():
        x = x_ref[...]
        h_ref[...] = _rms_norm(x, g_ref[...]).astype(BF16)
        o_ref[...] = x

    h = h_ref[...]
    a = jnp.dot(h, w1_ref[...].astype(BF16), preferred_element_type=F32)
    b = jnp.dot(h, w3_ref[...].astype(BF16), preferred_element_type=F32)
    o_ref[...] += 0.5 * jnp.dot((_silu(a) * b).astype(BF16), w2_ref[...].astype(BF16), preferred_element_type=F32)

    if final_norm:
        @pl.when(j == pl.num_programs(1) - 1)
        def _():
            o_ref[...] = _rms_norm(o_ref[...], fg_ref[...])


def _ffn(x, g, w1, w3, w2, layer, which, final_g=None, *, tm=1024, tf=256):
    m, d = x.shape
    f = w1.shape[-1]
    final_norm = final_g is not None
    fg = final_g if final_norm else g
    return pl.pallas_call(
        functools.partial(_ffn_kernel, final_norm=final_norm),
        grid=(m // tm, f // tf),
        in_specs=[
            pl.BlockSpec((tm, d), lambda i, j: (i, 0)),
            pl.BlockSpec((1, d), lambda i, j: (0, 0)),
            pl.BlockSpec((None, None, d, tf), lambda i, j: (layer, which, 0, j)),
            pl.BlockSpec((None, None, d, tf), lambda i, j: (layer, which, 0, j)),
            pl.BlockSpec((None, None, tf, d), lambda i, j: (layer, which, j, 0)),
            pl.BlockSpec((1, d), lambda i, j: (0, 0)),
        ],
        out_specs=pl.BlockSpec((tm, d), lambda i, j: (i, 0)),
        out_shape=jax.ShapeDtypeStruct((m, d), F32),
        scratch_shapes=[pltpu.VMEM((tm, d), BF16)],
        compiler_params=_params("parallel", "arbitrary"),
        name="ffn",
    )(x, g.reshape(1, d), w1, w3, w2, fg.reshape(1, d))


def _inproj_kernel(*refs, bounds):
    x_ref, g_ref = refs[:2]
    w_refs = refs[2:2 + len(bounds)]
    wg_ref, o_ref, og_ref, h_ref = refs[2 + len(bounds):]
    j = pl.program_id(1)

    @pl.when(j == 0)
    def _():
        h = _rms_norm(x_ref[...], g_ref[...]).astype(BF16)
        h_ref[...] = h
        og_ref[...] = jnp.dot(h, wg_ref[...], preferred_element_type=F32)

    for w_ref, (start, count) in zip(w_refs, bounds):
        @pl.when((j >= start) & (j < start + count))
        def _(w_ref=w_ref):
            o_ref[...] = jnp.dot(h_ref[...], w_ref[...].astype(BF16), preferred_element_type=F32).astype(BF16)


def _inproj(x, g, segments, w_gate, *, tm=1024, tn=512):
    m, d = x.shape
    bounds, start = [], 0
    for _, width in segments:
        bounds.append((start, width // tn))
        start += width // tn
    n = start * tn

    def w_spec(first, count):
        return pl.BlockSpec((d, tn), lambda i, j: (0, jnp.clip(j - first, 0, count - 1)))

    return pl.pallas_call(
        functools.partial(_inproj_kernel, bounds=tuple(bounds)),
        grid=(m // tm, n // tn),
        in_specs=[
            pl.BlockSpec((tm, d), lambda i, j: (i, 0)),
            pl.BlockSpec((1, d), lambda i, j: (0, 0)),
            *[w_spec(first, count) for first, count in bounds],
            pl.BlockSpec((d, LANES), lambda i, j: (0, 0)),
        ],
        out_specs=[
            pl.BlockSpec((tm, tn), lambda i, j: (i, j)),
            pl.BlockSpec((tm, LANES), lambda i, j: (i, 0)),
        ],
        out_shape=[jax.ShapeDtypeStruct((m, n), BF16), jax.ShapeDtypeStruct((m, LANES), F32)],
        scratch_shapes=[pltpu.VMEM((tm, d), BF16)],
        compiler_params=_params("parallel", "arbitrary"),
        name="inproj",
    )(x, g.reshape(1, d), *[w for w, _ in segments], w_gate)


def _outproj_kernel(x_ref, ya_ref, yb_ref, wa_ref, wb_ref, o_ref):
    o_ref[...] = (x_ref[...] + jnp.dot(ya_ref[...], wa_ref[...], preferred_element_type=F32)
                  + jnp.dot(yb_ref[...], wb_ref[...], preferred_element_type=F32))


def _outproj(x, ya, yb, wa, wb, *, tm=1024, tn=1024):
    m, d = x.shape
    ka, kb = ya.shape[1], yb.shape[1]
    return pl.pallas_call(
        _outproj_kernel,
        grid=(m // tm, d // tn),
        in_specs=[
            pl.BlockSpec((tm, tn), lambda i, j: (i, j)),
            pl.BlockSpec((tm, ka), lambda i, j: (i, 0)),
            pl.BlockSpec((tm, kb), lambda i, j: (i, 0)),
            pl.BlockSpec((ka, tn), lambda i, j: (0, j)),
            pl.BlockSpec((kb, tn), lambda i, j: (0, j)),
        ],
        out_specs=pl.BlockSpec((tm, tn), lambda i, j: (i, j)),
        out_shape=jax.ShapeDtypeStruct((m, d), F32),
        compiler_params=_params("parallel", "arbitrary"),
        name="outproj",
    )(x, ya, yb, wa, wb)


def _s5_disc_kernel(are_ref, aim_ref, step_ref, bre_ref, bim_ref, lr_ref, li_ref, bbr_ref, bbi_ref):
    ar, ai = are_ref[...], aim_ref[...]
    step = jnp.exp(step_ref[...])
    mag = jnp.exp(ar * step)
    lr, li = mag * jnp.cos(ai * step), mag * jnp.sin(ai * step)
    den = ar * ar + ai * ai
    fr = ((lr - 1.0) * ar + li * ai) / den
    fi = (li * ar - (lr - 1.0) * ai) / den
    lr_ref[...] = lr
    li_ref[...] = li
    br, bi = bre_ref[...], bim_ref[...]
    bbr_ref[...] = fr[None] * br - fi[None] * bi
    bbi_ref[...] = fr[None] * bi + fi[None] * br


def _s5_discretise(a_re, a_im, log_step, b_re, b_im):
    g, p = a_re.shape
    h = b_re.shape[-1]
    sd = jax.ShapeDtypeStruct
    return pl.pallas_call(
        _s5_disc_kernel,
        out_shape=[sd((g, p), F32), sd((g, p), F32), sd((h, g, p), F32), sd((h, g, p), F32)],
        name="s5_discretise",
    )(a_re, a_im, log_step.reshape(g, 1), b_re.transpose(2, 0, 1), b_im.transpose(2, 0, 1))


def _s5_kernel(u_ref, wb_ref, lr_ref, li_ref, cs_ref, dsk_ref, wglu_ref, bglu_ref, o_ref,
               rre_ref, rim_ref, xr_ref, xi_ref, y_ref, *, batch, tiles):
    chains = tiles * S5_PAIRS * batch

    @pl.when(pl.program_id(0) == 0)
    def _():
        xr_ref[...] = jnp.zeros_like(xr_ref)
        xi_ref[...] = jnp.zeros_like(xi_ref)

    def chain_rows(j, pair, b):
        return pl.ds(((j * S5_PAIRS + pair) * batch + b) * S5_PITCH, S5_TIME)

    def u_tile(j):
        return jnp.concatenate([u_ref[b, :, j * LANES:(j + 1) * LANES] for b in range(batch)], axis=0)

    for j in range(tiles):
        drive = jnp.dot(u_tile(j).astype(BF16), wb_ref[j], preferred_element_type=F32)
        for b in range(batch):
            rows = slice(b * S5_TIME, (b + 1) * S5_TIME)
            for pair in range(S5_PAIRS):
                rre_ref[chain_rows(j, pair, b), :] = drive[rows, pair * LANES:(pair + 1) * LANES]
                rim_ref[chain_rows(j, pair, b), :] = drive[rows, (S5_PAIRS + pair) * LANES:(S5_PAIRS + pair + 1) * LANES]

    lr, li = lr_ref[...], li_ref[...]

    def step(t, carry):
        xr, xi = carry
        rows = pl.ds(t, chains, stride=S5_PITCH)
        nxr = lr * xr - li * xi + rre_ref[rows, :]
        nxi = lr * xi + li * xr + rim_ref[rows, :]
        rre_ref[rows, :] = nxr
        rim_ref[rows, :] = nxi
        return nxr, nxi

    xr, xi = lax.fori_loop(0, S5_TIME, step, (xr_ref[...], xi_ref[...]), unroll=4)
    xr_ref[...] = xr
    xi_ref[...] = xi

    for j in range(tiles):
        states = jnp.concatenate([
            jnp.concatenate(
                [rre_ref[chain_rows(j, pair, b), :].astype(BF16) for pair in range(S5_PAIRS)]
                + [rim_ref[chain_rows(j, pair, b), :].astype(BF16) for pair in range(S5_PAIRS)], axis=1)
            for b in range(batch)], axis=0)
        cols = slice(j * LANES, (j + 1) * LANES)
        y = jnp.dot(states, cs_ref[j], preferred_element_type=F32) + dsk_ref[:, cols] * u_tile(j).astype(F32)
        y_ref[:, cols] = jax.nn.gelu(y)

    y = y_ref[...]
    gate = jnp.dot(y.astype(BF16), wglu_ref[...], preferred_element_type=F32) + bglu_ref[...]
    out = (y * jax.nn.sigmoid(gate)).astype(BF16)
    for b in range(batch):
        o_ref[b] = out[b * S5_TIME:(b + 1) * S5_TIME]


def _s5_mixer(zin, a_re, a_im, log_step, b_re, b_im, c_re, c_im, d_skip, w_glu, b_glu):
    batch, seq, _ = zin.shape
    groups, states = a_re.shape
    width = groups * S5_GROUP
    tiles = width // LANES
    chains = tiles * S5_PAIRS * batch
    lr, li, bbr, bbi = _s5_discretise(a_re, a_im, log_step, b_re, b_im)

    eye = jnp.eye(S5_TILE_GROUPS, dtype=F32)
    bb = jnp.stack([bbr, bbi]).reshape(2, S5_GROUP, tiles, S5_TILE_GROUPS, states)
    wb = jnp.einsum("rhjgp,gk->jghrkp", bb, eye).reshape(tiles, LANES, 2 * S5_TILE_GROUPS * states).astype(BF16)
    cc = jnp.stack([c_re, -c_im]).reshape(2, tiles, S5_TILE_GROUPS, S5_GROUP, states)
    cs = jnp.einsum("rjghp,gk->jrkpgh", cc, eye).reshape(tiles, 2 * S5_TILE_GROUPS * states, LANES).astype(BF16)

    def per_chain(lam):
        t = lam.reshape(tiles * S5_PAIRS, 1, 2 * states)
        return jnp.broadcast_to(t, (tiles * S5_PAIRS, batch, 2 * states)).reshape(chains, 2 * states)

    const = lambda *shape: pl.BlockSpec(shape, lambda i: (0,) * len(shape))
    return pl.pallas_call(
        functools.partial(_s5_kernel, batch=batch, tiles=tiles),
        grid=(seq // S5_TIME,),
        in_specs=[
            pl.BlockSpec((batch, S5_TIME, width), lambda i: (0, i, 0)),
            const(tiles, LANES, 2 * S5_TILE_GROUPS * states),
            const(chains, LANES),
            const(chains, LANES),
            const(tiles, 2 * S5_TILE_GROUPS * states, LANES),
            const(1, width),
            const(width, width),
            const(1, width),
        ],
        out_specs=pl.BlockSpec((batch, S5_TIME, width), lambda i: (0, i, 0)),
        out_shape=jax.ShapeDtypeStruct((batch, seq, width), BF16),
        scratch_shapes=[
            pltpu.VMEM((chains * S5_PITCH, LANES), F32),
            pltpu.VMEM((chains * S5_PITCH, LANES), F32),
            pltpu.VMEM((chains, LANES), F32),
            pltpu.VMEM((chains, LANES), F32),
            pltpu.VMEM((batch * S5_TIME, width), F32),
        ],
        compiler_params=_params("arbitrary"),
        name="s5",
    )(zin, wb, per_chain(lr), per_chain(li), cs, d_skip.reshape(1, width), w_glu.astype(BF16),
      b_glu.reshape(1, width))


def _shift_rows(cur, tail, s):
    rolled = pltpu.roll(cur, s, 0)
    rolled_tail = pltpu.roll(tail, s, 0)
    row = lax.broadcasted_iota(jnp.int32, tail.shape, 0)
    top = jnp.where(row < s, rolled_tail, rolled[:SUBLANES])
    return jnp.concatenate([top, rolled[SUBLANES:]], axis=0)


def _gdn_kernel(q_ref, k_ref, v_ref, z_ref, gc_ref, gr_ref, cw_ref, alc_ref, dtc_ref, alr_ref, dtr_ref,
                ng_ref, o_ref, s_ref, tail_ref, *, cps):
    heads, dk, dv = GDN_HEADS, GDN_DK, GDN_DV
    steps = cps * CHUNK

    @pl.when(pl.program_id(1) == 0)
    def _():
        s_ref[...] = jnp.zeros_like(s_ref)
        tail_ref[...] = jnp.zeros_like(tail_ref)

    tri, strict, tri_u = _tri_masks(CHUNK)
    row = lax.broadcasted_iota(jnp.int32, (steps, steps), 0)
    col = lax.broadcasted_iota(jnp.int32, (steps, steps), 1)
    tri_chunks = (row >= col) & (row // CHUNK == col // CHUNK)
    eye = (lax.broadcasted_iota(jnp.int32, (CHUNK, CHUNK), 0)
           == lax.broadcasted_iota(jnp.int32, (CHUNK, CHUNK), 1)).astype(F32)

    conv = []
    for idx, ref in enumerate((q_ref, k_ref, v_ref)):
        cur = ref[...].astype(F32)
        width = cur.shape[1]
        tail = tail_ref[idx]
        cw = cw_ref[:, idx * width:(idx + 1) * width]
        acc = cur * cw[GDN_CONV - 1:GDN_CONV]
        for s in range(1, GDN_CONV):
            acc = acc + _shift_rows(cur, tail, s) * cw[GDN_CONV - 1 - s:GDN_CONV - s]
        tail_ref[idx] = cur[steps - SUBLANES:]
        conv.append(_silu(acc))
    q_all, k_all, v_all = conv

    gc = gc_ref[...]
    beta_c = jax.nn.sigmoid(gc)
    g_c = -jnp.exp(alc_ref[...]) * _softplus(gc + dtc_ref[...])
    gcum_c = _cumsum_rows(tri_chunks, g_c)
    gr = gr_ref[...].reshape(cps * 2 * heads, CHUNK)
    g_r = -jnp.exp(alr_ref[...]) * _softplus(gr + dtr_ref[...])
    gcum_r = _cumsum_lanes(g_r, tri_u)

    items = [(c, h) for c in range(cps) for h in range(heads)]
    pre = {}
    for c, h in items:
        rows = slice(c * CHUNK, (c + 1) * CHUNK)
        q = q_all[rows, h * dk:(h + 1) * dk]
        k = k_all[rows, h * dk:(h + 1) * dk]
        v = v_all[rows, h * dv:(h + 1) * dv]
        q = q * lax.rsqrt(jnp.sum(q * q, axis=-1, keepdims=True) + EPS) * dk ** -0.5
        k = k * lax.rsqrt(jnp.sum(k * k, axis=-1, keepdims=True) + EPS)
        beta = beta_c[rows, h:h + 1]
        gcol = gcum_c[rows, heads + h:heads + h + 1]
        grow = gcum_r[c * 2 * heads + heads + h:c * 2 * heads + heads + h + 1, :]
        glast = gcol[CHUNK - 1:CHUNK, :]
        decay = jnp.where(tri, jnp.exp(gcol - grow), 0.0)
        egc = jnp.exp(gcol)
        kb = k * beta
        pre[c, h] = dict(
            lower=jnp.where(strict, _mm_nt(kb, k) * decay, 0.0),
            rhs=jnp.concatenate([v * beta, kb * egc], axis=1).astype(BF16),
            qk=jnp.where(tri, _mm_nt(q, k) * decay, 0.0).astype(BF16),
            q_dec=(q * egc).astype(BF16),
            k_dec=(k * jnp.exp(glast - gcol)).astype(BF16),
            carry=jnp.exp(glast))

    power = {it: -pre[it]["lower"] for it in items}
    inv = {it: eye + power[it] for it in items}
    for _ in range(int(math.log2(CHUNK)) - 1):
        for it in items:
            power[it] = _mm(power[it], power[it])
        for it in items:
            inv[it] = inv[it] + _mm(inv[it], power[it])
    sol = {it: _mm(inv[it], pre[it]["rhs"]) for it in items}

    for c in range(cps):
        rows = slice(c * CHUNK, (c + 1) * CHUNK)
        state = [s_ref[h] for h in range(heads)]
        v_new = [sol[c, h][:, :dv] - _mm(sol[c, h][:, dv:], state[h]) for h in range(heads)]
        out = [_mm(pre[c, h]["q_dec"], state[h]) + _mm(pre[c, h]["qk"], v_new[h]) for h in range(heads)]
        for h in range(heads):
            s_ref[h] = state[h] * pre[c, h]["carry"] + _mm_tn(pre[c, h]["k_dec"], v_new[h])
        for h in range(heads):
            o = _rms_norm(out[h], ng_ref[...]) * _silu(z_ref[rows, h * dv:(h + 1) * dv].astype(F32))
            o_ref[rows, h * dv:(h + 1) * dv] = o.astype(BF16)


def _gate_rows(gates, count):
    b, l, _ = gates.shape
    return gates[:, :, :count].reshape(b, l // CHUNK, CHUNK, count).transpose(0, 1, 3, 2)


def _lane_row(vec, offset):
    return jnp.zeros((1, LANES), F32).at[0, offset:offset + vec.shape[0]].set(vec)


def _gdn_mixer(zin, gates, conv_w, a_log, dt_bias, norm_g, *, col0, cps=MIXER_CHUNKS_PER_STEP):
    batch, seq, _ = zin.shape
    heads = GDN_HEADS
    qk_w, v_w = heads * GDN_DK, heads * GDN_DV
    assert qk_w == v_w and col0 % qk_w == 0
    c0 = col0 // qk_w
    steps = cps * CHUNK
    zeros = jnp.zeros((heads,), F32)
    al_c, dt_c = _lane_row(a_log, heads), _lane_row(dt_bias, heads)
    al_r = jnp.tile(jnp.concatenate([zeros, a_log]), cps).reshape(cps * 2 * heads, 1)
    dt_r = jnp.tile(jnp.concatenate([zeros, dt_bias]), cps).reshape(cps * 2 * heads, 1)
    tok = lambda blk: pl.BlockSpec((None, steps, qk_w), lambda b, n, blk=blk: (b, n, c0 + blk))
    const = lambda *shape: pl.BlockSpec(shape, lambda b, n: (0,) * len(shape))
    return pl.pallas_call(
        functools.partial(_gdn_kernel, cps=cps),
        grid=(batch, seq // steps),
        in_specs=[
            tok(0), tok(1), tok(2), tok(3),
            pl.BlockSpec((None, steps, LANES), lambda b, n: (b, n, 0)),
            pl.BlockSpec((None, cps, 2 * heads, CHUNK), lambda b, n: (b, n, 0, 0)),
            const(GDN_CONV, 2 * qk_w + v_w),
            const(1, LANES), const(1, LANES), const(cps * 2 * heads, 1), const(cps * 2 * heads, 1),
            const(1, GDN_DV),
        ],
        out_specs=pl.BlockSpec((None, steps, v_w), lambda b, n: (b, n, 0)),
        out_shape=jax.ShapeDtypeStruct((batch, seq, v_w), BF16),
        scratch_shapes=[
            pltpu.VMEM((heads, GDN_DK, GDN_DV), F32),
            pltpu.VMEM((3, SUBLANES, qk_w), F32),
        ],
        compiler_params=_params("parallel", "arbitrary"),
        name="gdn",
    )(zin, zin, zin, zin, gates, _gate_rows(gates, 2 * heads), conv_w, al_c, dt_c, al_r, dt_r,
      norm_g.reshape(1, GDN_DV))


def _mlstm_kernel(q_ref, k_ref, v_ref, op_ref, gc_ref, gr_ref, bc_ref, br_ref, ng_ref, o_ref,
                  c_ref, n_ref, m_ref, *, cps):
    heads, dk, dv = MLSTM_HEADS, MLSTM_DK, MLSTM_DV
    steps = cps * CHUNK

    @pl.when(pl.program_id(1) == 0)
    def _():
        c_ref[...] = jnp.zeros_like(c_ref)
        n_ref[...] = jnp.zeros_like(n_ref)
        m_ref[...] = jnp.zeros_like(m_ref)

    tri, _, tri_u = _tri_masks(CHUNK)
    row = lax.broadcasted_iota(jnp.int32, (steps, steps), 0)
    col = lax.broadcasted_iota(jnp.int32, (steps, steps), 1)
    tri_chunks = (row >= col) & (row // CHUNK == col // CHUNK)
    pre_c = gc_ref[...] + bc_ref[...]
    bcum_c = _cumsum_rows(tri_chunks, -_softplus(-pre_c))
    pre_r = gr_ref[...].reshape(cps * 2 * heads, CHUNK) + br_ref[...]
    bcum_r = _cumsum_lanes(-_softplus(-pre_r), tri_u)

    items = [(c, h) for c in range(cps) for h in range(heads)]
    q_s = {it: q_ref[it[0] * CHUNK:(it[0] + 1) * CHUNK, it[1] * dk:(it[1] + 1) * dk].astype(F32) * dk ** -0.5
           for it in items}
    k_s = {it: k_ref[it[0] * CHUNK:(it[0] + 1) * CHUNK, it[1] * dk:(it[1] + 1) * dk].astype(F32) for it in items}
    v_s = {it: v_ref[it[0] * CHUNK:(it[0] + 1) * CHUNK, it[1] * dv:(it[1] + 1) * dv].astype(BF16) for it in items}
    qk = {it: _mm_nt(q_s[it], k_s[it]) for it in items}
    gate = {}
    for c, h in items:
        rows = slice(c * CHUNK, (c + 1) * CHUNK)
        ig_col = pre_c[rows, h:h + 1]
        ig_row = pre_r[c * 2 * heads + h:c * 2 * heads + h + 1, :]
        b_col = bcum_c[rows, heads + h:heads + h + 1]
        b_row = bcum_r[c * 2 * heads + heads + h:c * 2 * heads + heads + h + 1, :]
        b_last = b_col[CHUNK - 1:CHUNK, :]
        intra_log = jnp.where(tri, b_col - b_row + ig_row, -jnp.inf)
        upd_log = b_last - b_col + ig_col
        gate[c, h] = dict(b_col=b_col, b_last=b_last, intra_log=intra_log, upd_log=upd_log,
                          intra_max=jnp.max(intra_log, axis=-1, keepdims=True),
                          upd_max=jnp.max(upd_log, axis=0, keepdims=True))

    m_run = [m_ref[h][0:1, 0:1] for h in range(heads)]
    for c, h in items:
        g = gate[c, h]
        m_new = jnp.maximum(g["b_last"] + m_run[h], g["upd_max"])
        g.update(m_in=m_run[h], m_out=m_new, carry=jnp.exp(g["b_last"] + m_run[h] - m_new))
        m_run[h] = m_new

    m_s, inter_w, s_mat = {}, {}, {}
    for it in items:
        g = gate[it]
        inter_log = g["b_col"] + g["m_in"]
        m_s[it] = jnp.maximum(inter_log, g["intra_max"])
        inter_w[it] = jnp.exp(inter_log - m_s[it])
        s_mat[it] = qk[it] * jnp.exp(g["intra_log"] - m_s[it])
    s_v = {it: _mm(s_mat[it], v_s[it]) for it in items}
    s_sum = {it: jnp.sum(s_mat[it], axis=-1, keepdims=True) for it in items}
    kw = {it: k_s[it] * jnp.exp(gate[it]["upd_log"] - gate[it]["m_out"]) for it in items}
    kw_v = {it: _mm_tn(kw[it], v_s[it]) for it in items}
    kw_sum = {it: jnp.sum(kw[it], axis=0, keepdims=True) for it in items}

    for c in range(cps):
        rows = slice(c * CHUNK, (c + 1) * CHUNK)
        c_mat = [c_ref[h] for h in range(heads)]
        n_vec = [n_ref[h][0:1, :] for h in range(heads)]
        q_c = [_mm(q_s[c, h], c_mat[h]) for h in range(heads)]
        for h in range(heads):
            c_ref[h] = gate[c, h]["carry"] * c_mat[h] + kw_v[c, h]
            n_ref[h] = jnp.broadcast_to(gate[c, h]["carry"] * n_vec[h] + kw_sum[c, h], (SUBLANES, dk))
        for h in range(heads):
            it = (c, h)
            num = inter_w[it] * q_c[h] + s_v[it]
            den = inter_w[it] * jnp.sum(q_s[it] * n_vec[h], axis=-1, keepdims=True) + s_sum[it]
            hid = num / jnp.maximum(jnp.abs(den), jnp.exp(-m_s[it]))
            out_gate = jax.nn.sigmoid(op_ref[rows, h * dv:(h + 1) * dv].astype(F32))
            o_ref[rows, h * dv:(h + 1) * dv] = (out_gate * _rms_norm(hid, ng_ref[...])).astype(BF16)
    for h in range(heads):
        m_ref[h] = jnp.broadcast_to(m_run[h], (SUBLANES, LANES))


def _mlstm_mixer(zin, gates, gate_bias, norm_g, *, cps=MIXER_CHUNKS_PER_STEP):
    batch, seq, _ = zin.shape
    heads = MLSTM_HEADS
    qk_w, v_w = heads * MLSTM_DK, heads * MLSTM_DV
    steps = cps * CHUNK
    bias = gate_bias.reshape(2 * heads)
    tok = lambda w, blk: pl.BlockSpec((None, steps, w), lambda b, n: (b, n, blk))
    const = lambda *shape: pl.BlockSpec(shape, lambda b, n: (0,) * len(shape))
    return pl.pallas_call(
        functools.partial(_mlstm_kernel, cps=cps),
        grid=(batch, seq // steps),
        in_specs=[
            tok(qk_w, 0), tok(qk_w, 1), tok(v_w, 1), tok(v_w, 2),
            pl.BlockSpec((None, steps, LANES), lambda b, n: (b, n, 0)),
            pl.BlockSpec((None, cps, 2 * heads, CHUNK), lambda b, n: (b, n, 0, 0)),
            const(1, LANES), const(cps * 2 * heads, 1), const(1, MLSTM_DV),
        ],
        out_specs=pl.BlockSpec((None, steps, v_w), lambda b, n: (b, n, 0)),
        out_shape=jax.ShapeDtypeStruct((batch, seq, v_w), BF16),
        scratch_shapes=[
            pltpu.VMEM((heads, MLSTM_DK, MLSTM_DV), F32),
            pltpu.VMEM((heads, SUBLANES, MLSTM_DK), F32),
            pltpu.VMEM((heads, SUBLANES, LANES), F32),
        ],
        compiler_params=_params("parallel", "arbitrary"),
        name="mlstm",
    )(zin, zin, zin, zin, gates, _gate_rows(gates, 2 * heads), _lane_row(bias, 0),
      jnp.tile(bias, cps).reshape(cps * 2 * heads, 1), norm_g.reshape(1, MLSTM_DV))


def _ret_kernel(q_ref, k_ref, v_ref, g_ref, pos_ref, freq_ref, dmat_ref, xi_ref, zeta_ref, gam_ref, ng_ref,
                o_ref, s_ref, *, cps):
    heads, dk, dv = RET_HEADS, RET_DK, RET_DV
    half = dk // 2
    steps = cps * CHUNK

    @pl.when(pl.program_id(1) == 0)
    def _():
        s_ref[...] = jnp.zeros_like(s_ref)

    ang = pos_ref[...].reshape(steps, 1).astype(F32) * freq_ref[...]
    cos, sin = jnp.cos(ang), jnp.sin(ang)
    lane = lax.broadcasted_iota(jnp.int32, (steps, dk), 1)
    sin_signed = jnp.where(lane < half, -sin, sin)

    def rotary(x):
        return x * cos + pltpu.roll(x, half, 1) * sin_signed

    q_rot = [rotary(q_ref[:, h * dk:(h + 1) * dk].astype(F32)) * dk ** -0.5 for h in range(heads)]
    k_rot = [rotary(k_ref[:, h * dk:(h + 1) * dk].astype(F32)) for h in range(heads)]
    items = [(c, h) for c in range(cps) for h in range(heads)]
    rows = {c: slice(c * CHUNK, (c + 1) * CHUNK) for c in range(cps)}
    q_s = {(c, h): q_rot[h][rows[c]].astype(BF16) for c, h in items}
    k_s = {(c, h): k_rot[h][rows[c]] for c, h in items}
    v_s = {(c, h): v_ref[rows[c], h * dv:(h + 1) * dv].astype(BF16) for c, h in items}
    qk = {it: _mm_nt(q_s[it], k_s[it]) * dmat_ref[it[1]] for it in items}
    intra = {it: _mm(qk[it], v_s[it]) for it in items}
    k_v = {it: _mm_tn(k_s[it] * zeta_ref[it[1]], v_s[it]) for it in items}

    for c in range(cps):
        state = [s_ref[h] for h in range(heads)]
        inter = [_mm(q_s[c, h], state[h]) for h in range(heads)]
        for h in range(heads):
            s_ref[h] = state[h] * gam_ref[h] + k_v[c, h]
        for h in range(heads):
            y = intra[c, h] + inter[h] * xi_ref[h]
            mu = jnp.mean(y, axis=-1, keepdims=True)
            yc = y - mu
            var = jnp.mean(yc * yc, axis=-1, keepdims=True)
            y = yc * lax.rsqrt(var + EPS) * ng_ref[...]
            o_ref[rows[c], h * dv:(h + 1) * dv] = (y * _silu(g_ref[rows[c], h * dv:(h + 1) * dv].astype(F32))).astype(BF16)


def _ret_mixer(zin, positions, norm_g, *, col0, cps=MIXER_CHUNKS_PER_STEP):
    batch, seq, _ = zin.shape
    heads, dk = RET_HEADS, RET_DK
    qk_w, v_w = heads * RET_DK, heads * RET_DV
    assert col0 % v_w == 0
    cq, cv = col0 // qk_w, col0 // v_w
    half = dk // 2
    steps = cps * CHUNK
    inv_freq = ROPE_BASE ** (-jnp.arange(half, dtype=F32) / half)
    freq = jnp.concatenate([inv_freq, inv_freq]).reshape(1, dk)
    log_gamma = jnp.log1p(-jnp.exp2(-5.0 - jnp.arange(heads, dtype=F32)))
    idx = jnp.arange(CHUNK, dtype=F32)
    tri = jnp.tril(jnp.ones((CHUNK, CHUNK), dtype=bool))
    diff = jnp.where(tri, idx[:, None] - idx[None, :], 0.0)
    dmat = jnp.where(tri, jnp.exp(diff * log_gamma[:, None, None]), 0.0)
    xi = jnp.exp((idx + 1.0) * log_gamma[:, None])[:, :, None]
    zeta = jnp.exp((CHUNK - 1.0 - idx) * log_gamma[:, None])[:, :, None]
    gamma_c = jnp.broadcast_to(jnp.exp(CHUNK * log_gamma)[:, None, None], (heads, 1, RET_DV))
    pos = positions.reshape(batch, seq // CHUNK, CHUNK, 1)
    tok = lambda w, blk: pl.BlockSpec((None, steps, w), lambda b, n: (b, n, blk))
    const = lambda *shape: pl.BlockSpec(shape, lambda b, n: (0,) * len(shape))
    return pl.pallas_call(
        functools.partial(_ret_kernel, cps=cps),
        grid=(batch, seq // steps),
        in_specs=[
            tok(qk_w, cq), tok(qk_w, cq + 1), tok(v_w, cv + 1), tok(v_w, cv + 2),
            pl.BlockSpec((None, cps, CHUNK, 1), lambda b, n: (b, n, 0, 0)),
            const(1, dk), const(heads, CHUNK, CHUNK), const(heads, CHUNK, 1), const(heads, CHUNK, 1),
            const(heads, 1, RET_DV), const(1, RET_DV),
        ],
        out_specs=pl.BlockSpec((None, steps, v_w), lambda b, n: (b, n, 0)),
        out_shape=jax.ShapeDtypeStruct((batch, seq, v_w), BF16),
        scratch_shapes=[pltpu.VMEM((heads, RET_DK, RET_DV), F32)],
        compiler_params=_params("parallel", "arbitrary"),
        name="retention",
    )(zin, zin, zin, zin, pos, freq, dmat, xi, zeta, gamma_c, norm_g.reshape(1, RET_DV))


def _pad_lanes(w):
    return jnp.pad(w, ((0, 0), (0, LANES - w.shape[1])))


def _even_mixer(x, batch, norm_g, w_in, w_out, a_re, a_im, log_step, b_re, b_im, c_re, c_im, d_skip, w_glu,
                b_glu, conv_w, a_log, dt_bias, gdn_g):
    s5_w = a_re.shape[0] * S5_GROUP
    main_w = s5_w + GDN_HEADS * (2 * GDN_DK + 2 * GDN_DV)
    w_gate = _pad_lanes(w_in[:, main_w:]).astype(BF16)
    zin, gates = _inproj(x, norm_g, [(w_in, main_w)], w_gate)
    zin = zin.reshape(batch, -1, main_w)
    gates = gates.reshape(batch, -1, LANES)
    ya = _s5_mixer(zin, a_re, a_im, log_step, b_re, b_im, c_re, c_im, d_skip, w_glu, b_glu)
    yb = _gdn_mixer(zin, gates, conv_w, a_log, dt_bias, gdn_g, col0=s5_w)
    m = x.shape[0]
    w_out = w_out.astype(BF16)
    return _outproj(x, ya.reshape(m, -1), yb.reshape(m, -1), w_out[:s5_w], w_out[s5_w:])


def _odd_mixer(x, batch, positions, norm_g, w_in, w_out, gate_bias, mlstm_g, ret_g):
    c_main = MLSTM_HEADS * (2 * MLSTM_DK + 2 * MLSTM_DV)
    r_main = RET_HEADS * (2 * RET_DK + 2 * RET_DV)
    n_gate = 2 * MLSTM_HEADS
    w_ret = w_in[:, c_main + n_gate:]
    w_gate = _pad_lanes(w_in[:, c_main:c_main + n_gate]).astype(BF16)
    zin, gates = _inproj(x, norm_g, [(w_in, c_main), (w_ret, r_main)], w_gate)
    zin = zin.reshape(batch, -1, c_main + r_main)
    gates = gates.reshape(batch, -1, LANES)
    yc = _mlstm_mixer(zin, gates, gate_bias, mlstm_g)
    yd = _ret_mixer(zin, positions, ret_g, col0=c_main)
    m = x.shape[0]
    w_out = w_out.astype(BF16)
    split = MLSTM_HEADS * MLSTM_DV
    return _outproj(x, yc.reshape(m, -1), yd.reshape(m, -1), w_out[:split], w_out[split:])


def kernel(x, positions, ffn_norm, ffn_w1, ffn_w3, ffn_w2, mix_norm, even_w_in, even_w_out, s5_a_re, s5_a_im, s5_log_step, s5_b_re, s5_b_im, s5_c_re, s5_c_im, s5_d, s5_w_glu, s5_b_glu, gdn_conv_w, gdn_a_log, gdn_dt_bias, gdn_norm, odd_w_in, odd_w_out, mlstm_gate_bias, mlstm_norm, ret_norm, final_norm):
    batch, seq, d = x.shape
    depth = ffn_norm.shape[0]
    x = x.reshape(batch * seq, d)

    def ffn(x, layer, which, final_g=None):
        return _ffn(x, ffn_norm[layer, which], ffn_w1, ffn_w3, ffn_w2, layer, which, final_g)

    for layer in range(depth):
        x = ffn(x, layer, 0)
        j = layer // 2
        if layer % 2 == 0:
            x = _even_mixer(x, batch, mix_norm[layer], even_w_in[j], even_w_out[j], s5_a_re[j], s5_a_im[j],
                            s5_log_step[j], s5_b_re[j], s5_b_im[j], s5_c_re[j], s5_c_im[j], s5_d[j].reshape(-1),
                            s5_w_glu[j], s5_b_glu[j], gdn_conv_w[j], gdn_a_log[j], gdn_dt_bias[j], gdn_norm[j])
        else:
            x = _odd_mixer(x, batch, positions, mix_norm[layer], odd_w_in[j], odd_w_out[j], mlstm_gate_bias[j],
                           mlstm_norm[j], ret_norm[j])
        x = ffn(x, layer, 1, final_norm if layer == depth - 1 else None)
    return x.reshape(batch, seq, d)
```

```python
import functools
import math

import jax
import jax.numpy as jnp
from jax import lax
from jax.experimental import pallas as pl
from jax.experimental.pallas import tpu as pltpu

F32 = jnp.float32
BF16 = jnp.bfloat16

EPS = 1e-6
CHUNK = 64
ROPE_BASE = 10000.0
LANES = 128
SUBLANES = 8
VMEM_LIMIT_BYTES = 60 * 1024 * 1024

S5_GROUP = 16
S5_STATE = 64
S5_TILE_GROUPS = LANES // S5_GROUP
S5_PAIRS = S5_TILE_GROUPS // 2
S5_TIME = 256
S5_PITCH = S5_TIME + 4

GDN_HEADS = 8
GDN_DK = 128
GDN_DV = 128
GDN_CONV = 4
MIXER_CHUNKS_PER_STEP = 4
MLSTM_HEADS = 4
MLSTM_DK = 128
MLSTM_DV = 256
RET_HEADS = 4
RET_DK = 128
RET_DV = 256


def _params(*semantics):
    return pltpu.CompilerParams(dimension_semantics=semantics, vmem_limit_bytes=VMEM_LIMIT_BYTES)


def _mm(a, b):
    return jnp.dot(a.astype(BF16), b.astype(BF16), preferred_element_type=F32)


def _mm_nt(a, b):
    return lax.dot_general(a.astype(BF16), b.astype(BF16), (((1,), (1,)), ((), ())), preferred_element_type=F32)


def _mm_tn(a, b):
    return lax.dot_general(a.astype(BF16), b.astype(BF16), (((0,), (0,)), ((), ())), preferred_element_type=F32)


def _split3(x):
    x1 = x.astype(BF16)
    r1 = x - x1.astype(F32)
    x2 = r1.astype(BF16)
    x3 = (r1 - x2.astype(F32)).astype(BF16)
    return x1, x2, x3


def _cumsum_rows(tri_lower, x):
    t = tri_lower.astype(BF16)
    return sum(jnp.dot(t, p, preferred_element_type=F32) for p in _split3(x))


def _cumsum_lanes(x, tri_upper):
    t = tri_upper.astype(BF16)
    return sum(jnp.dot(p, t, preferred_element_type=F32) for p in _split3(x))


def _rms_norm(x, g):
    return x * lax.rsqrt(jnp.mean(x * x, axis=-1, keepdims=True) + EPS) * g


def _silu(x):
    return x * jax.nn.sigmoid(x)


def _softplus(x):
    return jnp.maximum(x, 0.0) + jnp.log1p(jnp.exp(-jnp.abs(x)))


def _tri_masks(n):
    r = lax.broadcasted_iota(jnp.int32, (n, n), 0)
    c = lax.broadcasted_iota(jnp.int32, (n, n), 1)
    return r >= c, r > c, r <= c


def _ffn_kernel(x_ref, g_ref, w1_ref, w3_ref, w2_ref, fg_ref, o_ref, h_ref, *, final_norm):
    j = pl.program_id(1)

    @pl.when(j == 0)
    def _():
        x = x_ref[...]
        h_ref[...] = _rms_norm(x, g_ref[...]).astype(BF16)
        o_ref[...] = x

    h = h_ref[...]
    a = jnp.dot(h, w1_ref[...].astype(BF16), preferred_element_type=F32)
    b = jnp.dot(h, w3_ref[...].astype(BF16), preferred_element_type=F32)
    o_ref[...] += 0.5 * jnp.dot((_silu(a) * b).astype(BF16), w2_ref[...].astype(BF16), preferred_element_type=F32)

    if final_norm:
        @pl.when(j == pl.num_programs(1) - 1)
        def _():
            o_ref[...] = _rms_norm(o_ref[...], fg_ref[...])


def _ffn(x, g, w1, w3, w2, layer, which, final_g=None, *, tm=1024, tf=256):
    m, d = x.shape
    f = w1.shape[-1]
    final_norm = final_g is not None
    fg = final_g if final_norm else g
    return pl.pallas_call(
        functools.partial(_ffn_kernel, final_norm=final_norm),
        grid=(m // tm, f // tf),
        in_specs=[
            pl.BlockSpec((tm, d), lambda i, j: (i, 0)),
            pl.BlockSpec((1, d), lambda i, j: (0, 0)),
            pl.BlockSpec((None, None, d, tf), lambda i, j: (layer, which, 0, j)),
            pl.BlockSpec((None, None, d, tf), lambda i, j: (layer, which, 0, j)),
            pl.BlockSpec((None, None, tf, d), lambda i, j: (layer, which, j, 0)),
            pl.BlockSpec((1, d), lambda i, j: (0, 0)),
        ],
        out_specs=pl.BlockSpec((tm, d), lambda i, j: (i, 0)),
        out_shape=jax.ShapeDtypeStruct((m, d), F32),
        scratch_shapes=[pltpu.VMEM((tm, d), BF16)],
        compiler_params=_params("parallel", "arbitrary"),
        name="ffn",
    )(x, g.reshape(1, d), w1, w3, w2, fg.reshape(1, d))


def _inproj_kernel(*refs, bounds):
    x_ref, g_ref = refs[:2]
    w_refs = refs[2:2 + len(bounds)]
    wg_ref, o_ref, og_ref, h_ref = refs[2 + len(bounds):]
    j = pl.program_id(1)

    @pl.when(j == 0)
    def _():
        h = _rms_norm(x_ref[...], g_ref[...]).astype(BF16)
        h_ref[...] = h
        og_ref[...] = _mm_nt(h, wg_ref[...])

    for w_ref, (start, count) in zip(w_refs, bounds):
        @pl.when((j >= start) & (j < start + count))
        def _(w_ref=w_ref):
            o_ref[...] = _mm_nt(h_ref[...], w_ref[...]).astype(BF16)


def _inproj(x, g, segments, w_gate, *, tm=1024, tn=512):
    m, d = x.shape
    bounds, start = [], 0
    for _, width in segments:
        bounds.append((start, width // tn))
        start += width // tn
    n = start * tn

    def w_spec(first, count):
        return pl.BlockSpec((tn, d), lambda i, j: (jnp.clip(j - first, 0, count - 1), 0))

    return pl.pallas_call(
        functools.partial(_inproj_kernel, bounds=tuple(bounds)),
        grid=(m // tm, n // tn),
        in_specs=[
            pl.BlockSpec((tm, d), lambda i, j: (i, 0)),
            pl.BlockSpec((1, d), lambda i, j: (0, 0)),
            *[w_spec(first, count) for first, count in bounds],
            pl.BlockSpec((LANES, d), lambda i, j: (0, 0)),
        ],
        out_specs=[
            pl.BlockSpec((tm, tn), lambda i, j: (i, j)),
            pl.BlockSpec((tm, LANES), lambda i, j: (i, 0)),
        ],
        out_shape=[jax.ShapeDtypeStruct((m, n), BF16), jax.ShapeDtypeStruct((m, LANES), F32)],
        scratch_shapes=[pltpu.VMEM((tm, d), BF16)],
        compiler_params=_params("parallel", "arbitrary"),
        name="inproj",
    )(x, g.reshape(1, d), *[w for w, _ in segments], w_gate)


def _outproj_kernel(x_ref, ya_ref, yb_ref, wa_ref, wb_ref, o_ref):
    o_ref[...] = (x_ref[...] + jnp.dot(ya_ref[...], wa_ref[...], preferred_element_type=F32)
                  + jnp.dot(yb_ref[...], wb_ref[...], preferred_element_type=F32))


def _outproj(x, ya, yb, wa, wb, *, tm=1024, tn=1024):
    m, d = x.shape
    ka, kb = ya.shape[1], yb.shape[1]
    return pl.pallas_call(
        _outproj_kernel,
        grid=(m // tm, d // tn),
        in_specs=[
            pl.BlockSpec((tm, tn), lambda i, j: (i, j)),
            pl.BlockSpec((tm, ka), lambda i, j: (i, 0)),
            pl.BlockSpec((tm, kb), lambda i, j: (i, 0)),
            pl.BlockSpec((ka, tn), lambda i, j: (0, j)),
            pl.BlockSpec((kb, tn), lambda i, j: (0, j)),
        ],
        out_specs=pl.BlockSpec((tm, tn), lambda i, j: (i, j)),
        out_shape=jax.ShapeDtypeStruct((m, d), F32),
        compiler_params=_params("parallel", "arbitrary"),
        name="outproj",
    )(x, ya, yb, wa, wb)


def _s5_disc_kernel(are_ref, aim_ref, step_ref, bre_ref, bim_ref, lr_ref, li_ref, bbr_ref, bbi_ref):
    ar, ai = are_ref[...], aim_ref[...]
    step = jnp.exp(step_ref[...])
    mag = jnp.exp(ar * step)
    lr, li = mag * jnp.cos(ai * step), mag * jnp.sin(ai * step)
    den = ar * ar + ai * ai
    fr = ((lr - 1.0) * ar + li * ai) / den
    fi = (li * ar - (lr - 1.0) * ai) / den
    lr_ref[...] = lr
    li_ref[...] = li
    br, bi = bre_ref[...], bim_ref[...]
    bbr_ref[...] = fr[None] * br - fi[None] * bi
    bbi_ref[...] = fr[None] * bi + fi[None] * br


def _s5_discretise(a_re, a_im, log_step, b_re, b_im):
    g, p = a_re.shape
    h = b_re.shape[-1]
    sd = jax.ShapeDtypeStruct
    return pl.pallas_call(
        _s5_disc_kernel,
        out_shape=[sd((g, p), F32), sd((g, p), F32), sd((h, g, p), F32), sd((h, g, p), F32)],
        name="s5_discretise",
    )(a_re, a_im, log_step.reshape(g, 1), b_re.transpose(2, 0, 1), b_im.transpose(2, 0, 1))


def _s5_kernel(u_ref, wb_ref, lr_ref, li_ref, cs_ref, dsk_ref, wglu_ref, bglu_ref, o_ref,
               rre_ref, rim_ref, xr_ref, xi_ref, y_ref, *, batch, tiles):
    chains = tiles * S5_PAIRS * batch

    @pl.when(pl.program_id(0) == 0)
    def _():
        xr_ref[...] = jnp.zeros_like(xr_ref)
        xi_ref[...] = jnp.zeros_like(xi_ref)

    def chain_rows(j, pair, b):
        return pl.ds(((j * S5_PAIRS + pair) * batch + b) * S5_PITCH, S5_TIME)

    def u_tile(j):
        return jnp.concatenate([u_ref[b, :, j * LANES:(j + 1) * LANES] for b in range(batch)], axis=0)

    for j in range(tiles):
        drive = jnp.dot(u_tile(j).astype(BF16), wb_ref[j], preferred_element_type=F32)
        for b in range(batch):
            rows = slice(b * S5_TIME, (b + 1) * S5_TIME)
            for pair in range(S5_PAIRS):
                rre_ref[chain_rows(j, pair, b), :] = drive[rows, pair * LANES:(pair + 1) * LANES]
                rim_ref[chain_rows(j, pair, b), :] = drive[rows, (S5_PAIRS + pair) * LANES:(S5_PAIRS + pair + 1) * LANES]

    lr, li = lr_ref[...], li_ref[...]

    def step(t, carry):
        xr, xi = carry
        rows = pl.ds(t, chains, stride=S5_PITCH)
        nxr = lr * xr - li * xi + rre_ref[rows, :]
        nxi = lr * xi + li * xr + rim_ref[rows, :]
        rre_ref[rows, :] = nxr
        rim_ref[rows, :] = nxi
        return nxr, nxi

    xr, xi = lax.fori_loop(0, S5_TIME, step, (xr_ref[...], xi_ref[...]), unroll=4)
    xr_ref[...] = xr
    xi_ref[...] = xi

    for j in range(tiles):
        states = jnp.concatenate([
            jnp.concatenate(
                [rre_ref[chain_rows(j, pair, b), :].astype(BF16) for pair in range(S5_PAIRS)]
                + [rim_ref[chain_rows(j, pair, b), :].astype(BF16) for pair in range(S5_PAIRS)], axis=1)
            for b in range(batch)], axis=0)
        cols = slice(j * LANES, (j + 1) * LANES)
        y = jnp.dot(states, cs_ref[j], preferred_element_type=F32) + dsk_ref[:, cols] * u_tile(j).astype(F32)
        y_ref[:, cols] = jax.nn.gelu(y)

    y = y_ref[...]
    gate = jnp.dot(y.astype(BF16), wglu_ref[...], preferred_element_type=F32) + bglu_ref[...]
    out = (y * jax.nn.sigmoid(gate)).astype(BF16)
    for b in range(batch):
        o_ref[b] = out[b * S5_TIME:(b + 1) * S5_TIME]


def _s5_mixer(zin, a_re, a_im, log_step, b_re, b_im, c_re, c_im, d_skip, w_glu, b_glu):
    batch, seq, _ = zin.shape
    groups, states = a_re.shape
    width = groups * S5_GROUP
    tiles = width // LANES
    chains = tiles * S5_PAIRS * batch
    lr, li, bbr, bbi = _s5_discretise(a_re, a_im, log_step, b_re, b_im)

    eye = jnp.eye(S5_TILE_GROUPS, dtype=F32)
    bb = jnp.stack([bbr, bbi]).reshape(2, S5_GROUP, tiles, S5_TILE_GROUPS, states)
    wb = jnp.einsum("rhjgp,gk->jghrkp", bb, eye).reshape(tiles, LANES, 2 * S5_TILE_GROUPS * states).astype(BF16)
    cc = jnp.stack([c_re, -c_im]).reshape(2, tiles, S5_TILE_GROUPS, S5_GROUP, states)
    cs = jnp.einsum("rjghp,gk->jrkpgh", cc, eye).reshape(tiles, 2 * S5_TILE_GROUPS * states, LANES).astype(BF16)

    def per_chain(lam):
        t = lam.reshape(tiles * S5_PAIRS, 1, 2 * states)
        return jnp.broadcast_to(t, (tiles * S5_PAIRS, batch, 2 * states)).reshape(chains, 2 * states)

    const = lambda *shape: pl.BlockSpec(shape, lambda i: (0,) * len(shape))
    return pl.pallas_call(
        functools.partial(_s5_kernel, batch=batch, tiles=tiles),
        grid=(seq // S5_TIME,),
        in_specs=[
            pl.BlockSpec((batch, S5_TIME, width), lambda i: (0, i, 0)),
            const(tiles, LANES, 2 * S5_TILE_GROUPS * states),
            const(chains, LANES),
            const(chains, LANES),
            const(tiles, 2 * S5_TILE_GROUPS * states, LANES),
            const(1, width),
            const(width, width),
            const(1, width),
        ],
        out_specs=pl.BlockSpec((batch, S5_TIME, width), lambda i: (0, i, 0)),
        out_shape=jax.ShapeDtypeStruct((batch, seq, width), BF16),
        scratch_shapes=[
            pltpu.VMEM((chains * S5_PITCH, LANES), F32),
            pltpu.VMEM((chains * S5_PITCH, LANES), F32),
            pltpu.VMEM((chains, LANES), F32),
            pltpu.VMEM((chains, LANES), F32),
            pltpu.VMEM((batch * S5_TIME, width), F32),
        ],
        compiler_params=_params("arbitrary"),
        name="s5",
    )(zin, wb, per_chain(lr), per_chain(li), cs, d_skip.reshape(1, width), w_glu.astype(BF16),
      b_glu.reshape(1, width))


def _shift_rows(cur, tail, s):
    rolled = pltpu.roll(cur, s, 0)
    rolled_tail = pltpu.roll(tail, s, 0)
    row = lax.broadcasted_iota(jnp.int32, tail.shape, 0)
    top = jnp.where(row < s, rolled_tail, rolled[:SUBLANES])
    return jnp.concatenate([top, rolled[SUBLANES:]], axis=0)


def _gdn_kernel(q_ref, k_ref, v_ref, z_ref, gc_ref, gr_ref, cw_ref, alc_ref, dtc_ref, alr_ref, dtr_ref,
                ng_ref, o_ref, s_ref, tail_ref, *, cps):
    heads, dk, dv = GDN_HEADS, GDN_DK, GDN_DV
    steps = cps * CHUNK

    @pl.when(pl.program_id(1) == 0)
    def _():
        s_ref[...] = jnp.zeros_like(s_ref)
        tail_ref[...] = jnp.zeros_like(tail_ref)

    tri, strict, tri_u = _tri_masks(CHUNK)
    row = lax.broadcasted_iota(jnp.int32, (steps, steps), 0)
    col = lax.broadcasted_iota(jnp.int32, (steps, steps), 1)
    tri_chunks = (row >= col) & (row // CHUNK == col // CHUNK)
    eye = (lax.broadcasted_iota(jnp.int32, (CHUNK, CHUNK), 0)
           == lax.broadcasted_iota(jnp.int32, (CHUNK, CHUNK), 1)).astype(F32)

    conv = []
    for idx, ref in enumerate((q_ref, k_ref, v_ref)):
        cur = ref[...].astype(F32)
        width = cur.shape[1]
        tail = tail_ref[idx]
        cw = cw_ref[:, idx * width:(idx + 1) * width]
        acc = cur * cw[GDN_CONV - 1:GDN_CONV]
        for s in range(1, GDN_CONV):
            acc = acc + _shift_rows(cur, tail, s) * cw[GDN_CONV - 1 - s:GDN_CONV - s]
        tail_ref[idx] = cur[steps - SUBLANES:]
        conv.append(_silu(acc))
    q_all, k_all, v_all = conv

    gc = gc_ref[...]
    beta_c = jax.nn.sigmoid(gc)
    g_c = -jnp.exp(alc_ref[...]) * _softplus(gc + dtc_ref[...])
    gcum_c = _cumsum_rows(tri_chunks, g_c)
    gr = gr_ref[...].reshape(cps * 2 * heads, CHUNK)
    g_r = -jnp.exp(alr_ref[...]) * _softplus(gr + dtr_ref[...])
    gcum_r = _cumsum_lanes(g_r, tri_u)

    items = [(c, h) for c in range(cps) for h in range(heads)]
    pre = {}
    for c, h in items:
        rows = slice(c * CHUNK, (c + 1) * CHUNK)
        q = q_all[rows, h * dk:(h + 1) * dk]
        k = k_all[rows, h * dk:(h + 1) * dk]
        v = v_all[rows, h * dv:(h + 1) * dv]
        q = q * lax.rsqrt(jnp.sum(q * q, axis=-1, keepdims=True) + EPS) * dk ** -0.5
        k = k * lax.rsqrt(jnp.sum(k * k, axis=-1, keepdims=True) + EPS)
        beta = beta_c[rows, h:h + 1]
        gcol = gcum_c[rows, heads + h:heads + h + 1]
        grow = gcum_r[c * 2 * heads + heads + h:c * 2 * heads + heads + h + 1, :]
        glast = gcol[CHUNK - 1:CHUNK, :]
        decay = jnp.where(tri, jnp.exp(gcol - grow), 0.0)
        egc = jnp.exp(gcol)
        kb = k * beta
        pre[c, h] = dict(
            lower=jnp.where(strict, _mm_nt(kb, k) * decay, 0.0),
            rhs=jnp.concatenate([v * beta, kb * egc], axis=1).astype(BF16),
            qk=jnp.where(tri, _mm_nt(q, k) * decay, 0.0).astype(BF16),
            q_dec=(q * egc).astype(BF16),
            k_dec=(k * jnp.exp(glast - gcol)).astype(BF16),
            carry=jnp.exp(glast))

    power = {it: -pre[it]["lower"] for it in items}
    inv = {it: eye + power[it] for it in items}
    for _ in range(int(math.log2(CHUNK)) - 1):
        for it in items:
            power[it] = _mm(power[it], power[it])
        for it in items:
            inv[it] = inv[it] + _mm(inv[it], power[it])
    sol = {it: _mm(inv[it], pre[it]["rhs"]) for it in items}

    for c in range(cps):
        rows = slice(c * CHUNK, (c + 1) * CHUNK)
        state = [s_ref[h] for h in range(heads)]
        v_new = [sol[c, h][:, :dv] - _mm(sol[c, h][:, dv:], state[h]) for h in range(heads)]
        out = [_mm(pre[c, h]["q_dec"], state[h]) + _mm(pre[c, h]["qk"], v_new[h]) for h in range(heads)]
        for h in range(heads):
            s_ref[h] = state[h] * pre[c, h]["carry"] + _mm_tn(pre[c, h]["k_dec"], v_new[h])
        for h in range(heads):
            o = _rms_norm(out[h], ng_ref[...]) * _silu(z_ref[rows, h * dv:(h + 1) * dv].astype(F32))
            o_ref[rows, h * dv:(h + 1) * dv] = o.astype(BF16)


def _gate_rows(gates, count):
    b, l, _ = gates.shape
    return gates[:, :, :count].reshape(b, l // CHUNK, CHUNK, count).transpose(0, 1, 3, 2)


def _lane_row(vec, offset):
    return jnp.zeros((1, LANES), F32).at[0, offset:offset + vec.shape[0]].set(vec)


def _gdn_mixer(zin, gates, conv_w, a_log, dt_bias, norm_g, *, col0, cps=MIXER_CHUNKS_PER_STEP):
    batch, seq, _ = zin.shape
    heads = GDN_HEADS
    qk_w, v_w = heads * GDN_DK, heads * GDN_DV
    assert qk_w == v_w and col0 % qk_w == 0
    c0 = col0 // qk_w
    steps = cps * CHUNK
    zeros = jnp.zeros((heads,), F32)
    al_c, dt_c = _lane_row(a_log, heads), _lane_row(dt_bias, heads)
    al_r = jnp.tile(jnp.concatenate([zeros, a_log]), cps).reshape(cps * 2 * heads, 1)
    dt_r = jnp.tile(jnp.concatenate([zeros, dt_bias]), cps).reshape(cps * 2 * heads, 1)
    tok = lambda blk: pl.BlockSpec((None, steps, qk_w), lambda b, n, blk=blk: (b, n, c0 + blk))
    const = lambda *shape: pl.BlockSpec(shape, lambda b, n: (0,) * len(shape))
    return pl.pallas_call(
        functools.partial(_gdn_kernel, cps=cps),
        grid=(batch, seq // steps),
        in_specs=[
            tok(0), tok(1), tok(2), tok(3),
            pl.BlockSpec((None, steps, LANES), lambda b, n: (b, n, 0)),
            pl.BlockSpec((None, cps, 2 * heads, CHUNK), lambda b, n: (b, n, 0, 0)),
            const(GDN_CONV, 2 * qk_w + v_w),
            const(1, LANES), const(1, LANES), const(cps * 2 * heads, 1), const(cps * 2 * heads, 1),
            const(1, GDN_DV),
        ],
        out_specs=pl.BlockSpec((None, steps, v_w), lambda b, n: (b, n, 0)),
        out_shape=jax.ShapeDtypeStruct((batch, seq, v_w), BF16),
        scratch_shapes=[
            pltpu.VMEM((heads, GDN_DK, GDN_DV), F32),
            pltpu.VMEM((3, SUBLANES, qk_w), F32),
        ],
        compiler_params=_params("parallel", "arbitrary"),
        name="gdn",
    )(zin, zin, zin, zin, gates, _gate_rows(gates, 2 * heads), conv_w, al_c, dt_c, al_r, dt_r,
      norm_g.reshape(1, GDN_DV))


def _mlstm_kernel(q_ref, k_ref, v_ref, op_ref, gc_ref, gr_ref, bc_ref, br_ref, ng_ref, o_ref,
                  c_ref, n_ref, m_ref, *, cps):
    heads, dk, dv = MLSTM_HEADS, MLSTM_DK, MLSTM_DV
    steps = cps * CHUNK

    @pl.when(pl.program_id(1) == 0)
    def _():
        c_ref[...] = jnp.zeros_like(c_ref)
        n_ref[...] = jnp.zeros_like(n_ref)
        m_ref[...] = jnp.zeros_like(m_ref)

    tri, _, tri_u = _tri_masks(CHUNK)
    row = lax.broadcasted_iota(jnp.int32, (steps, steps), 0)
    col = lax.broadcasted_iota(jnp.int32, (steps, steps), 1)
    tri_chunks = (row >= col) & (row // CHUNK == col // CHUNK)
    pre_c = gc_ref[...] + bc_ref[...]
    bcum_c = _cumsum_rows(tri_chunks, -_softplus(-pre_c))
    pre_r = gr_ref[...].reshape(cps * 2 * heads, CHUNK) + br_ref[...]
    bcum_r = _cumsum_lanes(-_softplus(-pre_r), tri_u)

    items = [(c, h) for c in range(cps) for h in range(heads)]
    q_s = {it: q_ref[it[0] * CHUNK:(it[0] + 1) * CHUNK, it[1] * dk:(it[1] + 1) * dk].astype(F32) * dk ** -0.5
           for it in items}
    k_s = {it: k_ref[it[0] * CHUNK:(it[0] + 1) * CHUNK, it[1] * dk:(it[1] + 1) * dk].astype(F32) for it in items}
    v_s = {it: v_ref[it[0] * CHUNK:(it[0] + 1) * CHUNK, it[1] * dv:(it[1] + 1) * dv].astype(BF16) for it in items}
    qk = {it: _mm_nt(q_s[it], k_s[it]) for it in items}
    gate = {}
    for c, h in items:
        rows = slice(c * CHUNK, (c + 1) * CHUNK)
        ig_col = pre_c[rows, h:h + 1]
        ig_row = pre_r[c * 2 * heads + h:c * 2 * heads + h + 1, :]
        b_col = bcum_c[rows, heads + h:heads + h + 1]
        b_row = bcum_r[c * 2 * heads + heads + h:c * 2 * heads + heads + h + 1, :]
        b_last = b_col[CHUNK - 1:CHUNK, :]
        intra_log = jnp.where(tri, b_col - b_row + ig_row, -jnp.inf)
        upd_log = b_last - b_col + ig_col
        gate[c, h] = dict(b_col=b_col, b_last=b_last, intra_log=intra_log, upd_log=upd_log,
                          intra_max=jnp.max(intra_log, axis=-1, keepdims=True),
                          upd_max=jnp.max(upd_log, axis=0, keepdims=True))

    m_run = [m_ref[h][0:1, 0:1] for h in range(heads)]
    for c, h in items:
        g = gate[c, h]
        m_new = jnp.maximum(g["b_last"] + m_run[h], g["upd_max"])
        g.update(m_in=m_run[h], m_out=m_new, carry=jnp.exp(g["b_last"] + m_run[h] - m_new))
        m_run[h] = m_new

    m_s, inter_w, s_mat = {}, {}, {}
    for it in items:
        g = gate[it]
        inter_log = g["b_col"] + g["m_in"]
        m_s[it] = jnp.maximum(inter_log, g["intra_max"])
        inter_w[it] = jnp.exp(inter_log - m_s[it])
        s_mat[it] = qk[it] * jnp.exp(g["intra_log"] - m_s[it])
    s_v = {it: _mm(s_mat[it], v_s[it]) for it in items}
    s_sum = {it: jnp.sum(s_mat[it], axis=-1, keepdims=True) for it in items}
    kw = {it: k_s[it] * jnp.exp(gate[it]["upd_log"] - gate[it]["m_out"]) for it in items}
    kw_v = {it: _mm_tn(kw[it], v_s[it]) for it in items}
    kw_sum = {it: jnp.sum(kw[it], axis=0, keepdims=True) for it in items}

    for c in range(cps):
        rows = slice(c * CHUNK, (c + 1) * CHUNK)
        c_mat = [c_ref[h] for h in range(heads)]
        n_vec = [n_ref[h][0:1, :] for h in range(heads)]
        q_c = [_mm(q_s[c, h], c_mat[h]) for h in range(heads)]
        for h in range(heads):
            c_ref[h] = gate[c, h]["carry"] * c_mat[h] + kw_v[c, h]
            n_ref[h] = jnp.broadcast_to(gate[c, h]["carry"] * n_vec[h] + kw_sum[c, h], (SUBLANES, dk))
        for h in range(heads):
            it = (c, h)
            num = inter_w[it] * q_c[h] + s_v[it]
            den = inter_w[it] * jnp.sum(q_s[it] * n_vec[h], axis=-1, keepdims=True) + s_sum[it]
            hid = num / jnp.maximum(jnp.abs(den), jnp.exp(-m_s[it]))
            out_gate = jax.nn.sigmoid(op_ref[rows, h * dv:(h + 1) * dv].astype(F32))
            o_ref[rows, h * dv:(h + 1) * dv] = (out_gate * _rms_norm(hid, ng_ref[...])).astype(BF16)
    for h in range(heads):
        m_ref[h] = jnp.broadcast_to(m_run[h], (SUBLANES, LANES))


def _mlstm_mixer(zin, gates, gate_bias, norm_g, *, cps=MIXER_CHUNKS_PER_STEP):
    batch, seq, _ = zin.shape
    heads = MLSTM_HEADS
    qk_w, v_w = heads * MLSTM_DK, heads * MLSTM_DV
    steps = cps * CHUNK
    bias = gate_bias.reshape(2 * heads)
    tok = lambda w, blk: pl.BlockSpec((None, steps, w), lambda b, n: (b, n, blk))
    const = lambda *shape: pl.BlockSpec(shape, lambda b, n: (0,) * len(shape))
    return pl.pallas_call(
        functools.partial(_mlstm_kernel, cps=cps),
        grid=(batch, seq // steps),
        in_specs=[
            tok(qk_w, 0), tok(qk_w, 1), tok(v_w, 1), tok(v_w, 2),
            pl.BlockSpec((None, steps, LANES), lambda b, n: (b, n, 0)),
            pl.BlockSpec((None, cps, 2 * heads, CHUNK), lambda b, n: (b, n, 0, 0)),
            const(1, LANES), const(cps * 2 * heads, 1), const(1, MLSTM_DV),
        ],
        out_specs=pl.BlockSpec((None, steps, v_w), lambda b, n: (b, n, 0)),
        out_shape=jax.ShapeDtypeStruct((batch, seq, v_w), BF16),
        scratch_shapes=[
            pltpu.VMEM((heads, MLSTM_DK, MLSTM_DV), F32),
            pltpu.VMEM((heads, SUBLANES, MLSTM_DK), F32),
            pltpu.VMEM((heads, SUBLANES, LANES), F32),
        ],
        compiler_params=_params("parallel", "arbitrary"),
        name="mlstm",
    )(zin, zin, zin, zin, gates, _gate_rows(gates, 2 * heads), _lane_row(bias, 0),
      jnp.tile(bias, cps).reshape(cps * 2 * heads, 1), norm_g.reshape(1, MLSTM_DV))


def _ret_kernel(q_ref, k_ref, v_ref, g_ref, pos_ref, freq_ref, dmat_ref, xi_ref, zeta_ref, gam_ref, ng_ref,
                o_ref, s_ref, *, cps):
    heads, dk, dv = RET_HEADS, RET_DK, RET_DV
    half = dk // 2
    steps = cps * CHUNK

    @pl.when(pl.program_id(1) == 0)
    def _():
        s_ref[...] = jnp.zeros_like(s_ref)

    ang = pos_ref[...].reshape(steps, 1).astype(F32) * freq_ref[...]
    cos, sin = jnp.cos(ang), jnp.sin(ang)
    lane = lax.broadcasted_iota(jnp.int32, (steps, dk), 1)
    sin_signed = jnp.where(lane < half, -sin, sin)

    def rotary(x):
        return x * cos + pltpu.roll(x, half, 1) * sin_signed

    q_rot = [rotary(q_ref[:, h * dk:(h + 1) * dk].astype(F32)) * dk ** -0.5 for h in range(heads)]
    k_rot = [rotary(k_ref[:, h * dk:(h + 1) * dk].astype(F32)) for h in range(heads)]
    items = [(c, h) for c in range(cps) for h in range(heads)]
    rows = {c: slice(c * CHUNK, (c + 1) * CHUNK) for c in range(cps)}
    q_s = {(c, h): q_rot[h][rows[c]].astype(BF16) for c, h in items}
    k_s = {(c, h): k_rot[h][rows[c]] for c, h in items}
    v_s = {(c, h): v_ref[rows[c], h * dv:(h + 1) * dv].astype(BF16) for c, h in items}
    qk = {it: _mm_nt(q_s[it], k_s[it]) * dmat_ref[it[1]] for it in items}
    intra = {it: _mm(qk[it], v_s[it]) for it in items}
    k_v = {it: _mm_tn(k_s[it] * zeta_ref[it[1]], v_s[it]) for it in items}

    for c in range(cps):
        state = [s_ref[h] for h in range(heads)]
        inter = [_mm(q_s[c, h], state[h]) for h in range(heads)]
        for h in range(heads):
            s_ref[h] = state[h] * gam_ref[h] + k_v[c, h]
        for h in range(heads):
            y = intra[c, h] + inter[h] * xi_ref[h]
            mu = jnp.mean(y, axis=-1, keepdims=True)
            yc = y - mu
            var = jnp.mean(yc * yc, axis=-1, keepdims=True)
            y = yc * lax.rsqrt(var + EPS) * ng_ref[...]
            o_ref[rows[c], h * dv:(h + 1) * dv] = (y * _silu(g_ref[rows[c], h * dv:(h + 1) * dv].astype(F32))).astype(BF16)


def _ret_mixer(zin, positions, norm_g, *, col0, cps=MIXER_CHUNKS_PER_STEP):
    batch, seq, _ = zin.shape
    heads, dk = RET_HEADS, RET_DK
    qk_w, v_w = heads * RET_DK, heads * RET_DV
    assert col0 % v_w == 0
    cq, cv = col0 // qk_w, col0 // v_w
    half = dk // 2
    steps = cps * CHUNK
    inv_freq = ROPE_BASE ** (-jnp.arange(half, dtype=F32) / half)
    freq = jnp.concatenate([inv_freq, inv_freq]).reshape(1, dk)
    log_gamma = jnp.log1p(-jnp.exp2(-5.0 - jnp.arange(heads, dtype=F32)))
    idx = jnp.arange(CHUNK, dtype=F32)
    tri = jnp.tril(jnp.ones((CHUNK, CHUNK), dtype=bool))
    diff = jnp.where(tri, idx[:, None] - idx[None, :], 0.0)
    dmat = jnp.where(tri, jnp.exp(diff * log_gamma[:, None, None]), 0.0)
    xi = jnp.exp((idx + 1.0) * log_gamma[:, None])[:, :, None]
    zeta = jnp.exp((CHUNK - 1.0 - idx) * log_gamma[:, None])[:, :, None]
    gamma_c = jnp.broadcast_to(jnp.exp(CHUNK * log_gamma)[:, None, None], (heads, 1, RET_DV))
    pos = positions.reshape(batch, seq // CHUNK, CHUNK, 1)
    tok = lambda w, blk: pl.BlockSpec((None, steps, w), lambda b, n: (b, n, blk))
    const = lambda *shape: pl.BlockSpec(shape, lambda b, n: (0,) * len(shape))
    return pl.pallas_call(
        functools.partial(_ret_kernel, cps=cps),
        grid=(batch, seq // steps),
        in_specs=[
            tok(qk_w, cq), tok(qk_w, cq + 1), tok(v_w, cv + 1), tok(v_w, cv + 2),
            pl.BlockSpec((None, cps, CHUNK, 1), lambda b, n: (b, n, 0, 0)),
            const(1, dk), const(heads, CHUNK, CHUNK), const(heads, CHUNK, 1), const(heads, CHUNK, 1),
            const(heads, 1, RET_DV), const(1, RET_DV),
        ],
        out_specs=pl.BlockSpec((None, steps, v_w), lambda b, n: (b, n, 0)),
        out_shape=jax.ShapeDtypeStruct((batch, seq, v_w), BF16),
        scratch_shapes=[pltpu.VMEM((heads, RET_DK, RET_DV), F32)],
        compiler_params=_params("parallel", "arbitrary"),
        name="retention",
    )(zin, zin, zin, zin, pos, freq, dmat, xi, zeta, gamma_c, norm_g.reshape(1, RET_DV))


def _pad_rows(w):
    return jnp.pad(w, ((0, LANES - w.shape[0]), (0, 0)))


def _even_mixer(x, batch, norm_g, w_in, w_out, a_re, a_im, log_step, b_re, b_im, c_re, c_im, d_skip, w_glu,
                b_glu, conv_w, a_log, dt_bias, gdn_g):
    s5_w = a_re.shape[0] * S5_GROUP
    main_w = s5_w + GDN_HEADS * (2 * GDN_DK + 2 * GDN_DV)
    w_t = w_in.T
    w_gate = _pad_rows(w_t[main_w:]).astype(BF16)
    zin, gates = _inproj(x, norm_g, [(w_t, main_w)], w_gate)
    zin = zin.reshape(batch, -1, main_w)
    gates = gates.reshape(batch, -1, LANES)
    ya = _s5_mixer(zin, a_re, a_im, log_step, b_re, b_im, c_re, c_im, d_skip, w_glu, b_glu)
    yb = _gdn_mixer(zin, gates, conv_w, a_log, dt_bias, gdn_g, col0=s5_w)
    m = x.shape[0]
    w_out = w_out.astype(BF16)
    return _outproj(x, ya.reshape(m, -1), yb.reshape(m, -1), w_out[:s5_w], w_out[s5_w:])


def _odd_mixer(x, batch, positions, norm_g, w_in, w_out, gate_bias, mlstm_g, ret_g):
    c_main = MLSTM_HEADS * (2 * MLSTM_DK + 2 * MLSTM_DV)
    r_main = RET_HEADS * (2 * RET_DK + 2 * RET_DV)
    n_gate = 2 * MLSTM_HEADS
    w_t = w_in.T
    w_ret = w_t[c_main + n_gate:]
    w_gate = _pad_rows(w_t[c_main:c_main + n_gate]).astype(BF16)
    zin, gates = _inproj(x, norm_g, [(w_t, c_main), (w_ret, r_main)], w_gate)
    zin = zin.reshape(batch, -1, c_main + r_main)
    gates = gates.reshape(batch, -1, LANES)
    yc = _mlstm_mixer(zin, gates, gate_bias, mlstm_g)
    yd = _ret_mixer(zin, positions, ret_g, col0=c_main)
    m = x.shape[0]
    w_out = w_out.astype(BF16)
    split = MLSTM_HEADS * MLSTM_DV
    return _outproj(x, yc.reshape(m, -1), yd.reshape(m, -1), w_out[:split], w_out[split:])


def kernel(x, positions, ffn_norm, ffn_w1, ffn_w3, ffn_w2, mix_norm, even_w_in, even_w_out, s5_a_re, s5_a_im, s5_log_step, s5_b_re, s5_b_im, s5_c_re, s5_c_im, s5_d, s5_w_glu, s5_b_glu, gdn_conv_w, gdn_a_log, gdn_dt_bias, gdn_norm, odd_w_in, odd_w_out, mlstm_gate_bias, mlstm_norm, ret_norm, final_norm):
    batch, seq, d = x.shape
    depth = ffn_norm.shape[0]
    x = x.reshape(batch * seq, d)

    def ffn(x, layer, which, final_g=None):
        return _ffn(x, ffn_norm[layer, which], ffn_w1, ffn_w3, ffn_w2, layer, which, final_g)

    for layer in range(depth):
        x = ffn(x, layer, 0)
        j = layer // 2
        if layer % 2 == 0:
            x = _even_mixer(x, batch, mix_norm[layer], even_w_in[j], even_w_out[j], s5_a_re[j], s5_a_im[j],
                            s5_log_step[j], s5_b_re[j], s5_b_im[j], s5_c_re[j], s5_c_im[j], s5_d[j].reshape(-1),
                            s5_w_glu[j], s5_b_glu[j], gdn_conv_w[j], gdn_a_log[j], gdn_dt_bias[j], gdn_norm[j])
        else:
            x = _odd_mixer(x, batch, positions, mix_norm[layer], odd_w_in[j], odd_w_out[j], mlstm_gate_bias[j],
                           mlstm_norm[j], ret_norm[j])
        x = ffn(x, layer, 1, final_norm if layer == depth - 1 else None)
    return x.reshape(batch, seq, d)
```

```python
import functools
import math

import jax
import jax.numpy as jnp
from jax import lax
from jax.experimental import pallas as pl
from jax.experimental.pallas import tpu as pltpu

F32 = jnp.float32
BF16 = jnp.bfloat16

EPS = 1e-6
CHUNK = 64
ROPE_BASE = 10000.0
LANES = 128
SUBLANES = 8
VMEM_LIMIT_BYTES = 60 * 1024 * 1024

S5_GROUP = 16
S5_STATE = 64
S5_TILE_GROUPS = LANES // S5_GROUP
S5_PAIRS = S5_TILE_GROUPS // 2
S5_TIME = 256
S5_PITCH = S5_TIME + 4

GDN_HEADS = 8
GDN_DK = 128
GDN_DV = 128
GDN_CONV = 4
MIXER_CHUNKS_PER_STEP = 4
MLSTM_HEADS = 4
MLSTM_DK = 128
MLSTM_DV = 256
RET_HEADS = 4
RET_DK = 128
RET_DV = 256


def _params(*semantics):
    return pltpu.CompilerParams(dimension_semantics=semantics, vmem_limit_bytes=VMEM_LIMIT_BYTES)


def _mm(a, b):
    return jnp.dot(a.astype(BF16), b.astype(BF16), preferred_element_type=F32)


def _mm_nt(a, b):
    return lax.dot_general(a.astype(BF16), b.astype(BF16), (((1,), (1,)), ((), ())), preferred_element_type=F32)


def _mm_tn(a, b):
    return lax.dot_general(a.astype(BF16), b.astype(BF16), (((0,), (0,)), ((), ())), preferred_element_type=F32)


def _split3(x):
    x1 = x.astype(BF16)
    r1 = x - x1.astype(F32)
    x2 = r1.astype(BF16)
    x3 = (r1 - x2.astype(F32)).astype(BF16)
    return x1, x2, x3


def _cumsum_rows(tri_lower, x):
    t = tri_lower.astype(BF16)
    return sum(jnp.dot(t, p, preferred_element_type=F32) for p in _split3(x))


def _cumsum_lanes(x, tri_upper):
    t = tri_upper.astype(BF16)
    return sum(jnp.dot(p, t, preferred_element_type=F32) for p in _split3(x))


def _rms_norm(x, g):
    return x * lax.rsqrt(jnp.mean(x * x, axis=-1, keepdims=True) + EPS) * g


def _silu(x):
    return x * jax.nn.sigmoid(x)


def _softplus(x):
    return jnp.maximum(x, 0.0) + jnp.log(1.0 + jnp.exp(-jnp.abs(x)))


def _tri_masks(n):
    r = lax.broadcasted_iota(jnp.int32, (n, n), 0)
    c = lax.broadcasted_iota(jnp.int32, (n, n), 1)
    return r >= c, r > c, r <= c


def _ffn_kernel(x_ref, g_ref, w1_ref, w3_ref, w2_ref, fg_ref, o_ref, h_ref, *, final_norm):
    j = pl.program_id(1)

    @pl.when(j == 0)
    def _():
        x = x_ref[...]
        h_ref[...] = _rms_norm(x, g_ref[...]).astype(BF16)
        o_ref[...] = x

    h = h_ref[...]
    a = jnp.dot(h, w1_ref[...].astype(BF16), preferred_element_type=F32)
    b = jnp.dot(h, w3_ref[...].astype(BF16), preferred_element_type=F32)
    o_ref[...] += 0.5 * jnp.dot((_silu(a) * b).astype(BF16), w2_ref[...].astype(BF16), preferred_element_type=F32)

    if final_norm:
        @pl.when(j == pl.num_programs(1) - 1)
        def _():
            o_ref[...] = _rms_norm(o_ref[...], fg_ref[...])


def _ffn(x, g, w1, w3, w2, layer, which, final_g=None, *, tm=1024, tf=256):
    m, d = x.shape
    f = w1.shape[-1]
    final_norm = final_g is not None
    fg = final_g if final_norm else g
    return pl.pallas_call(
        functools.partial(_ffn_kernel, final_norm=final_norm),
        grid=(m // tm, f // tf),
        in_specs=[
            pl.BlockSpec((tm, d), lambda i, j: (i, 0)),
            pl.BlockSpec((1, d), lambda i, j: (0, 0)),
            pl.BlockSpec((None, None, d, tf), lambda i, j: (layer, which, 0, j)),
            pl.BlockSpec((None, None, d, tf), lambda i, j: (layer, which, 0, j)),
            pl.BlockSpec((None, None, tf, d), lambda i, j: (layer, which, j, 0)),
            pl.BlockSpec((1, d), lambda i, j: (0, 0)),
        ],
        out_specs=pl.BlockSpec((tm, d), lambda i, j: (i, 0)),
        out_shape=jax.ShapeDtypeStruct((m, d), F32),
        scratch_shapes=[pltpu.VMEM((tm, d), BF16)],
        compiler_params=_params("parallel", "arbitrary"),
        name="ffn",
    )(x, g.reshape(1, d), w1, w3, w2, fg.reshape(1, d))


def _inproj_kernel(*refs, bounds):
    x_ref, g_ref = refs[:2]
    w_refs = refs[2:2 + len(bounds)]
    wg_ref, o_ref, og_ref, h_ref = refs[2 + len(bounds):]
    j = pl.program_id(1)

    @pl.when(j == 0)
    def _():
        h = _rms_norm(x_ref[...], g_ref[...]).astype(BF16)
        h_ref[...] = h
        og_ref[...] = _mm_nt(h, wg_ref[...])

    for w_ref, (start, count) in zip(w_refs, bounds):
        @pl.when((j >= start) & (j < start + count))
        def _(w_ref=w_ref):
            o_ref[...] = _mm_nt(h_ref[...], w_ref[...]).astype(BF16)


def _inproj(x, g, segments, w_gate, *, tm=1024, tn=512):
    m, d = x.shape
    bounds, start = [], 0
    for _, row0, width in segments:
        assert row0 % SUBLANES == 0 and width % tn == 0
        bounds.append((start, width // tn))
        start += width // tn
    n = start * tn

    def w_spec(row0, first, count):
        return pl.BlockSpec((pl.Element(tn), pl.Element(d)),
                            lambda i, j: (pl.multiple_of(row0 + tn * jnp.clip(j - first, 0, count - 1), SUBLANES), 0))

    return pl.pallas_call(
        functools.partial(_inproj_kernel, bounds=tuple(bounds)),
        grid=(m // tm, n // tn),
        in_specs=[
            pl.BlockSpec((tm, d), lambda i, j: (i, 0)),
            pl.BlockSpec((1, d), lambda i, j: (0, 0)),
            *[w_spec(row0, first, count) for (_, row0, _), (first, count) in zip(segments, bounds)],
            pl.BlockSpec((LANES, d), lambda i, j: (0, 0)),
        ],
        out_specs=[
            pl.BlockSpec((tm, tn), lambda i, j: (i, j)),
            pl.BlockSpec((tm, LANES), lambda i, j: (i, 0)),
        ],
        out_shape=[jax.ShapeDtypeStruct((m, n), BF16), jax.ShapeDtypeStruct((m, LANES), F32)],
        scratch_shapes=[pltpu.VMEM((tm, d), BF16)],
        compiler_params=_params("parallel", "arbitrary"),
        name="inproj",
    )(x, g.reshape(1, d), *[w for w, _, _ in segments], w_gate)


def _outproj_kernel(x_ref, ya_ref, yb_ref, w_ref, o_ref, wbf_ref):
    @pl.when(pl.program_id(1) == 0)
    def _():
        wbf_ref[...] = w_ref[...].astype(BF16)

    ka = ya_ref.shape[1]
    o_ref[...] = (x_ref[...] + jnp.dot(ya_ref[...], wbf_ref[:ka, :], preferred_element_type=F32)
                  + jnp.dot(yb_ref[...], wbf_ref[ka:, :], preferred_element_type=F32))


def _outproj(x, ya, yb, w, *, tm=1024, tn=1024):
    m, d = x.shape
    ka, kb = ya.shape[1], yb.shape[1]
    return pl.pallas_call(
        _outproj_kernel,
        grid=(d // tn, m // tm),
        in_specs=[
            pl.BlockSpec((tm, tn), lambda j, i: (i, j)),
            pl.BlockSpec((tm, ka), lambda j, i: (i, 0)),
            pl.BlockSpec((tm, kb), lambda j, i: (i, 0)),
            pl.BlockSpec((ka + kb, tn), lambda j, i: (0, j)),
        ],
        out_specs=pl.BlockSpec((tm, tn), lambda j, i: (i, j)),
        out_shape=jax.ShapeDtypeStruct((m, d), F32),
        scratch_shapes=[pltpu.VMEM((ka + kb, tn), BF16)],
        compiler_params=_params("parallel", "arbitrary"),
        name="outproj",
    )(x, ya, yb, w)


def _s5_disc_kernel(are_ref, aim_ref, step_ref, bre_ref, bim_ref, lr_ref, li_ref, bbr_ref, bbi_ref):
    ar, ai = are_ref[...], aim_ref[...]
    step = jnp.exp(step_ref[...])
    mag = jnp.exp(ar * step)
    lr, li = mag * jnp.cos(ai * step), mag * jnp.sin(ai * step)
    den = ar * ar + ai * ai
    fr = ((lr - 1.0) * ar + li * ai) / den
    fi = (li * ar - (lr - 1.0) * ai) / den
    lr_ref[...] = lr
    li_ref[...] = li
    br, bi = bre_ref[...], bim_ref[...]
    bbr_ref[...] = fr[None] * br - fi[None] * bi
    bbi_ref[...] = fr[None] * bi + fi[None] * br


def _s5_discretise(a_re, a_im, log_step, b_re, b_im):
    g, p = a_re.shape
    h = b_re.shape[-1]
    sd = jax.ShapeDtypeStruct
    return pl.pallas_call(
        _s5_disc_kernel,
        out_shape=[sd((g, p), F32), sd((g, p), F32), sd((h, g, p), F32), sd((h, g, p), F32)],
        name="s5_discretise",
    )(a_re, a_im, log_step.reshape(g, 1), b_re.transpose(2, 0, 1), b_im.transpose(2, 0, 1))


def _s5_kernel(u_ref, wb_ref, lr_ref, li_ref, cs_ref, dsk_ref, wglu_ref, bglu_ref, o_ref,
               rre_ref, rim_ref, xr_ref, xi_ref, y_ref, *, batch, tiles):
    chains = tiles * S5_PAIRS * batch

    @pl.when(pl.program_id(0) == 0)
    def _():
        xr_ref[...] = jnp.zeros_like(xr_ref)
        xi_ref[...] = jnp.zeros_like(xi_ref)

    def chain_rows(j, pair, b):
        return pl.ds(((j * S5_PAIRS + pair) * batch + b) * S5_PITCH, S5_TIME)

    def u_tile(j):
        return jnp.concatenate([u_ref[b, :, j * LANES:(j + 1) * LANES] for b in range(batch)], axis=0)

    for j in range(tiles):
        drive = jnp.dot(u_tile(j).astype(BF16), wb_ref[j], preferred_element_type=F32)
        for b in range(batch):
            rows = slice(b * S5_TIME, (b + 1) * S5_TIME)
            for pair in range(S5_PAIRS):
                rre_ref[chain_rows(j, pair, b), :] = drive[rows, pair * LANES:(pair + 1) * LANES]
                rim_ref[chain_rows(j, pair, b), :] = drive[rows, (S5_PAIRS + pair) * LANES:(S5_PAIRS + pair + 1) * LANES]

    lr, li = lr_ref[...], li_ref[...]

    def step(t, carry):
        xr, xi = carry
        rows = pl.ds(t, chains, stride=S5_PITCH)
        nxr = lr * xr - li * xi + rre_ref[rows, :]
        nxi = lr * xi + li * xr + rim_ref[rows, :]
        rre_ref[rows, :] = nxr
        rim_ref[rows, :] = nxi
        return nxr, nxi

    xr, xi = lax.fori_loop(0, S5_TIME, step, (xr_ref[...], xi_ref[...]), unroll=4)
    xr_ref[...] = xr
    xi_ref[...] = xi

    for j in range(tiles):
        states = jnp.concatenate([
            jnp.concatenate(
                [rre_ref[chain_rows(j, pair, b), :].astype(BF16) for pair in range(S5_PAIRS)]
                + [rim_ref[chain_rows(j, pair, b), :].astype(BF16) for pair in range(S5_PAIRS)], axis=1)
            for b in range(batch)], axis=0)
        cols = slice(j * LANES, (j + 1) * LANES)
        y = jnp.dot(states, cs_ref[j], preferred_element_type=F32) + dsk_ref[:, cols] * u_tile(j).astype(F32)
        y_ref[:, cols] = jax.nn.gelu(y)

    y = y_ref[...]
    gate = jnp.dot(y.astype(BF16), wglu_ref[...], preferred_element_type=F32) + bglu_ref[...]
    out = (y * jax.nn.sigmoid(gate)).astype(BF16)
    for b in range(batch):
        o_ref[b] = out[b * S5_TIME:(b + 1) * S5_TIME]


def _s5_mixer(zin, a_re, a_im, log_step, b_re, b_im, c_re, c_im, d_skip, w_glu, b_glu):
    batch, seq, _ = zin.shape
    groups, states = a_re.shape
    width = groups * S5_GROUP
    tiles = width // LANES
    chains = tiles * S5_PAIRS * batch
    lr, li, bbr, bbi = _s5_discretise(a_re, a_im, log_step, b_re, b_im)

    eye = jnp.eye(S5_TILE_GROUPS, dtype=F32)
    bb = jnp.stack([bbr, bbi]).reshape(2, S5_GROUP, tiles, S5_TILE_GROUPS, states)
    wb = jnp.einsum("rhjgp,gk->jghrkp", bb, eye).reshape(tiles, LANES, 2 * S5_TILE_GROUPS * states).astype(BF16)
    cc = jnp.stack([c_re, -c_im]).reshape(2, tiles, S5_TILE_GROUPS, S5_GROUP, states)
    cs = jnp.einsum("rjghp,gk->jrkpgh", cc, eye).reshape(tiles, 2 * S5_TILE_GROUPS * states, LANES).astype(BF16)

    def per_chain(lam):
        t = lam.reshape(tiles * S5_PAIRS, 1, 2 * states)
        return jnp.broadcast_to(t, (tiles * S5_PAIRS, batch, 2 * states)).reshape(chains, 2 * states)

    const = lambda *shape: pl.BlockSpec(shape, lambda i: (0,) * len(shape))
    return pl.pallas_call(
        functools.partial(_s5_kernel, batch=batch, tiles=tiles),
        grid=(seq // S5_TIME,),
        in_specs=[
            pl.BlockSpec((batch, S5_TIME, width), lambda i: (0, i, 0)),
            const(tiles, LANES, 2 * S5_TILE_GROUPS * states),
            const(chains, LANES),
            const(chains, LANES),
            const(tiles, 2 * S5_TILE_GROUPS * states, LANES),
            const(1, width),
            const(width, width),
            const(1, width),
        ],
        out_specs=pl.BlockSpec((batch, S5_TIME, width), lambda i: (0, i, 0)),
        out_shape=jax.ShapeDtypeStruct((batch, seq, width), BF16),
        scratch_shapes=[
            pltpu.VMEM((chains * S5_PITCH, LANES), F32),
            pltpu.VMEM((chains * S5_PITCH, LANES), F32),
            pltpu.VMEM((chains, LANES), F32),
            pltpu.VMEM((chains, LANES), F32),
            pltpu.VMEM((batch * S5_TIME, width), F32),
        ],
        compiler_params=_params("arbitrary"),
        name="s5",
    )(zin, wb, per_chain(lr), per_chain(li), cs, d_skip.reshape(1, width), w_glu.astype(BF16),
      b_glu.reshape(1, width))


def _gdn_kernel(q_ref, k_ref, v_ref, z_ref, gc_ref, gr_ref, cw_ref, alc_ref, dtc_ref, alr_ref, dtr_ref,
                ng_ref, o_ref, s_ref, tail_ref, *, cps):
    heads, dk, dv = GDN_HEADS, GDN_DK, GDN_DV
    steps = cps * CHUNK

    @pl.when(pl.program_id(1) == 0)
    def _():
        s_ref[...] = jnp.zeros_like(s_ref)
        tail_ref[:, :2 * SUBLANES, :] = jnp.zeros((tail_ref.shape[0], 2 * SUBLANES, LANES), F32)

    tri, strict, tri_u = _tri_masks(CHUNK)
    row = lax.broadcasted_iota(jnp.int32, (steps, steps), 0)
    col = lax.broadcasted_iota(jnp.int32, (steps, steps), 1)
    tri_chunks = (row >= col) & (row // CHUNK == col // CHUNK)
    eye = (lax.broadcasted_iota(jnp.int32, (CHUNK, CHUNK), 0)
           == lax.broadcasted_iota(jnp.int32, (CHUNK, CHUNK), 1)).astype(F32)

    def time_rows(first, count):
        return pl.ds(2 * first, count, stride=2)

    conv = []
    for idx, ref in enumerate((q_ref, k_ref, v_ref)):
        slabs = []
        for h in range(ref.shape[1] // LANES):
            lanes = slice(h * LANES, (h + 1) * LANES)
            cur = ref[:, lanes].astype(F32)
            hist = tail_ref.at[idx * heads + h]
            hist[time_rows(SUBLANES, steps), :] = cur
            cw = cw_ref[:, idx * ref.shape[1] + h * LANES:idx * ref.shape[1] + (h + 1) * LANES]
            acc = cur * cw[GDN_CONV - 1:GDN_CONV]
            for s in range(1, GDN_CONV):
                acc = acc + hist[time_rows(SUBLANES - s, steps), :] * cw[GDN_CONV - 1 - s:GDN_CONV - s]
            hist[time_rows(0, SUBLANES), :] = cur[steps - SUBLANES:]
            slabs.append(_silu(acc))
        conv.append(slabs)
    q_all, k_all, v_all = conv

    gc = gc_ref[...]
    beta_c = jax.nn.sigmoid(gc)
    g_c = -jnp.exp(alc_ref[...]) * _softplus(gc + dtc_ref[...])
    gcum_c = _cumsum_rows(tri_chunks, g_c)
    gr = gr_ref[...].reshape(cps * 2 * heads, CHUNK)
    g_r = -jnp.exp(alr_ref[...]) * _softplus(gr + dtr_ref[...])
    gcum_r = _cumsum_lanes(g_r, tri_u)

    items = [(c, h) for c in range(cps) for h in range(heads)]
    tok = {(c, h): slice(c * CHUNK, (c + 1) * CHUNK) for c, h in items}
    q_raw = {it: q_all[it[1]][tok[it]] for it in items}
    k_raw = {it: k_all[it[1]][tok[it]] for it in items}
    v = {it: v_all[it[1]][tok[it]] for it in items}
    q_ss = {it: jnp.sum(q_raw[it] * q_raw[it], axis=-1, keepdims=True) for it in items}
    k_ss = {it: jnp.sum(k_raw[it] * k_raw[it], axis=-1, keepdims=True) for it in items}
    q = {it: q_raw[it] * lax.rsqrt(q_ss[it] + EPS) * dk ** -0.5 for it in items}
    k = {it: k_raw[it] * lax.rsqrt(k_ss[it] + EPS) for it in items}
    beta = {it: beta_c[tok[it], it[1]:it[1] + 1] for it in items}
    gcol = {it: gcum_c[tok[it], heads + it[1]:heads + it[1] + 1] for it in items}
    grow = {(c, h): gcum_r[c * 2 * heads + heads + h:c * 2 * heads + heads + h + 1, :] for c, h in items}
    glast = {it: gcol[it][CHUNK - 1:CHUNK, :] for it in items}
    decay = {it: jnp.where(tri, jnp.exp(gcol[it] - grow[it]), 0.0) for it in items}
    egc = {it: jnp.exp(gcol[it]) for it in items}
    kb = {it: k[it] * beta[it] for it in items}
    lower = {it: jnp.where(strict, _mm_nt(kb[it], k[it]) * decay[it], 0.0) for it in items}
    rhs = {it: jnp.concatenate([v[it] * beta[it], kb[it] * egc[it]], axis=1).astype(BF16) for it in items}
    qk = {it: jnp.where(tri, _mm_nt(q[it], k[it]) * decay[it], 0.0).astype(BF16) for it in items}
    q_dec = {it: (q[it] * egc[it]).astype(BF16) for it in items}
    k_dec = {it: (k[it] * jnp.exp(glast[it] - gcol[it])).astype(BF16) for it in items}
    carry = {it: jnp.exp(glast[it]) for it in items}

    power = {it: -lower[it] for it in items}
    inv = {it: eye + power[it] for it in items}
    for _ in range(int(math.log2(CHUNK)) - 1):
        for it in items:
            power[it] = _mm(power[it], power[it])
        for it in items:
            inv[it] = inv[it] + _mm(inv[it], power[it])
    sol = {it: _mm(inv[it], rhs[it]) for it in items}

    out = {}
    for c in range(cps):
        state = [s_ref[h] for h in range(heads)]
        v_new = [sol[c, h][:, :dv] - _mm(sol[c, h][:, dv:], state[h]) for h in range(heads)]
        for h in range(heads):
            out[c, h] = _mm(q_dec[c, h], state[h]) + _mm(qk[c, h], v_new[h])
        for h in range(heads):
            s_ref[h] = state[h] * carry[c, h] + _mm_tn(k_dec[c, h], v_new[h])

    mean_sq = {it: jnp.mean(out[it] * out[it], axis=-1, keepdims=True) for it in items}
    for c, h in items:
        normed = out[c, h] * lax.rsqrt(mean_sq[c, h] + EPS) * ng_ref[...]
        gate = _silu(z_ref[tok[c, h], h * dv:(h + 1) * dv].astype(F32))
        o_ref[tok[c, h], h * dv:(h + 1) * dv] = (normed * gate).astype(BF16)


def _gate_rows(gates, count):
    b, l, _ = gates.shape
    return gates[:, :, :count].reshape(b, l // CHUNK, CHUNK, count).transpose(0, 1, 3, 2)


def _lane_row(vec, offset):
    return jnp.zeros((1, LANES), F32).at[0, offset:offset + vec.shape[0]].set(vec)


def _gdn_mixer(zin, gates, conv_w, a_log, dt_bias, norm_g, *, col0, cps=MIXER_CHUNKS_PER_STEP):
    batch, seq, _ = zin.shape
    heads = GDN_HEADS
    qk_w, v_w = heads * GDN_DK, heads * GDN_DV
    assert GDN_DK == GDN_DV == LANES and col0 % qk_w == 0
    c0 = col0 // qk_w
    steps = cps * CHUNK
    zeros = jnp.zeros((heads,), F32)
    al_c, dt_c = _lane_row(a_log, heads), _lane_row(dt_bias, heads)
    al_r = jnp.tile(jnp.concatenate([zeros, a_log]), cps).reshape(cps * 2 * heads, 1)
    dt_r = jnp.tile(jnp.concatenate([zeros, dt_bias]), cps).reshape(cps * 2 * heads, 1)
    tok = lambda blk: pl.BlockSpec((None, steps, qk_w), lambda b, n, blk=blk: (b, n, c0 + blk))
    const = lambda *shape: pl.BlockSpec(shape, lambda b, n: (0,) * len(shape))
    return pl.pallas_call(
        functools.partial(_gdn_kernel, cps=cps),
        grid=(batch, seq // steps),
        in_specs=[
            tok(0), tok(1), tok(2), tok(3),
            pl.BlockSpec((None, steps, LANES), lambda b, n: (b, n, 0)),
            pl.BlockSpec((None, cps, 2 * heads, CHUNK), lambda b, n: (b, n, 0, 0)),
            const(GDN_CONV, 2 * qk_w + v_w),
            const(1, LANES), const(1, LANES), const(cps * 2 * heads, 1), const(cps * 2 * heads, 1),
            const(1, GDN_DV),
        ],
        out_specs=pl.BlockSpec((None, steps, v_w), lambda b, n: (b, n, 0)),
        out_shape=jax.ShapeDtypeStruct((batch, seq, v_w), BF16),
        scratch_shapes=[
            pltpu.VMEM((heads, GDN_DK, GDN_DV), F32),
            pltpu.VMEM((3 * heads, 2 * (SUBLANES + steps), LANES), F32),
        ],
        compiler_params=_params("parallel", "arbitrary"),
        name="gdn",
    )(zin, zin, zin, zin, gates, _gate_rows(gates, 2 * heads), conv_w, al_c, dt_c, al_r, dt_r,
      norm_g.reshape(1, GDN_DV))


def _mlstm_phases(q_ref, k_ref, v_ref, op_ref, gc_ref, gr_ref, bc_ref, br_ref, ng_ref, o_ref,
                  c_ref, n_ref, m_ref, *, cps):
    heads, dk, dv = MLSTM_HEADS, MLSTM_DK, MLSTM_DV
    steps = cps * CHUNK

    @pl.when(pl.program_id(1) == 0)
    def _():
        c_ref[...] = jnp.zeros_like(c_ref)
        n_ref[...] = jnp.zeros_like(n_ref)
        m_ref[...] = jnp.zeros_like(m_ref)

    tri, _, tri_u = _tri_masks(CHUNK)
    row = lax.broadcasted_iota(jnp.int32, (steps, steps), 0)
    col = lax.broadcasted_iota(jnp.int32, (steps, steps), 1)
    tri_chunks = (row >= col) & (row // CHUNK == col // CHUNK)
    pre_c = gc_ref[...] + bc_ref[...]
    bcum_c = _cumsum_rows(tri_chunks, -_softplus(-pre_c))
    pre_r = gr_ref[...].reshape(cps * 2 * heads, CHUNK) + br_ref[...]
    bcum_r = _cumsum_lanes(-_softplus(-pre_r), tri_u)
    yield

    items = [(c, h) for c in range(cps) for h in range(heads)]
    q_s = {it: q_ref[it[0] * CHUNK:(it[0] + 1) * CHUNK, it[1] * dk:(it[1] + 1) * dk].astype(F32) * dk ** -0.5
           for it in items}
    k_s = {it: k_ref[it[0] * CHUNK:(it[0] + 1) * CHUNK, it[1] * dk:(it[1] + 1) * dk].astype(F32) for it in items}
    v_s = {it: v_ref[it[0] * CHUNK:(it[0] + 1) * CHUNK, it[1] * dv:(it[1] + 1) * dv].astype(BF16) for it in items}
    qk = {it: _mm_nt(q_s[it], k_s[it]) for it in items}
    yield
    tok = {(c, h): slice(c * CHUNK, (c + 1) * CHUNK) for c, h in items}
    gl = {(c, h): c * 2 * heads + h for c, h in items}
    ig_col = {it: pre_c[tok[it], it[1]:it[1] + 1] for it in items}
    ig_row = {it: pre_r[gl[it]:gl[it] + 1, :] for it in items}
    b_col = {it: bcum_c[tok[it], heads + it[1]:heads + it[1] + 1] for it in items}
    b_row = {it: bcum_r[gl[it] + heads:gl[it] + heads + 1, :] for it in items}
    b_last = {it: b_col[it][CHUNK - 1:CHUNK, :] for it in items}
    intra_log = {it: jnp.where(tri, b_col[it] - b_row[it] + ig_row[it], -jnp.inf) for it in items}
    intra_max = {it: jnp.max(intra_log[it], axis=-1, keepdims=True) for it in items}
    upd_log = {it: b_last[it] - b_col[it] + ig_col[it] for it in items}
    upd_max = {it: jnp.max(upd_log[it], axis=0, keepdims=True) for it in items}
    gate = {it: dict(b_col=b_col[it], b_last=b_last[it], intra_log=intra_log[it], upd_log=upd_log[it],
                     intra_max=intra_max[it], upd_max=upd_max[it]) for it in items}
    yield

    m_run = [m_ref[h][0:1, 0:1] for h in range(heads)]
    for c, h in items:
        g = gate[c, h]
        m_new = jnp.maximum(g["b_last"] + m_run[h], g["upd_max"])
        g.update(m_in=m_run[h], m_out=m_new, carry=jnp.exp(g["b_last"] + m_run[h] - m_new))
        m_run[h] = m_new
    yield

    inter_log = {it: gate[it]["b_col"] + gate[it]["m_in"] for it in items}
    m_s = {it: jnp.maximum(inter_log[it], gate[it]["intra_max"]) for it in items}
    inter_w = {it: jnp.exp(inter_log[it] - m_s[it]) for it in items}
    s_mat = {it: qk[it] * jnp.exp(gate[it]["intra_log"] - m_s[it]) for it in items}
    yield
    s_v = {it: _mm(s_mat[it], v_s[it]) for it in items}
    s_sum = {it: jnp.sum(s_mat[it], axis=-1, keepdims=True) for it in items}
    kw = {it: k_s[it] * jnp.exp(gate[it]["upd_log"] - gate[it]["m_out"]) for it in items}
    kw_v = {it: _mm_tn(kw[it], v_s[it]) for it in items}
    kw_sum = {it: jnp.sum(kw[it], axis=0, keepdims=True) for it in items}

    yield
    c_run = [c_ref[h] for h in range(heads)]
    n_run = [n_ref[h][0:1, :] for h in range(heads)]
    c_in, n_in = {}, {}
    for c, h in items:
        c_in[c, h], n_in[c, h] = c_run[h], n_run[h]
        c_run[h] = gate[c, h]["carry"] * c_run[h] + kw_v[c, h]
        n_run[h] = gate[c, h]["carry"] * n_run[h] + kw_sum[c, h]
    for h in range(heads):
        c_ref[h] = c_run[h]
        n_ref[h] = jnp.broadcast_to(n_run[h], (SUBLANES, dk))
        m_ref[h] = jnp.broadcast_to(m_run[h], (SUBLANES, LANES))
    yield
    q_c = {it: _mm(q_s[it], c_in[it]) for it in items}
    yield
    den = {it: inter_w[it] * jnp.sum(q_s[it] * n_in[it], axis=-1, keepdims=True) + s_sum[it] for it in items}
    hid = {it: (inter_w[it] * q_c[it] + s_v[it]) / jnp.maximum(jnp.abs(den[it]), jnp.exp(-m_s[it])) for it in items}
    mean_sq = {it: jnp.mean(hid[it] * hid[it], axis=-1, keepdims=True) for it in items}
    for c, h in items:
        rows = slice(c * CHUNK, (c + 1) * CHUNK)
        out_gate = jax.nn.sigmoid(op_ref[rows, h * dv:(h + 1) * dv].astype(F32))
        normed = hid[c, h] * lax.rsqrt(mean_sq[c, h] + EPS) * ng_ref[...]
        o_ref[rows, h * dv:(h + 1) * dv] = (out_gate * normed).astype(BF16)


def _mlstm_operands(zin, gates, gate_bias, norm_g, cps):
    heads = MLSTM_HEADS
    qk_w, v_w = heads * MLSTM_DK, heads * MLSTM_DV
    steps = cps * CHUNK
    bias = gate_bias.reshape(2 * heads)
    tok = lambda w, blk: pl.BlockSpec((None, steps, w), lambda b, n: (b, n, blk))
    const = lambda *shape: pl.BlockSpec(shape, lambda b, n: (0,) * len(shape))
    in_specs = [
        tok(qk_w, 0), tok(qk_w, 1), tok(v_w, 1), tok(v_w, 2),
        pl.BlockSpec((None, steps, LANES), lambda b, n: (b, n, 0)),
        pl.BlockSpec((None, cps, 2 * heads, CHUNK), lambda b, n: (b, n, 0, 0)),
        const(1, LANES), const(cps * 2 * heads, 1), const(1, MLSTM_DV),
    ]
    operands = [zin, zin, zin, zin, gates, _gate_rows(gates, 2 * heads), _lane_row(bias, 0),
                jnp.tile(bias, cps).reshape(cps * 2 * heads, 1), norm_g.reshape(1, MLSTM_DV)]
    scratch = [
        pltpu.VMEM((heads, MLSTM_DK, MLSTM_DV), F32),
        pltpu.VMEM((heads, SUBLANES, MLSTM_DK), F32),
        pltpu.VMEM((heads, SUBLANES, LANES), F32),
    ]
    return in_specs, operands, scratch


def _ret_phases(q_ref, k_ref, v_ref, g_ref, pos_ref, freq_ref, dmat_ref, xi_ref, zeta_ref, gam_ref, ng_ref,
                o_ref, s_ref, *, cps):
    heads, dk, dv = RET_HEADS, RET_DK, RET_DV
    half = dk // 2
    steps = cps * CHUNK

    @pl.when(pl.program_id(1) == 0)
    def _():
        s_ref[...] = jnp.zeros_like(s_ref)

    ang = pos_ref[...].reshape(steps, 1).astype(F32) * freq_ref[...]
    cos, sin = jnp.cos(ang), jnp.sin(ang)
    lane = lax.broadcasted_iota(jnp.int32, (steps, dk), 1)
    sin_signed = jnp.where(lane < half, -sin, sin)
    yield

    def rotary(x):
        return x * cos + pltpu.roll(x, half, 1) * sin_signed

    q_rot = [rotary(q_ref[:, h * dk:(h + 1) * dk].astype(F32)) * dk ** -0.5 for h in range(heads)]
    yield
    k_rot = [rotary(k_ref[:, h * dk:(h + 1) * dk].astype(F32)) for h in range(heads)]
    yield
    items = [(c, h) for c in range(cps) for h in range(heads)]
    rows = {c: slice(c * CHUNK, (c + 1) * CHUNK) for c in range(cps)}
    q_s = {(c, h): q_rot[h][rows[c]].astype(BF16) for c, h in items}
    k_s = {(c, h): k_rot[h][rows[c]] for c, h in items}
    v_s = {(c, h): v_ref[rows[c], h * dv:(h + 1) * dv].astype(BF16) for c, h in items}
    qk = {it: _mm_nt(q_s[it], k_s[it]) * dmat_ref[it[1]] for it in items}
    yield
    intra = {it: _mm(qk[it], v_s[it]) for it in items}
    yield
    k_v = {it: _mm_tn(k_s[it] * zeta_ref[it[1]], v_s[it]) for it in items}

    yield
    s_run = [s_ref[h] for h in range(heads)]
    s_in = {}
    for c, h in items:
        s_in[c, h] = s_run[h]
        s_run[h] = s_run[h] * gam_ref[h] + k_v[c, h]
    for h in range(heads):
        s_ref[h] = s_run[h]
    yield
    inter = {it: _mm(q_s[it], s_in[it]) for it in items}
    yield
    y = {it: intra[it] + inter[it] * xi_ref[it[1]] for it in items}
    centred = {it: y[it] - jnp.mean(y[it], axis=-1, keepdims=True) for it in items}
    var = {it: jnp.mean(centred[it] * centred[it], axis=-1, keepdims=True) for it in items}
    for c, h in items:
        normed = centred[c, h] * lax.rsqrt(var[c, h] + EPS) * ng_ref[...]
        gate = _silu(g_ref[rows[c], h * dv:(h + 1) * dv].astype(F32))
        o_ref[rows[c], h * dv:(h + 1) * dv] = (normed * gate).astype(BF16)


def _ret_operands(zin, positions, norm_g, col0, cps):
    batch, seq, _ = zin.shape
    heads, dk = RET_HEADS, RET_DK
    qk_w, v_w = heads * RET_DK, heads * RET_DV
    assert col0 % v_w == 0
    cq, cv = col0 // qk_w, col0 // v_w
    half = dk // 2
    steps = cps * CHUNK
    inv_freq = ROPE_BASE ** (-jnp.arange(half, dtype=F32) / half)
    freq = jnp.concatenate([inv_freq, inv_freq]).reshape(1, dk)
    log_gamma = jnp.log1p(-jnp.exp2(-5.0 - jnp.arange(heads, dtype=F32)))
    idx = jnp.arange(CHUNK, dtype=F32)
    tri = jnp.tril(jnp.ones((CHUNK, CHUNK), dtype=bool))
    diff = jnp.where(tri, idx[:, None] - idx[None, :], 0.0)
    dmat = jnp.where(tri, jnp.exp(diff * log_gamma[:, None, None]), 0.0)
    xi = jnp.exp((idx + 1.0) * log_gamma[:, None])[:, :, None]
    zeta = jnp.exp((CHUNK - 1.0 - idx) * log_gamma[:, None])[:, :, None]
    gamma_c = jnp.broadcast_to(jnp.exp(CHUNK * log_gamma)[:, None, None], (heads, 1, RET_DV))
    pos = positions.reshape(batch, seq // CHUNK, CHUNK, 1)
    tok = lambda w, blk: pl.BlockSpec((None, steps, w), lambda b, n: (b, n, blk))
    const = lambda *shape: pl.BlockSpec(shape, lambda b, n: (0,) * len(shape))
    in_specs = [
        tok(qk_w, cq), tok(qk_w, cq + 1), tok(v_w, cv + 1), tok(v_w, cv + 2),
        pl.BlockSpec((None, cps, CHUNK, 1), lambda b, n: (b, n, 0, 0)),
        const(1, dk), const(heads, CHUNK, CHUNK), const(heads, CHUNK, 1), const(heads, CHUNK, 1),
        const(heads, 1, RET_DV), const(1, RET_DV),
    ]
    operands = [zin, zin, zin, zin, pos, freq, dmat, xi, zeta, gamma_c, norm_g.reshape(1, RET_DV)]
    return in_specs, operands, [pltpu.VMEM((heads, RET_DK, RET_DV), F32)]


def _odd_kernel(*refs, n_mlstm, n_ret, cps):
    mlstm_in, ret_in = refs[:n_mlstm], refs[n_mlstm:n_mlstm + n_ret]
    yc_ref, yd_ref, c_ref, n_ref, m_ref, s_ref = refs[n_mlstm + n_ret:]
    for stream in (_ret_phases(*ret_in, yd_ref, s_ref, cps=cps),
                   _mlstm_phases(*mlstm_in, yc_ref, c_ref, n_ref, m_ref, cps=cps)):
        for _ in stream:
            pass


def _odd_mixers(zin, gates, positions, gate_bias, mlstm_g, ret_g, *, ret_col0, cps=MIXER_CHUNKS_PER_STEP):
    batch, seq, _ = zin.shape
    steps = cps * CHUNK
    m_specs, m_ops, m_scratch = _mlstm_operands(zin, gates, gate_bias, mlstm_g, cps)
    r_specs, r_ops, r_scratch = _ret_operands(zin, positions, ret_g, ret_col0, cps)
    widths = (MLSTM_HEADS * MLSTM_DV, RET_HEADS * RET_DV)
    return pl.pallas_call(
        functools.partial(_odd_kernel, n_mlstm=len(m_ops), n_ret=len(r_ops), cps=cps),
        grid=(batch, seq // steps),
        in_specs=m_specs + r_specs,
        out_specs=[pl.BlockSpec((None, steps, w), lambda b, n: (b, n, 0)) for w in widths],
        out_shape=[jax.ShapeDtypeStruct((batch, seq, w), BF16) for w in widths],
        scratch_shapes=m_scratch + r_scratch,
        compiler_params=_params("parallel", "arbitrary"),
        name="mlstm_retention",
    )(*m_ops, *r_ops)


def _pad_rows(w):
    return jnp.pad(w, ((0, LANES - w.shape[0]), (0, 0)))


def _even_mixer(x, batch, norm_g, w_in, w_out, a_re, a_im, log_step, b_re, b_im, c_re, c_im, d_skip, w_glu,
                b_glu, conv_w, a_log, dt_bias, gdn_g):
    s5_w = a_re.shape[0] * S5_GROUP
    main_w = s5_w + GDN_HEADS * (2 * GDN_DK + 2 * GDN_DV)
    w_t = w_in.T
    w_gate = _pad_rows(w_t[main_w:]).astype(BF16)
    zin, gates = _inproj(x, norm_g, [(w_t, 0, main_w)], w_gate)
    zin = zin.reshape(batch, -1, main_w)
    gates = gates.reshape(batch, -1, LANES)
    ya = _s5_mixer(zin, a_re, a_im, log_step, b_re, b_im, c_re, c_im, d_skip, w_glu, b_glu)
    yb = _gdn_mixer(zin, gates, conv_w, a_log, dt_bias, gdn_g, col0=s5_w)
    m = x.shape[0]
    return _outproj(x, ya.reshape(m, -1), yb.reshape(m, -1), w_out)


def _odd_mixer(x, batch, positions, norm_g, w_in, w_out, gate_bias, mlstm_g, ret_g):
    c_main = MLSTM_HEADS * (2 * MLSTM_DK + 2 * MLSTM_DV)
    r_main = RET_HEADS * (2 * RET_DK + 2 * RET_DV)
    n_gate = 2 * MLSTM_HEADS
    w_t = w_in.T
    w_gate = _pad_rows(w_t[c_main:c_main + n_gate]).astype(BF16)
    zin, gates = _inproj(x, norm_g, [(w_t, 0, c_main), (w_t, c_main + n_gate, r_main)], w_gate)
    zin = zin.reshape(batch, -1, c_main + r_main)
    gates = gates.reshape(batch, -1, LANES)
    yc, yd = _odd_mixers(zin, gates, positions, gate_bias, mlstm_g, ret_g, ret_col0=c_main)
    m = x.shape[0]
    return _outproj(x, yc.reshape(m, -1), yd.reshape(m, -1), w_out)


def kernel(x, positions, ffn_norm, ffn_w1, ffn_w3, ffn_w2, mix_norm, even_w_in, even_w_out, s5_a_re, s5_a_im, s5_log_step, s5_b_re, s5_b_im, s5_c_re, s5_c_im, s5_d, s5_w_glu, s5_b_glu, gdn_conv_w, gdn_a_log, gdn_dt_bias, gdn_norm, odd_w_in, odd_w_out, mlstm_gate_bias, mlstm_norm, ret_norm, final_norm):
    batch, seq, d = x.shape
    depth = ffn_norm.shape[0]
    x = x.reshape(batch * seq, d)

    def ffn(x, layer, which, final_g=None):
        return _ffn(x, ffn_norm[layer, which], ffn_w1, ffn_w3, ffn_w2, layer, which, final_g)

    for layer in range(depth):
        x = ffn(x, layer, 0)
        j = layer // 2
        if layer % 2 == 0:
            x = _even_mixer(x, batch, mix_norm[layer], even_w_in[j], even_w_out[j], s5_a_re[j], s5_a_im[j],
                            s5_log_step[j], s5_b_re[j], s5_b_im[j], s5_c_re[j], s5_c_im[j], s5_d[j].reshape(-1),
                            s5_w_glu[j], s5_b_glu[j], gdn_conv_w[j], gdn_a_log[j], gdn_dt_bias[j], gdn_norm[j])
        else:
            x = _odd_mixer(x, batch, positions, mix_norm[layer], odd_w_in[j], odd_w_out[j], mlstm_gate_bias[j],
                           mlstm_norm[j], ret_norm[j])
        x = ffn(x, layer, 1, final_norm if layer == depth - 1 else None)
    return x.reshape(batch, seq, d)
```

```python
import functools
import math

import jax
import jax.numpy as jnp
from jax import lax
from jax.experimental import pallas as pl
from jax.experimental.pallas import tpu as pltpu

F32 = jnp.float32
BF16 = jnp.bfloat16

EPS = 1e-6
CHUNK = 64
ROPE_BASE = 10000.0
LANES = 128
SUBLANES = 8
VMEM_LIMIT_BYTES = 60 * 1024 * 1024

S5_GROUP = 16
S5_STATE = 64
S5_TILE_GROUPS = LANES // S5_GROUP
S5_PAIRS = S5_TILE_GROUPS // 2
S5_TIME = 256
S5_PITCH = S5_TIME + 4

GDN_HEADS = 8
GDN_DK = 128
GDN_DV = 128
GDN_CONV = 4
MIXER_CHUNKS_PER_STEP = 4
MLSTM_HEADS = 4
MLSTM_DK = 128
MLSTM_DV = 256
RET_HEADS = 4
RET_DK = 128
RET_DV = 256


def _params(*semantics):
    return pltpu.CompilerParams(dimension_semantics=semantics, vmem_limit_bytes=VMEM_LIMIT_BYTES)


def _mm(a, b):
    return jnp.dot(a.astype(BF16), b.astype(BF16), preferred_element_type=F32)


def _mm_nt(a, b):
    return lax.dot_general(a.astype(BF16), b.astype(BF16), (((1,), (1,)), ((), ())), preferred_element_type=F32)


def _mm_tn(a, b):
    return lax.dot_general(a.astype(BF16), b.astype(BF16), (((0,), (0,)), ((), ())), preferred_element_type=F32)


def _split3(x):
    x1 = x.astype(BF16)
    r1 = x - x1.astype(F32)
    x2 = r1.astype(BF16)
    x3 = (r1 - x2.astype(F32)).astype(BF16)
    return x1, x2, x3


def _cumsum_rows(tri_lower, x):
    t = tri_lower.astype(BF16)
    return sum(jnp.dot(t, p, preferred_element_type=F32) for p in _split3(x))


def _cumsum_lanes(x, tri_upper):
    t = tri_upper.astype(BF16)
    return sum(jnp.dot(p, t, preferred_element_type=F32) for p in _split3(x))


def _rms_norm(x, g):
    return x * lax.rsqrt(jnp.mean(x * x, axis=-1, keepdims=True) + EPS) * g


def _silu(x):
    return x * jax.nn.sigmoid(x)


def _softplus(x):
    return jnp.maximum(x, 0.0) + jnp.log(1.0 + jnp.exp(-jnp.abs(x)))


def _tri_masks(n):
    r = lax.broadcasted_iota(jnp.int32, (n, n), 0)
    c = lax.broadcasted_iota(jnp.int32, (n, n), 1)
    return r >= c, r > c, r <= c


def _ffn_kernel(x_ref, g_ref, w1_ref, w3_ref, w2_ref, fg_ref, o_ref, h_ref, *, final_norm):
    j = pl.program_id(1)

    @pl.when(j == 0)
    def _():
        x = x_ref[...]
        h_ref[...] = _rms_norm(x, g_ref[...]).astype(BF16)
        o_ref[...] = x

    h = h_ref[...]
    a = jnp.dot(h, w1_ref[...].astype(BF16), preferred_element_type=F32)
    b = jnp.dot(h, w3_ref[...].astype(BF16), preferred_element_type=F32)
    o_ref[...] += 0.5 * jnp.dot((_silu(a) * b).astype(BF16), w2_ref[...].astype(BF16), preferred_element_type=F32)

    if final_norm:
        @pl.when(j == pl.num_programs(1) - 1)
        def _():
            o_ref[...] = _rms_norm(o_ref[...], fg_ref[...])


def _ffn(x, g, w1, w3, w2, layer, which, final_g=None, *, tm=1024, tf=256):
    m, d = x.shape
    f = w1.shape[-1]
    final_norm = final_g is not None
    fg = final_g if final_norm else g
    return pl.pallas_call(
        functools.partial(_ffn_kernel, final_norm=final_norm),
        grid=(m // tm, f // tf),
        in_specs=[
            pl.BlockSpec((tm, d), lambda i, j: (i, 0)),
            pl.BlockSpec((1, d), lambda i, j: (0, 0)),
            pl.BlockSpec((None, None, d, tf), lambda i, j: (layer, which, 0, j)),
            pl.BlockSpec((None, None, d, tf), lambda i, j: (layer, which, 0, j)),
            pl.BlockSpec((None, None, tf, d), lambda i, j: (layer, which, j, 0)),
            pl.BlockSpec((1, d), lambda i, j: (0, 0)),
        ],
        out_specs=pl.BlockSpec((tm, d), lambda i, j: (i, 0)),
        out_shape=jax.ShapeDtypeStruct((m, d), F32),
        scratch_shapes=[pltpu.VMEM((tm, d), BF16)],
        compiler_params=_params("parallel", "arbitrary"),
        name="ffn",
    )(x, g.reshape(1, d), w1, w3, w2, fg.reshape(1, d))


def _inproj_kernel(x_ref, g_ref, w_ref, wg_ref, o_ref, og_ref, h_ref, wbf_ref):
    i, j = pl.program_id(0), pl.program_id(1)

    @pl.when(j == 0)
    def _():
        h = _rms_norm(x_ref[...], g_ref[...]).astype(BF16)
        h_ref[...] = h
        og_ref[...] = _mm_nt(h, wg_ref[...])

    @pl.when(i == 0)
    def _():
        wbf_ref[j] = w_ref[...].astype(BF16)

    o_ref[...] = _mm_nt(h_ref[...], wbf_ref[j]).astype(BF16)


def _inproj(x, g, w_t, segments, w_gate, *, tm=1024, tn=512):
    m, d = x.shape
    starts = []
    for row0, width in segments:
        assert row0 % SUBLANES == 0 and width % tn == 0
        starts += [row0 + tn * k for k in range(width // tn)]
    nj = len(starts)

    def block_row(j):
        row, base = 0, 0
        for row0, width in segments:
            count = width // tn
            row = row + jnp.where((j >= base) & (j < base + count), row0 + tn * (j - base), 0)
            base += count
        return row

    def w_index(i, j):
        return pl.multiple_of(block_row(jnp.where(i == 0, j, nj - 1)), SUBLANES), 0

    return pl.pallas_call(
        _inproj_kernel,
        grid=(m // tm, nj),
        in_specs=[
            pl.BlockSpec((tm, d), lambda i, j: (i, 0)),
            pl.BlockSpec((1, d), lambda i, j: (0, 0)),
            pl.BlockSpec((pl.Element(tn), pl.Element(d)), w_index),
            pl.BlockSpec((LANES, d), lambda i, j: (0, 0)),
        ],
        out_specs=[
            pl.BlockSpec((tm, tn), lambda i, j: (i, j)),
            pl.BlockSpec((tm, LANES), lambda i, j: (i, 0)),
        ],
        out_shape=[jax.ShapeDtypeStruct((m, nj * tn), BF16), jax.ShapeDtypeStruct((m, LANES), F32)],
        scratch_shapes=[pltpu.VMEM((tm, d), BF16), pltpu.VMEM((nj, tn, d), BF16)],
        compiler_params=_params("arbitrary", "arbitrary"),
        name="inproj",
    )(x, g.reshape(1, d), w_t, w_gate)


def _outproj_kernel(x_ref, ya_ref, yb_ref, w_ref, o_ref, wbf_ref):
    @pl.when(pl.program_id(1) == 0)
    def _():
        wbf_ref[...] = w_ref[...].astype(BF16)

    ka = ya_ref.shape[1]
    o_ref[...] = (x_ref[...] + jnp.dot(ya_ref[...], wbf_ref[:ka, :], preferred_element_type=F32)
                  + jnp.dot(yb_ref[...], wbf_ref[ka:, :], preferred_element_type=F32))


def _outproj(x, ya, yb, w, *, tm=1024, tn=1024):
    m, d = x.shape
    ka, kb = ya.shape[1], yb.shape[1]
    return pl.pallas_call(
        _outproj_kernel,
        grid=(d // tn, m // tm),
        in_specs=[
            pl.BlockSpec((tm, tn), lambda j, i: (i, j)),
            pl.BlockSpec((tm, ka), lambda j, i: (i, 0)),
            pl.BlockSpec((tm, kb), lambda j, i: (i, 0)),
            pl.BlockSpec((ka + kb, tn), lambda j, i: (0, j)),
        ],
        out_specs=pl.BlockSpec((tm, tn), lambda j, i: (i, j)),
        out_shape=jax.ShapeDtypeStruct((m, d), F32),
        scratch_shapes=[pltpu.VMEM((ka + kb, tn), BF16)],
        compiler_params=_params("parallel", "arbitrary"),
        name="outproj",
    )(x, ya, yb, w)


def _s5_disc_kernel(are_ref, aim_ref, step_ref, bre_ref, bim_ref, lr_ref, li_ref, bbr_ref, bbi_ref):
    ar, ai = are_ref[...], aim_ref[...]
    step = jnp.exp(step_ref[...])
    mag = jnp.exp(ar * step)
    lr, li = mag * jnp.cos(ai * step), mag * jnp.sin(ai * step)
    den = ar * ar + ai * ai
    fr = ((lr - 1.0) * ar + li * ai) / den
    fi = (li * ar - (lr - 1.0) * ai) / den
    lr_ref[...] = lr
    li_ref[...] = li
    br, bi = bre_ref[...], bim_ref[...]
    bbr_ref[...] = fr[None] * br - fi[None] * bi
    bbi_ref[...] = fr[None] * bi + fi[None] * br


def _s5_discretise(a_re, a_im, log_step, b_re, b_im):
    g, p = a_re.shape
    h = b_re.shape[-1]
    sd = jax.ShapeDtypeStruct
    return pl.pallas_call(
        _s5_disc_kernel,
        out_shape=[sd((g, p), F32), sd((g, p), F32), sd((h, g, p), F32), sd((h, g, p), F32)],
        name="s5_discretise",
    )(a_re, a_im, log_step.reshape(g, 1), b_re.transpose(2, 0, 1), b_im.transpose(2, 0, 1))


def _s5_kernel(u_ref, wb_ref, lr_ref, li_ref, cs_ref, dsk_ref, wglu_ref, bglu_ref, o_ref,
               rre_ref, rim_ref, xr_ref, xi_ref, y_ref, *, batch, tiles):
    chains = tiles * S5_PAIRS * batch

    @pl.when(pl.program_id(0) == 0)
    def _():
        xr_ref[...] = jnp.zeros_like(xr_ref)
        xi_ref[...] = jnp.zeros_like(xi_ref)

    def chain_rows(j, pair, b):
        return pl.ds(((j * S5_PAIRS + pair) * batch + b) * S5_PITCH, S5_TIME)

    def u_tile(j):
        return jnp.concatenate([u_ref[b, :, j * LANES:(j + 1) * LANES] for b in range(batch)], axis=0)

    for j in range(tiles):
        drive = jnp.dot(u_tile(j).astype(BF16), wb_ref[j], preferred_element_type=F32)
        for b in range(batch):
            rows = slice(b * S5_TIME, (b + 1) * S5_TIME)
            for pair in range(S5_PAIRS):
                rre_ref[chain_rows(j, pair, b), :] = drive[rows, pair * LANES:(pair + 1) * LANES]
                rim_ref[chain_rows(j, pair, b), :] = drive[rows, (S5_PAIRS + pair) * LANES:(S5_PAIRS + pair + 1) * LANES]

    lr, li = lr_ref[...], li_ref[...]

    def step(t, carry):
        xr, xi = carry
        rows = pl.ds(t, chains, stride=S5_PITCH)
        nxr = lr * xr - li * xi + rre_ref[rows, :]
        nxi = lr * xi + li * xr + rim_ref[rows, :]
        rre_ref[rows, :] = nxr
        rim_ref[rows, :] = nxi
        return nxr, nxi

    xr, xi = lax.fori_loop(0, S5_TIME, step, (xr_ref[...], xi_ref[...]), unroll=4)
    xr_ref[...] = xr
    xi_ref[...] = xi

    for j in range(tiles):
        states = jnp.concatenate([
            jnp.concatenate(
                [rre_ref[chain_rows(j, pair, b), :].astype(BF16) for pair in range(S5_PAIRS)]
                + [rim_ref[chain_rows(j, pair, b), :].astype(BF16) for pair in range(S5_PAIRS)], axis=1)
            for b in range(batch)], axis=0)
        cols = slice(j * LANES, (j + 1) * LANES)
        y = jnp.dot(states, cs_ref[j], preferred_element_type=F32) + dsk_ref[:, cols] * u_tile(j).astype(F32)
        y_ref[:, cols] = jax.nn.gelu(y)

    y = y_ref[...]
    gate = jnp.dot(y.astype(BF16), wglu_ref[...], preferred_element_type=F32) + bglu_ref[...]
    out = (y * jax.nn.sigmoid(gate)).astype(BF16)
    for b in range(batch):
        o_ref[b] = out[b * S5_TIME:(b + 1) * S5_TIME]


def _s5_mixer(zin, a_re, a_im, log_step, b_re, b_im, c_re, c_im, d_skip, w_glu, b_glu):
    batch, seq, _ = zin.shape
    groups, states = a_re.shape
    width = groups * S5_GROUP
    tiles = width // LANES
    chains = tiles * S5_PAIRS * batch
    lr, li, bbr, bbi = _s5_discretise(a_re, a_im, log_step, b_re, b_im)

    eye = jnp.eye(S5_TILE_GROUPS, dtype=F32)
    bb = jnp.stack([bbr, bbi]).reshape(2, S5_GROUP, tiles, S5_TILE_GROUPS, states)
    wb = jnp.einsum("rhjgp,gk->jghrkp", bb, eye).reshape(tiles, LANES, 2 * S5_TILE_GROUPS * states).astype(BF16)
    cc = jnp.stack([c_re, -c_im]).reshape(2, tiles, S5_TILE_GROUPS, S5_GROUP, states)
    cs = jnp.einsum("rjghp,gk->jrkpgh", cc, eye).reshape(tiles, 2 * S5_TILE_GROUPS * states, LANES).astype(BF16)

    def per_chain(lam):
        t = lam.reshape(tiles * S5_PAIRS, 1, 2 * states)
        return jnp.broadcast_to(t, (tiles * S5_PAIRS, batch, 2 * states)).reshape(chains, 2 * states)

    const = lambda *shape: pl.BlockSpec(shape, lambda i: (0,) * len(shape))
    return pl.pallas_call(
        functools.partial(_s5_kernel, batch=batch, tiles=tiles),
        grid=(seq // S5_TIME,),
        in_specs=[
            pl.BlockSpec((batch, S5_TIME, width), lambda i: (0, i, 0)),
            const(tiles, LANES, 2 * S5_TILE_GROUPS * states),
            const(chains, LANES),
            const(chains, LANES),
            const(tiles, 2 * S5_TILE_GROUPS * states, LANES),
            const(1, width),
            const(width, width),
            const(1, width),
        ],
        out_specs=pl.BlockSpec((batch, S5_TIME, width), lambda i: (0, i, 0)),
        out_shape=jax.ShapeDtypeStruct((batch, seq, width), BF16),
        scratch_shapes=[
            pltpu.VMEM((chains * S5_PITCH, LANES), F32),
            pltpu.VMEM((chains * S5_PITCH, LANES), F32),
            pltpu.VMEM((chains, LANES), F32),
            pltpu.VMEM((chains, LANES), F32),
            pltpu.VMEM((batch * S5_TIME, width), F32),
        ],
        compiler_params=_params("arbitrary"),
        name="s5",
    )(zin, wb, per_chain(lr), per_chain(li), cs, d_skip.reshape(1, width), w_glu.astype(BF16),
      b_glu.reshape(1, width))


def _gdn_kernel(q_ref, k_ref, v_ref, z_ref, gc_ref, gr_ref, cw_ref, alc_ref, dtc_ref, alr_ref, dtr_ref,
                ng_ref, o_ref, s_ref, tail_ref, *, cps):
    heads, dk, dv = GDN_HEADS, GDN_DK, GDN_DV
    steps = cps * CHUNK

    @pl.when(pl.program_id(1) == 0)
    def _():
        s_ref[...] = jnp.zeros_like(s_ref)
        tail_ref[:, :2 * SUBLANES, :] = jnp.zeros((tail_ref.shape[0], 2 * SUBLANES, LANES), F32)

    tri, strict, tri_u = _tri_masks(CHUNK)
    row = lax.broadcasted_iota(jnp.int32, (steps, steps), 0)
    col = lax.broadcasted_iota(jnp.int32, (steps, steps), 1)
    tri_chunks = (row >= col) & (row // CHUNK == col // CHUNK)
    eye = (lax.broadcasted_iota(jnp.int32, (CHUNK, CHUNK), 0)
           == lax.broadcasted_iota(jnp.int32, (CHUNK, CHUNK), 1)).astype(F32)

    def time_rows(first, count):
        return pl.ds(2 * first, count, stride=2)

    conv = []
    for idx, ref in enumerate((q_ref, k_ref, v_ref)):
        slabs = []
        for h in range(ref.shape[1] // LANES):
            lanes = slice(h * LANES, (h + 1) * LANES)
            cur = ref[:, lanes].astype(F32)
            hist = tail_ref.at[idx * heads + h]
            hist[time_rows(SUBLANES, steps), :] = cur
            cw = cw_ref[:, idx * ref.shape[1] + h * LANES:idx * ref.shape[1] + (h + 1) * LANES]
            acc = cur * cw[GDN_CONV - 1:GDN_CONV]
            for s in range(1, GDN_CONV):
                acc = acc + hist[time_rows(SUBLANES - s, steps), :] * cw[GDN_CONV - 1 - s:GDN_CONV - s]
            hist[time_rows(0, SUBLANES), :] = cur[steps - SUBLANES:]
            slabs.append(_silu(acc))
        conv.append(slabs)
    q_all, k_all, v_all = conv

    gc = gc_ref[...]
    beta_c = jax.nn.sigmoid(gc)
    g_c = -jnp.exp(alc_ref[...]) * _softplus(gc + dtc_ref[...])
    gcum_c = _cumsum_rows(tri_chunks, g_c)
    gr = gr_ref[...].reshape(cps * 2 * heads, CHUNK)
    g_r = -jnp.exp(alr_ref[...]) * _softplus(gr + dtr_ref[...])
    gcum_r = _cumsum_lanes(g_r, tri_u)

    items = [(c, h) for c in range(cps) for h in range(heads)]
    tok = {(c, h): slice(c * CHUNK, (c + 1) * CHUNK) for c, h in items}
    q_raw = {it: q_all[it[1]][tok[it]] for it in items}
    k_raw = {it: k_all[it[1]][tok[it]] for it in items}
    v = {it: v_all[it[1]][tok[it]] for it in items}
    q_ss = {it: jnp.sum(q_raw[it] * q_raw[it], axis=-1, keepdims=True) for it in items}
    k_ss = {it: jnp.sum(k_raw[it] * k_raw[it], axis=-1, keepdims=True) for it in items}
    q = {it: q_raw[it] * lax.rsqrt(q_ss[it] + EPS) * dk ** -0.5 for it in items}
    k = {it: k_raw[it] * lax.rsqrt(k_ss[it] + EPS) for it in items}
    beta = {it: beta_c[tok[it], it[1]:it[1] + 1] for it in items}
    gcol = {it: gcum_c[tok[it], heads + it[1]:heads + it[1] + 1] for it in items}
    grow = {(c, h): gcum_r[c * 2 * heads + heads + h:c * 2 * heads + heads + h + 1, :] for c, h in items}
    glast = {it: gcol[it][CHUNK - 1:CHUNK, :] for it in items}
    decay = {it: jnp.where(tri, jnp.exp(gcol[it] - grow[it]), 0.0) for it in items}
    egc = {it: jnp.exp(gcol[it]) for it in items}
    kb = {it: k[it] * beta[it] for it in items}
    lower = {it: jnp.where(strict, _mm_nt(kb[it], k[it]) * decay[it], 0.0) for it in items}
    rhs = {it: jnp.concatenate([v[it] * beta[it], kb[it] * egc[it]], axis=1).astype(BF16) for it in items}
    qk = {it: jnp.where(tri, _mm_nt(q[it], k[it]) * decay[it], 0.0).astype(BF16) for it in items}
    q_dec = {it: (q[it] * egc[it]).astype(BF16) for it in items}
    k_dec = {it: (k[it] * jnp.exp(glast[it] - gcol[it])).astype(BF16) for it in items}
    carry = {it: jnp.exp(glast[it]) for it in items}

    power = {it: -lower[it] for it in items}
    inv = {it: eye + power[it] for it in items}
    for _ in range(int(math.log2(CHUNK)) - 1):
        for it in items:
            power[it] = _mm(power[it], power[it])
        for it in items:
            inv[it] = inv[it] + _mm(inv[it], power[it])
    sol = {it: _mm(inv[it], rhs[it]) for it in items}

    out = {}
    for c in range(cps):
        state = [s_ref[h] for h in range(heads)]
        v_new = [sol[c, h][:, :dv] - _mm(sol[c, h][:, dv:], state[h]) for h in range(heads)]
        for h in range(heads):
            out[c, h] = _mm(q_dec[c, h], state[h]) + _mm(qk[c, h], v_new[h])
        for h in range(heads):
            s_ref[h] = state[h] * carry[c, h] + _mm_tn(k_dec[c, h], v_new[h])

    mean_sq = {it: jnp.mean(out[it] * out[it], axis=-1, keepdims=True) for it in items}
    for c, h in items:
        normed = out[c, h] * lax.rsqrt(mean_sq[c, h] + EPS) * ng_ref[...]
        gate = _silu(z_ref[tok[c, h], h * dv:(h + 1) * dv].astype(F32))
        o_ref[tok[c, h], h * dv:(h + 1) * dv] = (normed * gate).astype(BF16)


def _gate_rows(gates, count):
    b, l, _ = gates.shape
    return gates[:, :, :count].reshape(b, l // CHUNK, CHUNK, count).transpose(0, 1, 3, 2)


def _lane_row(vec, offset):
    return jnp.zeros((1, LANES), F32).at[0, offset:offset + vec.shape[0]].set(vec)


def _gdn_mixer(zin, gates, conv_w, a_log, dt_bias, norm_g, *, col0, cps=MIXER_CHUNKS_PER_STEP):
    batch, seq, _ = zin.shape
    heads = GDN_HEADS
    qk_w, v_w = heads * GDN_DK, heads * GDN_DV
    assert GDN_DK == GDN_DV == LANES and col0 % qk_w == 0
    c0 = col0 // qk_w
    steps = cps * CHUNK
    zeros = jnp.zeros((heads,), F32)
    al_c, dt_c = _lane_row(a_log, heads), _lane_row(dt_bias, heads)
    al_r = jnp.tile(jnp.concatenate([zeros, a_log]), cps).reshape(cps * 2 * heads, 1)
    dt_r = jnp.tile(jnp.concatenate([zeros, dt_bias]), cps).reshape(cps * 2 * heads, 1)
    tok = lambda blk: pl.BlockSpec((None, steps, qk_w), lambda b, n, blk=blk: (b, n, c0 + blk))
    const = lambda *shape: pl.BlockSpec(shape, lambda b, n: (0,) * len(shape))
    return pl.pallas_call(
        functools.partial(_gdn_kernel, cps=cps),
        grid=(batch, seq // steps),
        in_specs=[
            tok(0), tok(1), tok(2), tok(3),
            pl.BlockSpec((None, steps, LANES), lambda b, n: (b, n, 0)),
            pl.BlockSpec((None, cps, 2 * heads, CHUNK), lambda b, n: (b, n, 0, 0)),
            const(GDN_CONV, 2 * qk_w + v_w),
            const(1, LANES), const(1, LANES), const(cps * 2 * heads, 1), const(cps * 2 * heads, 1),
            const(1, GDN_DV),
        ],
        out_specs=pl.BlockSpec((None, steps, v_w), lambda b, n: (b, n, 0)),
        out_shape=jax.ShapeDtypeStruct((batch, seq, v_w), BF16),
        scratch_shapes=[
            pltpu.VMEM((heads, GDN_DK, GDN_DV), F32),
            pltpu.VMEM((3 * heads, 2 * (SUBLANES + steps), LANES), F32),
        ],
        compiler_params=_params("parallel", "arbitrary"),
        name="gdn",
    )(zin, zin, zin, zin, gates, _gate_rows(gates, 2 * heads), conv_w, al_c, dt_c, al_r, dt_r,
      norm_g.reshape(1, GDN_DV))


def _mlstm_phases(q_ref, k_ref, v_ref, op_ref, gc_ref, gr_ref, bc_ref, br_ref, ng_ref, o_ref,
                  c_ref, n_ref, m_ref, *, cps):
    heads, dk, dv = MLSTM_HEADS, MLSTM_DK, MLSTM_DV
    steps = cps * CHUNK

    @pl.when(pl.program_id(1) == 0)
    def _():
        c_ref[...] = jnp.zeros_like(c_ref)
        n_ref[...] = jnp.zeros_like(n_ref)
        m_ref[...] = jnp.zeros_like(m_ref)

    tri, _, tri_u = _tri_masks(CHUNK)
    row = lax.broadcasted_iota(jnp.int32, (steps, steps), 0)
    col = lax.broadcasted_iota(jnp.int32, (steps, steps), 1)
    tri_chunks = (row >= col) & (row // CHUNK == col // CHUNK)
    pre_c = gc_ref[...] + bc_ref[...]
    bcum_c = _cumsum_rows(tri_chunks, -_softplus(-pre_c))
    pre_r = gr_ref[...].reshape(cps * 2 * heads, CHUNK) + br_ref[...]
    bcum_r = _cumsum_lanes(-_softplus(-pre_r), tri_u)
    yield

    items = [(c, h) for c in range(cps) for h in range(heads)]
    q_s = {it: q_ref[it[0] * CHUNK:(it[0] + 1) * CHUNK, it[1] * dk:(it[1] + 1) * dk].astype(F32) * dk ** -0.5
           for it in items}
    k_s = {it: k_ref[it[0] * CHUNK:(it[0] + 1) * CHUNK, it[1] * dk:(it[1] + 1) * dk].astype(F32) for it in items}
    v_s = {it: v_ref[it[0] * CHUNK:(it[0] + 1) * CHUNK, it[1] * dv:(it[1] + 1) * dv].astype(BF16) for it in items}
    qk = {it: _mm_nt(q_s[it], k_s[it]) for it in items}
    yield
    tok = {(c, h): slice(c * CHUNK, (c + 1) * CHUNK) for c, h in items}
    gl = {(c, h): c * 2 * heads + h for c, h in items}
    ig_col = {it: pre_c[tok[it], it[1]:it[1] + 1] for it in items}
    ig_row = {it: pre_r[gl[it]:gl[it] + 1, :] for it in items}
    b_col = {it: bcum_c[tok[it], heads + it[1]:heads + it[1] + 1] for it in items}
    b_row = {it: bcum_r[gl[it] + heads:gl[it] + heads + 1, :] for it in items}
    b_last = {it: b_col[it][CHUNK - 1:CHUNK, :] for it in items}
    intra_log = {it: jnp.where(tri, b_col[it] - b_row[it] + ig_row[it], -jnp.inf) for it in items}
    intra_max = {it: jnp.max(intra_log[it], axis=-1, keepdims=True) for it in items}
    upd_log = {it: b_last[it] - b_col[it] + ig_col[it] for it in items}
    upd_max = {it: jnp.max(upd_log[it], axis=0, keepdims=True) for it in items}
    gate = {it: dict(b_col=b_col[it], b_last=b_last[it], intra_log=intra_log[it], upd_log=upd_log[it],
                     intra_max=intra_max[it], upd_max=upd_max[it]) for it in items}
    yield

    m_run = [m_ref[h][0:1, 0:1] for h in range(heads)]
    for c, h in items:
        g = gate[c, h]
        m_new = jnp.maximum(g["b_last"] + m_run[h], g["upd_max"])
        g.update(m_in=m_run[h], m_out=m_new, carry=jnp.exp(g["b_last"] + m_run[h] - m_new))
        m_run[h] = m_new
    yield

    inter_log = {it: gate[it]["b_col"] + gate[it]["m_in"] for it in items}
    m_s = {it: jnp.maximum(inter_log[it], gate[it]["intra_max"]) for it in items}
    inter_w = {it: jnp.exp(inter_log[it] - m_s[it]) for it in items}
    s_mat = {it: qk[it] * jnp.exp(gate[it]["intra_log"] - m_s[it]) for it in items}
    yield
    s_v = {it: _mm(s_mat[it], v_s[it]) for it in items}
    s_sum = {it: jnp.sum(s_mat[it], axis=-1, keepdims=True) for it in items}
    kw = {it: k_s[it] * jnp.exp(gate[it]["upd_log"] - gate[it]["m_out"]) for it in items}
    kw_v = {it: _mm_tn(kw[it], v_s[it]) for it in items}
    kw_sum = {it: jnp.sum(kw[it], axis=0, keepdims=True) for it in items}

    yield
    c_run = [c_ref[h] for h in range(heads)]
    n_run = [n_ref[h][0:1, :] for h in range(heads)]
    c_in, n_in = {}, {}
    for c, h in items:
        c_in[c, h], n_in[c, h] = c_run[h], n_run[h]
        c_run[h] = gate[c, h]["carry"] * c_run[h] + kw_v[c, h]
        n_run[h] = gate[c, h]["carry"] * n_run[h] + kw_sum[c, h]
    for h in range(heads):
        c_ref[h] = c_run[h]
        n_ref[h] = jnp.broadcast_to(n_run[h], (SUBLANES, dk))
        m_ref[h] = jnp.broadcast_to(m_run[h], (SUBLANES, LANES))
    yield
    q_c = {it: _mm(q_s[it], c_in[it]) for it in items}
    yield
    den = {it: inter_w[it] * jnp.sum(q_s[it] * n_in[it], axis=-1, keepdims=True) + s_sum[it] for it in items}
    hid = {it: (inter_w[it] * q_c[it] + s_v[it]) / jnp.maximum(jnp.abs(den[it]), jnp.exp(-m_s[it])) for it in items}
    mean_sq = {it: jnp.mean(hid[it] * hid[it], axis=-1, keepdims=True) for it in items}
    for c, h in items:
        rows = slice(c * CHUNK, (c + 1) * CHUNK)
        out_gate = jax.nn.sigmoid(op_ref[rows, h * dv:(h + 1) * dv].astype(F32))
        normed = hid[c, h] * lax.rsqrt(mean_sq[c, h] + EPS) * ng_ref[...]
        o_ref[rows, h * dv:(h + 1) * dv] = (out_gate * normed).astype(BF16)


def _mlstm_operands(zin, gates, gate_bias, norm_g, cps):
    heads = MLSTM_HEADS
    qk_w, v_w = heads * MLSTM_DK, heads * MLSTM_DV
    steps = cps * CHUNK
    bias = gate_bias.reshape(2 * heads)
    tok = lambda w, blk: pl.BlockSpec((None, steps, w), lambda b, n: (b, n, blk))
    const = lambda *shape: pl.BlockSpec(shape, lambda b, n: (0,) * len(shape))
    in_specs = [
        tok(qk_w, 0), tok(qk_w, 1), tok(v_w, 1), tok(v_w, 2),
        pl.BlockSpec((None, steps, LANES), lambda b, n: (b, n, 0)),
        pl.BlockSpec((None, cps, 2 * heads, CHUNK), lambda b, n: (b, n, 0, 0)),
        const(1, LANES), const(cps * 2 * heads, 1), const(1, MLSTM_DV),
    ]
    operands = [zin, zin, zin, zin, gates, _gate_rows(gates, 2 * heads), _lane_row(bias, 0),
                jnp.tile(bias, cps).reshape(cps * 2 * heads, 1), norm_g.reshape(1, MLSTM_DV)]
    scratch = [
        pltpu.VMEM((heads, MLSTM_DK, MLSTM_DV), F32),
        pltpu.VMEM((heads, SUBLANES, MLSTM_DK), F32),
        pltpu.VMEM((heads, SUBLANES, LANES), F32),
    ]
    return in_specs, operands, scratch


def _ret_phases(q_ref, k_ref, v_ref, g_ref, pos_ref, freq_ref, dmat_ref, xi_ref, zeta_ref, gam_ref, ng_ref,
                o_ref, s_ref, *, cps):
    heads, dk, dv = RET_HEADS, RET_DK, RET_DV
    half = dk // 2
    steps = cps * CHUNK

    @pl.when(pl.program_id(1) == 0)
    def _():
        s_ref[...] = jnp.zeros_like(s_ref)

    ang = pos_ref[...].reshape(steps, 1).astype(F32) * freq_ref[...]
    cos, sin = jnp.cos(ang), jnp.sin(ang)
    lane = lax.broadcasted_iota(jnp.int32, (steps, dk), 1)
    sin_signed = jnp.where(lane < half, -sin, sin)
    yield

    def rotary(x):
        return x * cos + pltpu.roll(x, half, 1) * sin_signed

    q_rot = [rotary(q_ref[:, h * dk:(h + 1) * dk].astype(F32)) * dk ** -0.5 for h in range(heads)]
    yield
    k_rot = [rotary(k_ref[:, h * dk:(h + 1) * dk].astype(F32)) for h in range(heads)]
    yield
    items = [(c, h) for c in range(cps) for h in range(heads)]
    rows = {c: slice(c * CHUNK, (c + 1) * CHUNK) for c in range(cps)}
    q_s = {(c, h): q_rot[h][rows[c]].astype(BF16) for c, h in items}
    k_s = {(c, h): k_rot[h][rows[c]] for c, h in items}
    v_s = {(c, h): v_ref[rows[c], h * dv:(h + 1) * dv].astype(BF16) for c, h in items}
    qk = {it: _mm_nt(q_s[it], k_s[it]) * dmat_ref[it[1]] for it in items}
    yield
    intra = {it: _mm(qk[it], v_s[it]) for it in items}
    yield
    k_v = {it: _mm_tn(k_s[it] * zeta_ref[it[1]], v_s[it]) for it in items}

    yield
    s_run = [s_ref[h] for h in range(heads)]
    s_in = {}
    for c, h in items:
        s_in[c, h] = s_run[h]
        s_run[h] = s_run[h] * gam_ref[h] + k_v[c, h]
    for h in range(heads):
        s_ref[h] = s_run[h]
    yield
    inter = {it: _mm(q_s[it], s_in[it]) for it in items}
    yield
    y = {it: intra[it] + inter[it] * xi_ref[it[1]] for it in items}
    centred = {it: y[it] - jnp.mean(y[it], axis=-1, keepdims=True) for it in items}
    var = {it: jnp.mean(centred[it] * centred[it], axis=-1, keepdims=True) for it in items}
    for c, h in items:
        normed = centred[c, h] * lax.rsqrt(var[c, h] + EPS) * ng_ref[...]
        gate = _silu(g_ref[rows[c], h * dv:(h + 1) * dv].astype(F32))
        o_ref[rows[c], h * dv:(h + 1) * dv] = (normed * gate).astype(BF16)


def _ret_operands(zin, positions, norm_g, col0, cps):
    batch, seq, _ = zin.shape
    heads, dk = RET_HEADS, RET_DK
    qk_w, v_w = heads * RET_DK, heads * RET_DV
    assert col0 % v_w == 0
    cq, cv = col0 // qk_w, col0 // v_w
    half = dk // 2
    steps = cps * CHUNK
    inv_freq = ROPE_BASE ** (-jnp.arange(half, dtype=F32) / half)
    freq = jnp.concatenate([inv_freq, inv_freq]).reshape(1, dk)
    log_gamma = jnp.log1p(-jnp.exp2(-5.0 - jnp.arange(heads, dtype=F32)))
    idx = jnp.arange(CHUNK, dtype=F32)
    tri = jnp.tril(jnp.ones((CHUNK, CHUNK), dtype=bool))
    diff = jnp.where(tri, idx[:, None] - idx[None, :], 0.0)
    dmat = jnp.where(tri, jnp.exp(diff * log_gamma[:, None, None]), 0.0)
    xi = jnp.exp((idx + 1.0) * log_gamma[:, None])[:, :, None]
    zeta = jnp.exp((CHUNK - 1.0 - idx) * log_gamma[:, None])[:, :, None]
    gamma_c = jnp.broadcast_to(jnp.exp(CHUNK * log_gamma)[:, None, None], (heads, 1, RET_DV))
    pos = positions.reshape(batch, seq // CHUNK, CHUNK, 1)
    tok = lambda w, blk: pl.BlockSpec((None, steps, w), lambda b, n: (b, n, blk))
    const = lambda *shape: pl.BlockSpec(shape, lambda b, n: (0,) * len(shape))
    in_specs = [
        tok(qk_w, cq), tok(qk_w, cq + 1), tok(v_w, cv + 1), tok(v_w, cv + 2),
        pl.BlockSpec((None, cps, CHUNK, 1), lambda b, n: (b, n, 0, 0)),
        const(1, dk), const(heads, CHUNK, CHUNK), const(heads, CHUNK, 1), const(heads, CHUNK, 1),
        const(heads, 1, RET_DV), const(1, RET_DV),
    ]
    operands = [zin, zin, zin, zin, pos, freq, dmat, xi, zeta, gamma_c, norm_g.reshape(1, RET_DV)]
    return in_specs, operands, [pltpu.VMEM((heads, RET_DK, RET_DV), F32)]


def _odd_kernel(*refs, n_mlstm, n_ret, cps):
    mlstm_in, ret_in = refs[:n_mlstm], refs[n_mlstm:n_mlstm + n_ret]
    yc_ref, yd_ref, c_ref, n_ref, m_ref, s_ref = refs[n_mlstm + n_ret:]
    for stream in (_ret_phases(*ret_in, yd_ref, s_ref, cps=cps),
                   _mlstm_phases(*mlstm_in, yc_ref, c_ref, n_ref, m_ref, cps=cps)):
        for _ in stream:
            pass


def _odd_mixers(zin, gates, positions, gate_bias, mlstm_g, ret_g, *, ret_col0, cps=MIXER_CHUNKS_PER_STEP):
    batch, seq, _ = zin.shape
    steps = cps * CHUNK
    m_specs, m_ops, m_scratch = _mlstm_operands(zin, gates, gate_bias, mlstm_g, cps)
    r_specs, r_ops, r_scratch = _ret_operands(zin, positions, ret_g, ret_col0, cps)
    widths = (MLSTM_HEADS * MLSTM_DV, RET_HEADS * RET_DV)
    return pl.pallas_call(
        functools.partial(_odd_kernel, n_mlstm=len(m_ops), n_ret=len(r_ops), cps=cps),
        grid=(batch, seq // steps),
        in_specs=m_specs + r_specs,
        out_specs=[pl.BlockSpec((None, steps, w), lambda b, n: (b, n, 0)) for w in widths],
        out_shape=[jax.ShapeDtypeStruct((batch, seq, w), BF16) for w in widths],
        scratch_shapes=m_scratch + r_scratch,
        compiler_params=_params("parallel", "arbitrary"),
        name="mlstm_retention",
    )(*m_ops, *r_ops)


def _pad_rows(w):
    return jnp.pad(w, ((0, LANES - w.shape[0]), (0, 0)))


def _even_mixer(x, batch, norm_g, w_in, w_out, a_re, a_im, log_step, b_re, b_im, c_re, c_im, d_skip, w_glu,
                b_glu, conv_w, a_log, dt_bias, gdn_g):
    s5_w = a_re.shape[0] * S5_GROUP
    main_w = s5_w + GDN_HEADS * (2 * GDN_DK + 2 * GDN_DV)
    w_t = w_in.T
    w_gate = _pad_rows(w_t[main_w:]).astype(BF16)
    zin, gates = _inproj(x, norm_g, w_t, [(0, main_w)], w_gate)
    zin = zin.reshape(batch, -1, main_w)
    gates = gates.reshape(batch, -1, LANES)
    ya = _s5_mixer(zin, a_re, a_im, log_step, b_re, b_im, c_re, c_im, d_skip, w_glu, b_glu)
    yb = _gdn_mixer(zin, gates, conv_w, a_log, dt_bias, gdn_g, col0=s5_w)
    m = x.shape[0]
    return _outproj(x, ya.reshape(m, -1), yb.reshape(m, -1), w_out)


def _odd_mixer(x, batch, positions, norm_g, w_in, w_out, gate_bias, mlstm_g, ret_g):
    c_main = MLSTM_HEADS * (2 * MLSTM_DK + 2 * MLSTM_DV)
    r_main = RET_HEADS * (2 * RET_DK + 2 * RET_DV)
    n_gate = 2 * MLSTM_HEADS
    w_t = w_in.T
    w_gate = _pad_rows(w_t[c_main:c_main + n_gate]).astype(BF16)
    zin, gates = _inproj(x, norm_g, w_t, [(0, c_main), (c_main + n_gate, r_main)], w_gate)
    zin = zin.reshape(batch, -1, c_main + r_main)
    gates = gates.reshape(batch, -1, LANES)
    yc, yd = _odd_mixers(zin, gates, positions, gate_bias, mlstm_g, ret_g, ret_col0=c_main)
    m = x.shape[0]
    return _outproj(x, yc.reshape(m, -1), yd.reshape(m, -1), w_out)


def kernel(x, positions, ffn_norm, ffn_w1, ffn_w3, ffn_w2, mix_norm, even_w_in, even_w_out, s5_a_re, s5_a_im, s5_log_step, s5_b_re, s5_b_im, s5_c_re, s5_c_im, s5_d, s5_w_glu, s5_b_glu, gdn_conv_w, gdn_a_log, gdn_dt_bias, gdn_norm, odd_w_in, odd_w_out, mlstm_gate_bias, mlstm_norm, ret_norm, final_norm):
    batch, seq, d = x.shape
    depth = ffn_norm.shape[0]
    x = x.reshape(batch * seq, d)

    def ffn(x, layer, which, final_g=None):
        return _ffn(x, ffn_norm[layer, which], ffn_w1, ffn_w3, ffn_w2, layer, which, final_g)

    for layer in range(depth):
        x = ffn(x, layer, 0)
        j = layer // 2
        if layer % 2 == 0:
            x = _even_mixer(x, batch, mix_norm[layer], even_w_in[j], even_w_out[j], s5_a_re[j], s5_a_im[j],
                            s5_log_step[j], s5_b_re[j], s5_b_im[j], s5_c_re[j], s5_c_im[j], s5_d[j].reshape(-1),
                            s5_w_glu[j], s5_b_glu[j], gdn_conv_w[j], gdn_a_log[j], gdn_dt_bias[j], gdn_norm[j])
        else:
            x = _odd_mixer(x, batch, positions, mix_norm[layer], odd_w_in[j], odd_w_out[j], mlstm_gate_bias[j],
                           mlstm_norm[j], ret_norm[j])
        x = ffn(x, layer, 1, final_norm if layer == depth - 1 else None)
    return x.reshape(batch, seq, d)
```

```python
import functools
import math

import jax
import jax.numpy as jnp
from jax import lax
from jax.experimental import pallas as pl
from jax.experimental.pallas import tpu as pltpu

F32 = jnp.float32
BF16 = jnp.bfloat16

EPS = 1e-6
CHUNK = 64
ROPE_BASE = 10000.0
LANES = 128
SUBLANES = 8
VMEM_LIMIT_BYTES = 60 * 1024 * 1024

S5_GROUP = 16
S5_TILE_GROUPS = LANES // S5_GROUP
S5_PAIRS = S5_TILE_GROUPS // 2
S5_TIME = 256
S5_PITCH = S5_TIME + 4

GDN_HEADS = 8
GDN_DK = 128
GDN_DV = 128
GDN_CONV = 4
MIXER_CHUNKS_PER_STEP = 4
MLSTM_HEADS = 4
MLSTM_DK = 128
MLSTM_DV = 256
RET_HEADS = 4
RET_DK = 128
RET_DV = 256


def _params(*semantics):
    return pltpu.CompilerParams(dimension_semantics=semantics, vmem_limit_bytes=VMEM_LIMIT_BYTES)


def _mm(a, b):
    return jnp.dot(a.astype(BF16), b.astype(BF16), preferred_element_type=F32)


def _mm_nt(a, b):
    return lax.dot_general(a.astype(BF16), b.astype(BF16), (((1,), (1,)), ((), ())), preferred_element_type=F32)


def _mm_tn(a, b):
    return lax.dot_general(a.astype(BF16), b.astype(BF16), (((0,), (0,)), ((), ())), preferred_element_type=F32)


def _split3(x):
    x1 = x.astype(BF16)
    r1 = x - x1.astype(F32)
    x2 = r1.astype(BF16)
    x3 = (r1 - x2.astype(F32)).astype(BF16)
    return x1, x2, x3


def _cumsum_rows(tri_lower, x):
    t = tri_lower.astype(BF16)
    return sum(jnp.dot(t, p, preferred_element_type=F32) for p in _split3(x))


def _cumsum_lanes(x, tri_upper):
    t = tri_upper.astype(BF16)
    return sum(jnp.dot(p, t, preferred_element_type=F32) for p in _split3(x))


def _rms_norm(x, g):
    return x * lax.rsqrt(jnp.mean(x * x, axis=-1, keepdims=True) + EPS) * g


def _silu(x):
    return x * jax.nn.sigmoid(x)


def _softplus(x):
    return jnp.maximum(x, 0.0) + jnp.log(1.0 + jnp.exp(-jnp.abs(x)))


def _tri_masks(n):
    r = lax.broadcasted_iota(jnp.int32, (n, n), 0)
    c = lax.broadcasted_iota(jnp.int32, (n, n), 1)
    return r >= c, r > c, r <= c


def _ffn_kernel(x_ref, g_ref, w1_ref, w3_ref, w2_ref, fg_ref, o_ref, h_ref, *, final_norm):
    j = pl.program_id(1)

    @pl.when(j == 0)
    def _():
        x = x_ref[...]
        h_ref[...] = _rms_norm(x, g_ref[...]).astype(BF16)
        o_ref[...] = x

    h = h_ref[...]
    a = jnp.dot(h, w1_ref[...].astype(BF16), preferred_element_type=F32)
    b = jnp.dot(h, w3_ref[...].astype(BF16), preferred_element_type=F32)
    o_ref[...] += 0.5 * jnp.dot((_silu(a) * b).astype(BF16), w2_ref[...].astype(BF16), preferred_element_type=F32)

    if final_norm:
        @pl.when(j == pl.num_programs(1) - 1)
        def _():
            o_ref[...] = _rms_norm(o_ref[...], fg_ref[...])


def _ffn(x, g, w1, w3, w2, layer, which, final_g=None, *, tm=1024, tf=256):
    m, d = x.shape
    f = w1.shape[-1]
    final_norm = final_g is not None
    fg = final_g if final_norm else g
    return pl.pallas_call(
        functools.partial(_ffn_kernel, final_norm=final_norm),
        grid=(m // tm, f // tf),
        in_specs=[
            pl.BlockSpec((tm, d), lambda i, j: (i, 0)),
            pl.BlockSpec((1, d), lambda i, j: (0, 0)),
            pl.BlockSpec((None, None, d, tf), lambda i, j: (layer, which, 0, j)),
            pl.BlockSpec((None, None, d, tf), lambda i, j: (layer, which, 0, j)),
            pl.BlockSpec((None, None, tf, d), lambda i, j: (layer, which, j, 0)),
            pl.BlockSpec((1, d), lambda i, j: (0, 0)),
        ],
        out_specs=pl.BlockSpec((tm, d), lambda i, j: (i, 0)),
        out_shape=jax.ShapeDtypeStruct((m, d), F32),
        scratch_shapes=[pltpu.VMEM((tm, d), BF16)],
        compiler_params=_params("parallel", "arbitrary"),
        name="ffn",
    )(x, g.reshape(1, d), w1, w3, w2, fg.reshape(1, d))


def _inproj_kernel(x_ref, g_ref, w_ref, wg_ref, o_ref, og_ref, h_ref, wbf_ref):
    i, j = pl.program_id(0), pl.program_id(1)

    @pl.when(j == 0)
    def _():
        h = _rms_norm(x_ref[...], g_ref[...]).astype(BF16)
        h_ref[...] = h
        og_ref[...] = _mm_nt(h, wg_ref[...])

    @pl.when(i == 0)
    def _():
        wbf_ref[j] = w_ref[...].astype(BF16)

    o_ref[...] = _mm_nt(h_ref[...], wbf_ref[j]).astype(BF16)


def _inproj(x, g, w_t, segments, w_gate, *, tm=1024, tn=512):
    m, d = x.shape
    starts = []
    for row0, width in segments:
        assert row0 % SUBLANES == 0 and width % tn == 0
        starts += [row0 + tn * k for k in range(width // tn)]
    nj = len(starts)

    def block_row(j):
        row, base = 0, 0
        for row0, width in segments:
            count = width // tn
            row = row + jnp.where((j >= base) & (j < base + count), row0 + tn * (j - base), 0)
            base += count
        return row

    def w_index(i, j):
        return pl.multiple_of(block_row(jnp.where(i == 0, j, nj - 1)), SUBLANES), 0

    return pl.pallas_call(
        _inproj_kernel,
        grid=(m // tm, nj),
        in_specs=[
            pl.BlockSpec((tm, d), lambda i, j: (i, 0)),
            pl.BlockSpec((1, d), lambda i, j: (0, 0)),
            pl.BlockSpec((pl.Element(tn), pl.Element(d)), w_index),
            pl.BlockSpec((LANES, d), lambda i, j: (0, 0)),
        ],
        out_specs=[
            pl.BlockSpec((tm, tn), lambda i, j: (i, j)),
            pl.BlockSpec((tm, LANES), lambda i, j: (i, 0)),
        ],
        out_shape=[jax.ShapeDtypeStruct((m, nj * tn), BF16), jax.ShapeDtypeStruct((m, LANES), F32)],
        scratch_shapes=[pltpu.VMEM((tm, d), BF16), pltpu.VMEM((nj, tn, d), BF16)],
        compiler_params=_params("arbitrary", "arbitrary"),
        name="inproj",
    )(x, g.reshape(1, d), w_t, w_gate)


def _outproj_kernel(x_ref, ya_ref, yb_ref, w_ref, o_ref, wbf_ref):
    @pl.when(pl.program_id(1) == 0)
    def _():
        wbf_ref[...] = w_ref[...].astype(BF16)

    ka = ya_ref.shape[1]
    o_ref[...] = (x_ref[...] + jnp.dot(ya_ref[...], wbf_ref[:ka, :], preferred_element_type=F32)
                  + jnp.dot(yb_ref[...], wbf_ref[ka:, :], preferred_element_type=F32))


def _outproj(x, ya, yb, w, *, tm=1024, tn=1024):
    m, d = x.shape
    ka, kb = ya.shape[1], yb.shape[1]
    return pl.pallas_call(
        _outproj_kernel,
        grid=(d // tn, m // tm),
        in_specs=[
            pl.BlockSpec((tm, tn), lambda j, i: (i, j)),
            pl.BlockSpec((tm, ka), lambda j, i: (i, 0)),
            pl.BlockSpec((tm, kb), lambda j, i: (i, 0)),
            pl.BlockSpec((ka + kb, tn), lambda j, i: (0, j)),
        ],
        out_specs=pl.BlockSpec((tm, tn), lambda j, i: (i, j)),
        out_shape=jax.ShapeDtypeStruct((m, d), F32),
        scratch_shapes=[pltpu.VMEM((ka + kb, tn), BF16)],
        compiler_params=_params("parallel", "arbitrary"),
        name="outproj",
    )(x, ya, yb, w)


def _s5_disc_kernel(are_ref, aim_ref, step_ref, bre_ref, bim_ref, lr_ref, li_ref, bbr_ref, bbi_ref):
    ar, ai = are_ref[...], aim_ref[...]
    step = jnp.exp(step_ref[...])
    mag = jnp.exp(ar * step)
    lr, li = mag * jnp.cos(ai * step), mag * jnp.sin(ai * step)
    den = ar * ar + ai * ai
    fr = ((lr - 1.0) * ar + li * ai) / den
    fi = (li * ar - (lr - 1.0) * ai) / den
    lr_ref[...] = lr
    li_ref[...] = li
    br, bi = bre_ref[...], bim_ref[...]
    bbr_ref[...] = fr[None] * br - fi[None] * bi
    bbi_ref[...] = fr[None] * bi + fi[None] * br


def _s5_discretise(a_re, a_im, log_step, b_re, b_im):
    g, p = a_re.shape
    h = b_re.shape[-1]
    sd = jax.ShapeDtypeStruct
    return pl.pallas_call(
        _s5_disc_kernel,
        out_shape=[sd((g, p), F32), sd((g, p), F32), sd((h, g, p), F32), sd((h, g, p), F32)],
        name="s5_discretise",
    )(a_re, a_im, log_step.reshape(g, 1), b_re.transpose(2, 0, 1), b_im.transpose(2, 0, 1))


def _s5_kernel(u_ref, wb_ref, lr_ref, li_ref, cs_ref, dsk_ref, wglu_ref, bglu_ref, o_ref,
               rre_ref, rim_ref, xr_ref, xi_ref, y_ref, *, batch, tiles):
    chains = tiles * S5_PAIRS * batch

    @pl.when(pl.program_id(0) == 0)
    def _():
        xr_ref[...] = jnp.zeros_like(xr_ref)
        xi_ref[...] = jnp.zeros_like(xi_ref)

    def chain_rows(j, pair, b):
        return pl.ds(((j * S5_PAIRS + pair) * batch + b) * S5_PITCH, S5_TIME)

    def u_tile(j):
        return jnp.concatenate([u_ref[b, :, j * LANES:(j + 1) * LANES] for b in range(batch)], axis=0)

    for j in range(tiles):
        drive = jnp.dot(u_tile(j).astype(BF16), wb_ref[j], preferred_element_type=F32)
        for b in range(batch):
            rows = slice(b * S5_TIME, (b + 1) * S5_TIME)
            for pair in range(S5_PAIRS):
                rre_ref[chain_rows(j, pair, b), :] = drive[rows, pair * LANES:(pair + 1) * LANES]
                rim_ref[chain_rows(j, pair, b), :] = drive[rows, (S5_PAIRS + pair) * LANES:(S5_PAIRS + pair + 1) * LANES]

    lr, li = lr_ref[...], li_ref[...]

    def step(t, carry):
        xr, xi = carry
        rows = pl.ds(t, chains, stride=S5_PITCH)
        nxr = lr * xr - li * xi + rre_ref[rows, :]
        nxi = lr * xi + li * xr + rim_ref[rows, :]
        rre_ref[rows, :] = nxr
        rim_ref[rows, :] = nxi
        return nxr, nxi

    xr, xi = lax.fori_loop(0, S5_TIME, step, (xr_ref[...], xi_ref[...]), unroll=8)
    xr_ref[...] = xr
    xi_ref[...] = xi

    for j in range(tiles):
        states = jnp.concatenate([
            jnp.concatenate(
                [rre_ref[chain_rows(j, pair, b), :].astype(BF16) for pair in range(S5_PAIRS)]
                + [rim_ref[chain_rows(j, pair, b), :].astype(BF16) for pair in range(S5_PAIRS)], axis=1)
            for b in range(batch)], axis=0)
        cols = slice(j * LANES, (j + 1) * LANES)
        y = jnp.dot(states, cs_ref[j], preferred_element_type=F32) + dsk_ref[:, cols] * u_tile(j).astype(F32)
        y_ref[:, cols] = jax.nn.gelu(y)

    y = y_ref[...]
    gate = jnp.dot(y.astype(BF16), wglu_ref[...], preferred_element_type=F32) + bglu_ref[...]
    out = (y * jax.nn.sigmoid(gate)).astype(BF16)
    for b in range(batch):
        o_ref[b] = out[b * S5_TIME:(b + 1) * S5_TIME]


def _s5_mixer(zin, a_re, a_im, log_step, b_re, b_im, c_re, c_im, d_skip, w_glu, b_glu):
    batch, seq, _ = zin.shape
    groups, states = a_re.shape
    width = groups * S5_GROUP
    tiles = width // LANES
    chains = tiles * S5_PAIRS * batch
    lr, li, bbr, bbi = _s5_discretise(a_re, a_im, log_step, b_re, b_im)

    eye = jnp.eye(S5_TILE_GROUPS, dtype=F32)
    bb = jnp.stack([bbr, bbi]).reshape(2, S5_GROUP, tiles, S5_TILE_GROUPS, states)
    wb = jnp.einsum("rhjgp,gk->jghrkp", bb, eye).reshape(tiles, LANES, 2 * S5_TILE_GROUPS * states).astype(BF16)
    cc = jnp.stack([c_re, -c_im]).reshape(2, tiles, S5_TILE_GROUPS, S5_GROUP, states)
    cs = jnp.einsum("rjghp,gk->jrkpgh", cc, eye).reshape(tiles, 2 * S5_TILE_GROUPS * states, LANES).astype(BF16)

    def per_chain(lam):
        t = lam.reshape(tiles * S5_PAIRS, 1, 2 * states)
        return jnp.broadcast_to(t, (tiles * S5_PAIRS, batch, 2 * states)).reshape(chains, 2 * states)

    const = lambda *shape: pl.BlockSpec(shape, lambda i: (0,) * len(shape))
    return pl.pallas_call(
        functools.partial(_s5_kernel, batch=batch, tiles=tiles),
        grid=(seq // S5_TIME,),
        in_specs=[
            pl.BlockSpec((batch, S5_TIME, width), lambda i: (0, i, 0)),
            const(tiles, LANES, 2 * S5_TILE_GROUPS * states),
            const(chains, LANES),
            const(chains, LANES),
            const(tiles, 2 * S5_TILE_GROUPS * states, LANES),
            const(1, width),
            const(width, width),
            const(1, width),
        ],
        out_specs=pl.BlockSpec((batch, S5_TIME, width), lambda i: (0, i, 0)),
        out_shape=jax.ShapeDtypeStruct((batch, seq, width), BF16),
        scratch_shapes=[
            pltpu.VMEM((chains * S5_PITCH, LANES), F32),
            pltpu.VMEM((chains * S5_PITCH, LANES), F32),
            pltpu.VMEM((chains, LANES), F32),
            pltpu.VMEM((chains, LANES), F32),
            pltpu.VMEM((batch * S5_TIME, width), F32),
        ],
        compiler_params=_params("arbitrary"),
        name="s5",
    )(zin, wb, per_chain(lr), per_chain(li), cs, d_skip.reshape(1, width), w_glu.astype(BF16),
      b_glu.reshape(1, width))


def _gdn_kernel(q_ref, k_ref, v_ref, z_ref, gc_ref, gr_ref, cw_ref, alc_ref, dtc_ref, alr_ref, dtr_ref,
                ng_ref, o_ref, s_ref, tail_ref, *, cps):
    heads, dk, dv = GDN_HEADS, GDN_DK, GDN_DV
    steps = cps * CHUNK

    @pl.when(pl.program_id(1) == 0)
    def _():
        s_ref[...] = jnp.zeros_like(s_ref)
        tail_ref[:, :2 * SUBLANES, :] = jnp.zeros((tail_ref.shape[0], 2 * SUBLANES, LANES), F32)

    tri, strict, tri_u = _tri_masks(CHUNK)
    row = lax.broadcasted_iota(jnp.int32, (steps, steps), 0)
    col = lax.broadcasted_iota(jnp.int32, (steps, steps), 1)
    tri_chunks = (row >= col) & (row // CHUNK == col // CHUNK)
    eye = (lax.broadcasted_iota(jnp.int32, (CHUNK, CHUNK), 0)
           == lax.broadcasted_iota(jnp.int32, (CHUNK, CHUNK), 1)).astype(F32)

    def time_rows(first, count):
        return pl.ds(2 * first, count, stride=2)

    conv = []
    for idx, ref in enumerate((q_ref, k_ref, v_ref)):
        slabs = []
        for h in range(ref.shape[1] // LANES):
            lanes = slice(h * LANES, (h + 1) * LANES)
            cur = ref[:, lanes].astype(F32)
            hist = tail_ref.at[idx * heads + h]
            hist[time_rows(SUBLANES, steps), :] = cur
            cw = cw_ref[:, idx * ref.shape[1] + h * LANES:idx * ref.shape[1] + (h + 1) * LANES]
            acc = cur * cw[GDN_CONV - 1:GDN_CONV]
            for s in range(1, GDN_CONV):
                acc = acc + hist[time_rows(SUBLANES - s, steps), :] * cw[GDN_CONV - 1 - s:GDN_CONV - s]
            hist[time_rows(0, SUBLANES), :] = cur[steps - SUBLANES:]
            slabs.append(_silu(acc))
        conv.append(slabs)
    q_all, k_all, v_all = conv

    gc = gc_ref[...]
    beta_c = jax.nn.sigmoid(gc)
    g_c = -jnp.exp(alc_ref[...]) * _softplus(gc + dtc_ref[...])
    gcum_c = _cumsum_rows(tri_chunks, g_c)
    gr = gr_ref[...].reshape(cps * 2 * heads, CHUNK)
    g_r = -jnp.exp(alr_ref[...]) * _softplus(gr + dtr_ref[...])
    gcum_r = _cumsum_lanes(g_r, tri_u)

    items = [(c, h) for c in range(cps) for h in range(heads)]
    tok = {(c, h): slice(c * CHUNK, (c + 1) * CHUNK) for c, h in items}
    q_raw = {it: q_all[it[1]][tok[it]] for it in items}
    k_raw = {it: k_all[it[1]][tok[it]] for it in items}
    v = {it: v_all[it[1]][tok[it]] for it in items}
    q_ss = {it: jnp.sum(q_raw[it] * q_raw[it], axis=-1, keepdims=True) for it in items}
    k_ss = {it: jnp.sum(k_raw[it] * k_raw[it], axis=-1, keepdims=True) for it in items}
    q = {it: q_raw[it] * lax.rsqrt(q_ss[it] + EPS) * dk ** -0.5 for it in items}
    k = {it: k_raw[it] * lax.rsqrt(k_ss[it] + EPS) for it in items}
    beta = {it: beta_c[tok[it], it[1]:it[1] + 1] for it in items}
    gcol = {it: gcum_c[tok[it], heads + it[1]:heads + it[1] + 1] for it in items}
    grow = {(c, h): gcum_r[c * 2 * heads + heads + h:c * 2 * heads + heads + h + 1, :] for c, h in items}
    glast = {it: gcol[it][CHUNK - 1:CHUNK, :] for it in items}
    decay = {it: jnp.where(tri, jnp.exp(gcol[it] - grow[it]), 0.0) for it in items}
    egc = {it: jnp.exp(gcol[it]) for it in items}
    kb = {it: k[it] * beta[it] for it in items}
    lower = {it: jnp.where(strict, _mm_nt(kb[it], k[it]) * decay[it], 0.0) for it in items}
    rhs = {it: jnp.concatenate([v[it] * beta[it], kb[it] * egc[it]], axis=1).astype(BF16) for it in items}
    qk = {it: jnp.where(tri, _mm_nt(q[it], k[it]) * decay[it], 0.0).astype(BF16) for it in items}
    q_dec = {it: (q[it] * egc[it]).astype(BF16) for it in items}
    k_dec = {it: (k[it] * jnp.exp(glast[it] - gcol[it])).astype(BF16) for it in items}
    carry = {it: jnp.exp(glast[it]) for it in items}

    power = {it: -lower[it] for it in items}
    inv = {it: eye + power[it] for it in items}
    for _ in range(int(math.log2(CHUNK)) - 1):
        for it in items:
            power[it] = _mm(power[it], power[it])
        for it in items:
            inv[it] = inv[it] + _mm(inv[it], power[it])
    sol = {it: _mm(inv[it], rhs[it]) for it in items}

    out = {}
    for c in range(cps):
        state = [s_ref[h] for h in range(heads)]
        v_new = [sol[c, h][:, :dv] - _mm(sol[c, h][:, dv:], state[h]) for h in range(heads)]
        for h in range(heads):
            out[c, h] = _mm(q_dec[c, h], state[h]) + _mm(qk[c, h], v_new[h])
        for h in range(heads):
            s_ref[h] = state[h] * carry[c, h] + _mm_tn(k_dec[c, h], v_new[h])

    mean_sq = {it: jnp.mean(out[it] * out[it], axis=-1, keepdims=True) for it in items}
    for c, h in items:
        normed = out[c, h] * lax.rsqrt(mean_sq[c, h] + EPS) * ng_ref[...]
        gate = _silu(z_ref[tok[c, h], h * dv:(h + 1) * dv].astype(F32))
        o_ref[tok[c, h], h * dv:(h + 1) * dv] = (normed * gate).astype(BF16)


def _gate_rows(gates, count):
    b, l, _ = gates.shape
    return gates[:, :, :count].reshape(b, l // CHUNK, CHUNK, count).transpose(0, 1, 3, 2)


def _lane_row(vec, offset):
    return jnp.zeros((1, LANES), F32).at[0, offset:offset + vec.shape[0]].set(vec)


def _gdn_mixer(zin, gates, conv_w, a_log, dt_bias, norm_g, *, col0, cps=MIXER_CHUNKS_PER_STEP):
    batch, seq, _ = zin.shape
    heads = GDN_HEADS
    qk_w, v_w = heads * GDN_DK, heads * GDN_DV
    assert GDN_DK == GDN_DV == LANES and col0 % qk_w == 0
    c0 = col0 // qk_w
    steps = cps * CHUNK
    zeros = jnp.zeros((heads,), F32)
    al_c, dt_c = _lane_row(a_log, heads), _lane_row(dt_bias, heads)
    al_r = jnp.tile(jnp.concatenate([zeros, a_log]), cps).reshape(cps * 2 * heads, 1)
    dt_r = jnp.tile(jnp.concatenate([zeros, dt_bias]), cps).reshape(cps * 2 * heads, 1)
    tok = lambda blk: pl.BlockSpec((None, steps, qk_w), lambda b, n, blk=blk: (b, n, c0 + blk))
    const = lambda *shape: pl.BlockSpec(shape, lambda b, n: (0,) * len(shape))
    return pl.pallas_call(
        functools.partial(_gdn_kernel, cps=cps),
        grid=(batch, seq // steps),
        in_specs=[
            tok(0), tok(1), tok(2), tok(3),
            pl.BlockSpec((None, steps, LANES), lambda b, n: (b, n, 0)),
            pl.BlockSpec((None, cps, 2 * heads, CHUNK), lambda b, n: (b, n, 0, 0)),
            const(GDN_CONV, 2 * qk_w + v_w),
            const(1, LANES), const(1, LANES), const(cps * 2 * heads, 1), const(cps * 2 * heads, 1),
            const(1, GDN_DV),
        ],
        out_specs=pl.BlockSpec((None, steps, v_w), lambda b, n: (b, n, 0)),
        out_shape=jax.ShapeDtypeStruct((batch, seq, v_w), BF16),
        scratch_shapes=[
            pltpu.VMEM((heads, GDN_DK, GDN_DV), F32),
            pltpu.VMEM((3 * heads, 2 * (SUBLANES + steps), LANES), F32),
        ],
        compiler_params=_params("parallel", "arbitrary"),
        name="gdn",
    )(zin, zin, zin, zin, gates, _gate_rows(gates, 2 * heads), conv_w, al_c, dt_c, al_r, dt_r,
      norm_g.reshape(1, GDN_DV))


def _mlstm_phases(q_ref, k_ref, v_ref, op_ref, gc_ref, gr_ref, bc_ref, br_ref, ng_ref, o_ref,
                  c_ref, n_ref, m_ref, *, cps):
    heads, dk, dv = MLSTM_HEADS, MLSTM_DK, MLSTM_DV
    steps = cps * CHUNK

    @pl.when(pl.program_id(1) == 0)
    def _():
        c_ref[...] = jnp.zeros_like(c_ref)
        n_ref[...] = jnp.zeros_like(n_ref)
        m_ref[...] = jnp.zeros_like(m_ref)

    tri, _, tri_u = _tri_masks(CHUNK)
    row = lax.broadcasted_iota(jnp.int32, (steps, steps), 0)
    col = lax.broadcasted_iota(jnp.int32, (steps, steps), 1)
    tri_chunks = (row >= col) & (row // CHUNK == col // CHUNK)
    pre_c = gc_ref[...] + bc_ref[...]
    bcum_c = _cumsum_rows(tri_chunks, -_softplus(-pre_c))
    pre_r = gr_ref[...].reshape(cps * 2 * heads, CHUNK) + br_ref[...]
    bcum_r = _cumsum_lanes(-_softplus(-pre_r), tri_u)
    yield

    items = [(c, h) for c in range(cps) for h in range(heads)]
    q_s = {it: q_ref[it[0] * CHUNK:(it[0] + 1) * CHUNK, it[1] * dk:(it[1] + 1) * dk].astype(F32) * dk ** -0.5
           for it in items}
    k_s = {it: k_ref[it[0] * CHUNK:(it[0] + 1) * CHUNK, it[1] * dk:(it[1] + 1) * dk].astype(F32) for it in items}
    v_s = {it: v_ref[it[0] * CHUNK:(it[0] + 1) * CHUNK, it[1] * dv:(it[1] + 1) * dv].astype(BF16) for it in items}
    qk = {it: _mm_nt(q_s[it], k_s[it]) for it in items}
    yield
    tok = {(c, h): slice(c * CHUNK, (c + 1) * CHUNK) for c, h in items}
    gl = {(c, h): c * 2 * heads + h for c, h in items}
    ig_col = {it: pre_c[tok[it], it[1]:it[1] + 1] for it in items}
    ig_row = {it: pre_r[gl[it]:gl[it] + 1, :] for it in items}
    b_col = {it: bcum_c[tok[it], heads + it[1]:heads + it[1] + 1] for it in items}
    b_row = {it: bcum_r[gl[it] + heads:gl[it] + heads + 1, :] for it in items}
    b_last = {it: b_col[it][CHUNK - 1:CHUNK, :] for it in items}
    intra_log = {it: jnp.where(tri, b_col[it] - b_row[it] + ig_row[it], -jnp.inf) for it in items}
    intra_max = {it: jnp.max(intra_log[it], axis=-1, keepdims=True) for it in items}
    upd_log = {it: b_last[it] - b_col[it] + ig_col[it] for it in items}
    upd_max = {it: jnp.max(upd_log[it], axis=0, keepdims=True) for it in items}
    gate = {it: dict(b_col=b_col[it], b_last=b_last[it], intra_log=intra_log[it], upd_log=upd_log[it],
                     intra_max=intra_max[it], upd_max=upd_max[it]) for it in items}
    yield

    m_run = [m_ref[h][0:1, 0:1] for h in range(heads)]
    for c, h in items:
        g = gate[c, h]
        m_new = jnp.maximum(g["b_last"] + m_run[h], g["upd_max"])
        g.update(m_in=m_run[h], m_out=m_new, carry=jnp.exp(g["b_last"] + m_run[h] - m_new))
        m_run[h] = m_new
    yield

    inter_log = {it: gate[it]["b_col"] + gate[it]["m_in"] for it in items}
    m_s = {it: jnp.maximum(inter_log[it], gate[it]["intra_max"]) for it in items}
    inter_w = {it: jnp.exp(inter_log[it] - m_s[it]) for it in items}
    s_mat = {it: qk[it] * jnp.exp(gate[it]["intra_log"] - m_s[it]) for it in items}
    yield
    s_v = {it: _mm(s_mat[it], v_s[it]) for it in items}
    s_sum = {it: jnp.sum(s_mat[it], axis=-1, keepdims=True) for it in items}
    kw = {it: k_s[it] * jnp.exp(gate[it]["upd_log"] - gate[it]["m_out"]) for it in items}
    kw_v = {it: _mm_tn(kw[it], v_s[it]) for it in items}
    kw_sum = {it: jnp.sum(kw[it], axis=0, keepdims=True) for it in items}

    yield
    c_run = [c_ref[h] for h in range(heads)]
    n_run = [n_ref[h][0:1, :] for h in range(heads)]
    c_in, n_in = {}, {}
    for c, h in items:
        c_in[c, h], n_in[c, h] = c_run[h], n_run[h]
        c_run[h] = gate[c, h]["carry"] * c_run[h] + kw_v[c, h]
        n_run[h] = gate[c, h]["carry"] * n_run[h] + kw_sum[c, h]
    for h in range(heads):
        c_ref[h] = c_run[h]
        n_ref[h] = jnp.broadcast_to(n_run[h], (SUBLANES, dk))
        m_ref[h] = jnp.broadcast_to(m_run[h], (SUBLANES, LANES))
    yield
    q_c = {it: _mm(q_s[it], c_in[it]) for it in items}
    yield
    den = {it: inter_w[it] * jnp.sum(q_s[it] * n_in[it], axis=-1, keepdims=True) + s_sum[it] for it in items}
    hid = {it: (inter_w[it] * q_c[it] + s_v[it]) / jnp.maximum(jnp.abs(den[it]), jnp.exp(-m_s[it])) for it in items}
    mean_sq = {it: jnp.mean(hid[it] * hid[it], axis=-1, keepdims=True) for it in items}
    for c, h in items:
        rows = slice(c * CHUNK, (c + 1) * CHUNK)
        out_gate = jax.nn.sigmoid(op_ref[rows, h * dv:(h + 1) * dv].astype(F32))
        normed = hid[c, h] * lax.rsqrt(mean_sq[c, h] + EPS) * ng_ref[...]
        o_ref[rows, h * dv:(h + 1) * dv] = (out_gate * normed).astype(BF16)


def _mlstm_operands(zin, gates, gate_bias, norm_g, cps):
    heads = MLSTM_HEADS
    qk_w, v_w = heads * MLSTM_DK, heads * MLSTM_DV
    steps = cps * CHUNK
    bias = gate_bias.reshape(2 * heads)
    tok = lambda w, blk: pl.BlockSpec((None, steps, w), lambda b, n: (b, n, blk))
    const = lambda *shape: pl.BlockSpec(shape, lambda b, n: (0,) * len(shape))
    in_specs = [
        tok(qk_w, 0), tok(qk_w, 1), tok(v_w, 1), tok(v_w, 2),
        pl.BlockSpec((None, steps, LANES), lambda b, n: (b, n, 0)),
        pl.BlockSpec((None, cps, 2 * heads, CHUNK), lambda b, n: (b, n, 0, 0)),
        const(1, LANES), const(cps * 2 * heads, 1), const(1, MLSTM_DV),
    ]
    operands = [zin, zin, zin, zin, gates, _gate_rows(gates, 2 * heads), _lane_row(bias, 0),
                jnp.tile(bias, cps).reshape(cps * 2 * heads, 1), norm_g.reshape(1, MLSTM_DV)]
    scratch = [
        pltpu.VMEM((heads, MLSTM_DK, MLSTM_DV), F32),
        pltpu.VMEM((heads, SUBLANES, MLSTM_DK), F32),
        pltpu.VMEM((heads, SUBLANES, LANES), F32),
    ]
    return in_specs, operands, scratch


def _ret_phases(q_ref, k_ref, v_ref, g_ref, pos_ref, freq_ref, dmat_ref, xi_ref, zeta_ref, gam_ref, ng_ref,
                o_ref, s_ref, *, cps):
    heads, dk, dv = RET_HEADS, RET_DK, RET_DV
    half = dk // 2
    steps = cps * CHUNK

    @pl.when(pl.program_id(1) == 0)
    def _():
        s_ref[...] = jnp.zeros_like(s_ref)

    ang = pos_ref[...].reshape(steps, 1).astype(F32) * freq_ref[...]
    cos, sin = jnp.cos(ang), jnp.sin(ang)
    lane = lax.broadcasted_iota(jnp.int32, (steps, dk), 1)
    sin_signed = jnp.where(lane < half, -sin, sin)
    yield

    def rotary(x):
        return x * cos + pltpu.roll(x, half, 1) * sin_signed

    q_rot = [rotary(q_ref[:, h * dk:(h + 1) * dk].astype(F32)) * dk ** -0.5 for h in range(heads)]
    yield
    k_rot = [rotary(k_ref[:, h * dk:(h + 1) * dk].astype(F32)) for h in range(heads)]
    yield
    items = [(c, h) for c in range(cps) for h in range(heads)]
    rows = {c: slice(c * CHUNK, (c + 1) * CHUNK) for c in range(cps)}
    q_s = {(c, h): q_rot[h][rows[c]].astype(BF16) for c, h in items}
    k_s = {(c, h): k_rot[h][rows[c]] for c, h in items}
    v_s = {(c, h): v_ref[rows[c], h * dv:(h + 1) * dv].astype(BF16) for c, h in items}
    qk = {it: _mm_nt(q_s[it], k_s[it]) * dmat_ref[it[1]] for it in items}
    yield
    intra = {it: _mm(qk[it], v_s[it]) for it in items}
    yield
    k_v = {it: _mm_tn(k_s[it] * zeta_ref[it[1]], v_s[it]) for it in items}

    yield
    s_run = [s_ref[h] for h in range(heads)]
    s_in = {}
    for c, h in items:
        s_in[c, h] = s_run[h]
        s_run[h] = s_run[h] * gam_ref[h] + k_v[c, h]
    for h in range(heads):
        s_ref[h] = s_run[h]
    yield
    inter = {it: _mm(q_s[it], s_in[it]) for it in items}
    yield
    y = {it: intra[it] + inter[it] * xi_ref[it[1]] for it in items}
    centred = {it: y[it] - jnp.mean(y[it], axis=-1, keepdims=True) for it in items}
    var = {it: jnp.mean(centred[it] * centred[it], axis=-1, keepdims=True) for it in items}
    for c, h in items:
        normed = centred[c, h] * lax.rsqrt(var[c, h] + EPS) * ng_ref[...]
        gate = _silu(g_ref[rows[c], h * dv:(h + 1) * dv].astype(F32))
        o_ref[rows[c], h * dv:(h + 1) * dv] = (normed * gate).astype(BF16)


def _ret_operands(zin, positions, norm_g, col0, cps):
    batch, seq, _ = zin.shape
    heads, dk = RET_HEADS, RET_DK
    qk_w, v_w = heads * RET_DK, heads * RET_DV
    assert col0 % v_w == 0
    cq, cv = col0 // qk_w, col0 // v_w
    half = dk // 2
    steps = cps * CHUNK
    inv_freq = ROPE_BASE ** (-jnp.arange(half, dtype=F32) / half)
    freq = jnp.concatenate([inv_freq, inv_freq]).reshape(1, dk)
    log_gamma = jnp.log1p(-jnp.exp2(-5.0 - jnp.arange(heads, dtype=F32)))
    idx = jnp.arange(CHUNK, dtype=F32)
    tri = jnp.tril(jnp.ones((CHUNK, CHUNK), dtype=bool))
    diff = jnp.where(tri, idx[:, None] - idx[None, :], 0.0)
    dmat = jnp.where(tri, jnp.exp(diff * log_gamma[:, None, None]), 0.0)
    xi = jnp.exp((idx + 1.0) * log_gamma[:, None])[:, :, None]
    zeta = jnp.exp((CHUNK - 1.0 - idx) * log_gamma[:, None])[:, :, None]
    gamma_c = jnp.broadcast_to(jnp.exp(CHUNK * log_gamma)[:, None, None], (heads, 1, RET_DV))
    pos = positions.reshape(batch, seq // CHUNK, CHUNK, 1)
    tok = lambda w, blk: pl.BlockSpec((None, steps, w), lambda b, n: (b, n, blk))
    const = lambda *shape: pl.BlockSpec(shape, lambda b, n: (0,) * len(shape))
    in_specs = [
        tok(qk_w, cq), tok(qk_w, cq + 1), tok(v_w, cv + 1), tok(v_w, cv + 2),
        pl.BlockSpec((None, cps, CHUNK, 1), lambda b, n: (b, n, 0, 0)),
        const(1, dk), const(heads, CHUNK, CHUNK), const(heads, CHUNK, 1), const(heads, CHUNK, 1),
        const(heads, 1, RET_DV), const(1, RET_DV),
    ]
    operands = [zin, zin, zin, zin, pos, freq, dmat, xi, zeta, gamma_c, norm_g.reshape(1, RET_DV)]
    return in_specs, operands, [pltpu.VMEM((heads, RET_DK, RET_DV), F32)]


def _odd_kernel(*refs, n_mlstm, n_ret, cps):
    mlstm_in, ret_in = refs[:n_mlstm], refs[n_mlstm:n_mlstm + n_ret]
    yc_ref, yd_ref, c_ref, n_ref, m_ref, s_ref = refs[n_mlstm + n_ret:]
    for stream in (_ret_phases(*ret_in, yd_ref, s_ref, cps=cps),
                   _mlstm_phases(*mlstm_in, yc_ref, c_ref, n_ref, m_ref, cps=cps)):
        for _ in stream:
            pass


def _odd_mixers(zin, gates, positions, gate_bias, mlstm_g, ret_g, *, ret_col0, cps=MIXER_CHUNKS_PER_STEP):
    batch, seq, _ = zin.shape
    steps = cps * CHUNK
    m_specs, m_ops, m_scratch = _mlstm_operands(zin, gates, gate_bias, mlstm_g, cps)
    r_specs, r_ops, r_scratch = _ret_operands(zin, positions, ret_g, ret_col0, cps)
    widths = (MLSTM_HEADS * MLSTM_DV, RET_HEADS * RET_DV)
    return pl.pallas_call(
        functools.partial(_odd_kernel, n_mlstm=len(m_ops), n_ret=len(r_ops), cps=cps),
        grid=(batch, seq // steps),
        in_specs=m_specs + r_specs,
        out_specs=[pl.BlockSpec((None, steps, w), lambda b, n: (b, n, 0)) for w in widths],
        out_shape=[jax.ShapeDtypeStruct((batch, seq, w), BF16) for w in widths],
        scratch_shapes=m_scratch + r_scratch,
        compiler_params=_params("parallel", "arbitrary"),
        name="mlstm_retention",
    )(*m_ops, *r_ops)


def _pad_rows(w):
    return jnp.pad(w, ((0, LANES - w.shape[0]), (0, 0)))


def _even_mixer(x, batch, norm_g, w_in, w_out, a_re, a_im, log_step, b_re, b_im, c_re, c_im, d_skip, w_glu,
                b_glu, conv_w, a_log, dt_bias, gdn_g):
    s5_w = a_re.shape[0] * S5_GROUP
    main_w = s5_w + GDN_HEADS * (2 * GDN_DK + 2 * GDN_DV)
    w_t = w_in.T
    w_gate = _pad_rows(w_t[main_w:]).astype(BF16)
    zin, gates = _inproj(x, norm_g, w_t, [(0, main_w)], w_gate)
    zin = zin.reshape(batch, -1, main_w)
    gates = gates.reshape(batch, -1, LANES)
    ya = _s5_mixer(zin, a_re, a_im, log_step, b_re, b_im, c_re, c_im, d_skip, w_glu, b_glu)
    yb = _gdn_mixer(zin, gates, conv_w, a_log, dt_bias, gdn_g, col0=s5_w)
    m = x.shape[0]
    return _outproj(x, ya.reshape(m, -1), yb.reshape(m, -1), w_out)


def _odd_mixer(x, batch, positions, norm_g, w_in, w_out, gate_bias, mlstm_g, ret_g):
    c_main = MLSTM_HEADS * (2 * MLSTM_DK + 2 * MLSTM_DV)
    r_main = RET_HEADS * (2 * RET_DK + 2 * RET_DV)
    n_gate = 2 * MLSTM_HEADS
    w_t = w_in.T
    w_gate = _pad_rows(w_t[c_main:c_main + n_gate]).astype(BF16)
    zin, gates = _inproj(x, norm_g, w_t, [(0, c_main), (c_main + n_gate, r_main)], w_gate)
    zin = zin.reshape(batch, -1, c_main + r_main)
    gates = gates.reshape(batch, -1, LANES)
    yc, yd = _odd_mixers(zin, gates, positions, gate_bias, mlstm_g, ret_g, ret_col0=c_main)
    m = x.shape[0]
    return _outproj(x, yc.reshape(m, -1), yd.reshape(m, -1), w_out)


def kernel(x, positions, ffn_norm, ffn_w1, ffn_w3, ffn_w2, mix_norm, even_w_in, even_w_out, s5_a_re, s5_a_im, s5_log_step, s5_b_re, s5_b_im, s5_c_re, s5_c_im, s5_d, s5_w_glu, s5_b_glu, gdn_conv_w, gdn_a_log, gdn_dt_bias, gdn_norm, odd_w_in, odd_w_out, mlstm_gate_bias, mlstm_norm, ret_norm, final_norm):
    batch, seq, d = x.shape
    depth = ffn_norm.shape[0]
    x = x.reshape(batch * seq, d)

    def ffn(x, layer, which, final_g=None):
        return _ffn(x, ffn_norm[layer, which], ffn_w1, ffn_w3, ffn_w2, layer, which, final_g)

    for layer in range(depth):
        x = ffn(x, layer, 0)
        j = layer // 2
        if layer % 2 == 0:
            x = _even_mixer(x, batch, mix_norm[layer], even_w_in[j], even_w_out[j], s5_a_re[j], s5_a_im[j],
                            s5_log_step[j], s5_b_re[j], s5_b_im[j], s5_c_re[j], s5_c_im[j], s5_d[j].reshape(-1),
                            s5_w_glu[j], s5_b_glu[j], gdn_conv_w[j], gdn_a_log[j], gdn_dt_bias[j], gdn_norm[j])
        else:
            x = _odd_mixer(x, batch, positions, mix_norm[layer], odd_w_in[j], odd_w_out[j], mlstm_gate_bias[j],
                           mlstm_norm[j], ret_norm[j])
        x = ffn(x, layer, 1, final_norm if layer == depth - 1 else None)
    return x.reshape(batch, seq, d)
```

```python
import functools
import math

import jax
import jax.numpy as jnp
from jax import lax
from jax.experimental import pallas as pl
from jax.experimental.pallas import tpu as pltpu

F32 = jnp.float32
BF16 = jnp.bfloat16

EPS = 1e-6
CHUNK = 64
ROPE_BASE = 10000.0
LANES = 128
SUBLANES = 8
VMEM_LIMIT_BYTES = 60 * 1024 * 1024

S5_GROUP = 16
S5_TILE_GROUPS = LANES // S5_GROUP
S5_PAIRS = S5_TILE_GROUPS // 2
S5_TIME = 256
S5_PITCH = S5_TIME + 4

GDN_HEADS = 8
GDN_DK = 128
GDN_DV = 128
GDN_CONV = 4
MIXER_CHUNKS_PER_STEP = 4
MLSTM_HEADS = 4
MLSTM_DK = 128
MLSTM_DV = 256
RET_HEADS = 4
RET_DK = 128
RET_DV = 256


def _params(*semantics):
    return pltpu.CompilerParams(dimension_semantics=semantics, vmem_limit_bytes=VMEM_LIMIT_BYTES)


def _mm(a, b):
    return jnp.dot(a.astype(BF16), b.astype(BF16), preferred_element_type=F32)


def _mm_nt(a, b):
    return lax.dot_general(a.astype(BF16), b.astype(BF16), (((1,), (1,)), ((), ())), preferred_element_type=F32)


def _mm_tn(a, b):
    return lax.dot_general(a.astype(BF16), b.astype(BF16), (((0,), (0,)), ((), ())), preferred_element_type=F32)


def _split3(x):
    x1 = x.astype(BF16)
    r1 = x - x1.astype(F32)
    x2 = r1.astype(BF16)
    x3 = (r1 - x2.astype(F32)).astype(BF16)
    return x1, x2, x3


def _cumsum_rows(tri_lower, x):
    t = tri_lower.astype(BF16)
    return sum(jnp.dot(t, p, preferred_element_type=F32) for p in _split3(x))


def _cumsum_lanes(x, tri_upper):
    t = tri_upper.astype(BF16)
    return sum(jnp.dot(p, t, preferred_element_type=F32) for p in _split3(x))


def _rms_norm(x, g):
    return x * lax.rsqrt(jnp.mean(x * x, axis=-1, keepdims=True) + EPS) * g


def _silu(x):
    return x * jax.nn.sigmoid(x)


def _softplus(x):
    return jnp.maximum(x, 0.0) + jnp.log(1.0 + jnp.exp(-jnp.abs(x)))


def _tri_masks(n):
    r = lax.broadcasted_iota(jnp.int32, (n, n), 0)
    c = lax.broadcasted_iota(jnp.int32, (n, n), 1)
    return r >= c, r > c, r <= c


def _ffn_kernel(x_ref, g_ref, w1_ref, w3_ref, w2_ref, fg_ref, o_ref, h_ref, *, final_norm):
    j = pl.program_id(1)

    @pl.when(j == 0)
    def _():
        x = x_ref[...]
        h_ref[...] = _rms_norm(x, g_ref[...]).astype(BF16)
        o_ref[...] = x

    h = h_ref[...]
    a = jnp.dot(h, w1_ref[...].astype(BF16), preferred_element_type=F32)
    b = jnp.dot(h, w3_ref[...].astype(BF16), preferred_element_type=F32)
    o_ref[...] += 0.5 * jnp.dot((_silu(a) * b).astype(BF16), w2_ref[...].astype(BF16), preferred_element_type=F32)

    if final_norm:
        @pl.when(j == pl.num_programs(1) - 1)
        def _():
            o_ref[...] = _rms_norm(o_ref[...], fg_ref[...])


def _ffn(x, g, w1, w3, w2, layer, which, final_g=None, *, tm=1024, tf=256):
    m, d = x.shape
    f = w1.shape[-1]
    final_norm = final_g is not None
    fg = final_g if final_norm else g
    return pl.pallas_call(
        functools.partial(_ffn_kernel, final_norm=final_norm),
        grid=(m // tm, f // tf),
        in_specs=[
            pl.BlockSpec((tm, d), lambda i, j: (i, 0)),
            pl.BlockSpec((1, d), lambda i, j: (0, 0)),
            pl.BlockSpec((None, None, d, tf), lambda i, j: (layer, which, 0, j)),
            pl.BlockSpec((None, None, d, tf), lambda i, j: (layer, which, 0, j)),
            pl.BlockSpec((None, None, tf, d), lambda i, j: (layer, which, j, 0)),
            pl.BlockSpec((1, d), lambda i, j: (0, 0)),
        ],
        out_specs=pl.BlockSpec((tm, d), lambda i, j: (i, 0)),
        out_shape=jax.ShapeDtypeStruct((m, d), F32),
        scratch_shapes=[pltpu.VMEM((tm, d), BF16)],
        compiler_params=_params("parallel", "arbitrary"),
        name="ffn",
    )(x, g.reshape(1, d), w1, w3, w2, fg.reshape(1, d))


def _inproj_kernel(x_ref, g_ref, w_ref, wg_ref, o_ref, og_ref, h_ref, wbf_ref):
    i, j = pl.program_id(0), pl.program_id(1)

    @pl.when(j == 0)
    def _():
        h = _rms_norm(x_ref[...], g_ref[...]).astype(BF16)
        h_ref[...] = h
        og_ref[...] = _mm_nt(h, wg_ref[...])

    @pl.when(i == 0)
    def _():
        wbf_ref[j] = w_ref[...].astype(BF16)

    o_ref[...] = _mm_nt(h_ref[...], wbf_ref[j]).astype(BF16)


def _inproj(x, g, w_t, segments, w_gate, *, tm=1024, tn=512):
    m, d = x.shape
    starts = []
    for row0, width in segments:
        assert row0 % SUBLANES == 0 and width % tn == 0
        starts += [row0 + tn * k for k in range(width // tn)]
    nj = len(starts)

    def block_row(j):
        row, base = 0, 0
        for row0, width in segments:
            count = width // tn
            row = row + jnp.where((j >= base) & (j < base + count), row0 + tn * (j - base), 0)
            base += count
        return row

    def w_index(i, j):
        return pl.multiple_of(block_row(jnp.where(i == 0, j, nj - 1)), SUBLANES), 0

    return pl.pallas_call(
        _inproj_kernel,
        grid=(m // tm, nj),
        in_specs=[
            pl.BlockSpec((tm, d), lambda i, j: (i, 0)),
            pl.BlockSpec((1, d), lambda i, j: (0, 0)),
            pl.BlockSpec((pl.Element(tn), pl.Element(d)), w_index),
            pl.BlockSpec((LANES, d), lambda i, j: (0, 0)),
        ],
        out_specs=[
            pl.BlockSpec((tm, tn), lambda i, j: (i, j)),
            pl.BlockSpec((tm, LANES), lambda i, j: (i, 0)),
        ],
        out_shape=[jax.ShapeDtypeStruct((m, nj * tn), BF16), jax.ShapeDtypeStruct((m, LANES), F32)],
        scratch_shapes=[pltpu.VMEM((tm, d), BF16), pltpu.VMEM((nj, tn, d), BF16)],
        compiler_params=_params("arbitrary", "arbitrary"),
        name="inproj",
    )(x, g.reshape(1, d), w_t, w_gate)


def _outproj_kernel(x_ref, ya_ref, yb_ref, w_ref, o_ref, wbf_ref):
    @pl.when(pl.program_id(1) == 0)
    def _():
        wbf_ref[...] = w_ref[...].astype(BF16)

    ka = ya_ref.shape[1]
    o_ref[...] = (x_ref[...] + jnp.dot(ya_ref[...], wbf_ref[:ka, :], preferred_element_type=F32)
                  + jnp.dot(yb_ref[...], wbf_ref[ka:, :], preferred_element_type=F32))


def _outproj(x, ya, yb, w, *, tm=1024, tn=1024):
    m, d = x.shape
    ka, kb = ya.shape[1], yb.shape[1]
    return pl.pallas_call(
        _outproj_kernel,
        grid=(d // tn, m // tm),
        in_specs=[
            pl.BlockSpec((tm, tn), lambda j, i: (i, j)),
            pl.BlockSpec((tm, ka), lambda j, i: (i, 0)),
            pl.BlockSpec((tm, kb), lambda j, i: (i, 0)),
            pl.BlockSpec((ka + kb, tn), lambda j, i: (0, j)),
        ],
        out_specs=pl.BlockSpec((tm, tn), lambda j, i: (i, j)),
        out_shape=jax.ShapeDtypeStruct((m, d), F32),
        scratch_shapes=[pltpu.VMEM((ka + kb, tn), BF16)],
        compiler_params=_params("parallel", "arbitrary"),
        name="outproj",
    )(x, ya, yb, w)


def _s5_disc_kernel(are_ref, aim_ref, step_ref, bre_ref, bim_ref, lr_ref, li_ref, bbr_ref, bbi_ref):
    ar, ai = are_ref[...], aim_ref[...]
    step = jnp.exp(step_ref[...])
    mag = jnp.exp(ar * step)
    lr, li = mag * jnp.cos(ai * step), mag * jnp.sin(ai * step)
    den = ar * ar + ai * ai
    fr = ((lr - 1.0) * ar + li * ai) / den
    fi = (li * ar - (lr - 1.0) * ai) / den
    lr_ref[...] = lr
    li_ref[...] = li
    br, bi = bre_ref[...], bim_ref[...]
    bbr_ref[...] = fr[None] * br - fi[None] * bi
    bbi_ref[...] = fr[None] * bi + fi[None] * br


def _s5_discretise(a_re, a_im, log_step, b_re, b_im):
    g, p = a_re.shape
    h = b_re.shape[-1]
    sd = jax.ShapeDtypeStruct
    return pl.pallas_call(
        _s5_disc_kernel,
        out_shape=[sd((g, p), F32), sd((g, p), F32), sd((h, g, p), F32), sd((h, g, p), F32)],
        name="s5_discretise",
    )(a_re, a_im, log_step.reshape(g, 1), b_re.transpose(2, 0, 1), b_im.transpose(2, 0, 1))


def _s5_kernel(u_ref, wb_ref, lr_ref, li_ref, cs_ref, dsk_ref, wglu_ref, bglu_ref, o_ref,
               rre_ref, rim_ref, xr_ref, xi_ref, y_ref, *, batch, tiles):
    chains = tiles * S5_PAIRS * batch

    @pl.when(pl.program_id(0) == 0)
    def _():
        xr_ref[...] = jnp.zeros_like(xr_ref)
        xi_ref[...] = jnp.zeros_like(xi_ref)

    def chain_rows(j, pair, b):
        return pl.ds(((j * S5_PAIRS + pair) * batch + b) * S5_PITCH, S5_TIME)

    def u_tile(j):
        return jnp.concatenate([u_ref[b, :, j * LANES:(j + 1) * LANES] for b in range(batch)], axis=0)

    for j in range(tiles):
        drive = jnp.dot(u_tile(j).astype(BF16), wb_ref[j], preferred_element_type=F32)
        for b in range(batch):
            rows = slice(b * S5_TIME, (b + 1) * S5_TIME)
            for pair in range(S5_PAIRS):
                rre_ref[chain_rows(j, pair, b), :] = drive[rows, pair * LANES:(pair + 1) * LANES]
                rim_ref[chain_rows(j, pair, b), :] = drive[rows, (S5_PAIRS + pair) * LANES:(S5_PAIRS + pair + 1) * LANES]

    lr, li = lr_ref[...], li_ref[...]

    def step(t, carry):
        xr, xi = carry
        rows = pl.ds(t, chains, stride=S5_PITCH)
        nxr = lr * xr - li * xi + rre_ref[rows, :]
        nxi = lr * xi + li * xr + rim_ref[rows, :]
        rre_ref[rows, :] = nxr
        rim_ref[rows, :] = nxi
        return nxr, nxi

    xr, xi = lax.fori_loop(0, S5_TIME, step, (xr_ref[...], xi_ref[...]), unroll=8)
    xr_ref[...] = xr
    xi_ref[...] = xi

    for j in range(tiles):
        states = jnp.concatenate([
            jnp.concatenate(
                [rre_ref[chain_rows(j, pair, b), :].astype(BF16) for pair in range(S5_PAIRS)]
                + [rim_ref[chain_rows(j, pair, b), :].astype(BF16) for pair in range(S5_PAIRS)], axis=1)
            for b in range(batch)], axis=0)
        cols = slice(j * LANES, (j + 1) * LANES)
        y = jnp.dot(states, cs_ref[j], preferred_element_type=F32) + dsk_ref[:, cols] * u_tile(j).astype(F32)
        y_ref[:, cols] = jax.nn.gelu(y)

    y = y_ref[...]
    gate = jnp.dot(y.astype(BF16), wglu_ref[...], preferred_element_type=F32) + bglu_ref[...]
    out = (y * jax.nn.sigmoid(gate)).astype(BF16)
    for b in range(batch):
        o_ref[b] = out[b * S5_TIME:(b + 1) * S5_TIME]


def _s5_mixer(zin, a_re, a_im, log_step, b_re, b_im, c_re, c_im, d_skip, w_glu, b_glu):
    batch, seq, _ = zin.shape
    groups, states = a_re.shape
    width = groups * S5_GROUP
    tiles = width // LANES
    chains = tiles * S5_PAIRS * batch
    lr, li, bbr, bbi = _s5_discretise(a_re, a_im, log_step, b_re, b_im)

    eye = jnp.eye(S5_TILE_GROUPS, dtype=F32)
    bb = jnp.stack([bbr, bbi]).reshape(2, S5_GROUP, tiles, S5_TILE_GROUPS, states)
    wb = jnp.einsum("rhjgp,gk->jghrkp", bb, eye).reshape(tiles, LANES, 2 * S5_TILE_GROUPS * states).astype(BF16)
    cc = jnp.stack([c_re, -c_im]).reshape(2, tiles, S5_TILE_GROUPS, S5_GROUP, states)
    cs = jnp.einsum("rjghp,gk->jrkpgh", cc, eye).reshape(tiles, 2 * S5_TILE_GROUPS * states, LANES).astype(BF16)

    def per_chain(lam):
        t = lam.reshape(tiles * S5_PAIRS, 1, 2 * states)
        return jnp.broadcast_to(t, (tiles * S5_PAIRS, batch, 2 * states)).reshape(chains, 2 * states)

    const = lambda *shape: pl.BlockSpec(shape, lambda i: (0,) * len(shape))
    return pl.pallas_call(
        functools.partial(_s5_kernel, batch=batch, tiles=tiles),
        grid=(seq // S5_TIME,),
        in_specs=[
            pl.BlockSpec((batch, S5_TIME, width), lambda i: (0, i, 0)),
            const(tiles, LANES, 2 * S5_TILE_GROUPS * states),
            const(chains, LANES),
            const(chains, LANES),
            const(tiles, 2 * S5_TILE_GROUPS * states, LANES),
            const(1, width),
            const(width, width),
            const(1, width),
        ],
        out_specs=pl.BlockSpec((batch, S5_TIME, width), lambda i: (0, i, 0)),
        out_shape=jax.ShapeDtypeStruct((batch, seq, width), BF16),
        scratch_shapes=[
            pltpu.VMEM((chains * S5_PITCH, LANES), F32),
            pltpu.VMEM((chains * S5_PITCH, LANES), F32),
            pltpu.VMEM((chains, LANES), F32),
            pltpu.VMEM((chains, LANES), F32),
            pltpu.VMEM((batch * S5_TIME, width), F32),
        ],
        compiler_params=_params("arbitrary"),
        name="s5",
    )(zin, wb, per_chain(lr), per_chain(li), cs, d_skip.reshape(1, width), w_glu.astype(BF16),
      b_glu.reshape(1, width))


def _gdn_kernel(q_ref, k_ref, v_ref, z_ref, gc_ref, gr_ref, cw_ref, alc_ref, dtc_ref, alr_ref, dtr_ref,
                ng_ref, o_ref, s_ref, tail_ref, *, cps):
    heads, dk, dv = GDN_HEADS, GDN_DK, GDN_DV
    steps = cps * CHUNK

    @pl.when(pl.program_id(1) == 0)
    def _():
        s_ref[...] = jnp.zeros_like(s_ref)
        tail_ref[:, :2 * SUBLANES, :] = jnp.zeros((tail_ref.shape[0], 2 * SUBLANES, LANES), F32)

    tri, strict, tri_u = _tri_masks(CHUNK)
    row = lax.broadcasted_iota(jnp.int32, (steps, steps), 0)
    col = lax.broadcasted_iota(jnp.int32, (steps, steps), 1)
    tri_chunks = (row >= col) & (row // CHUNK == col // CHUNK)
    eye = (lax.broadcasted_iota(jnp.int32, (CHUNK, CHUNK), 0)
           == lax.broadcasted_iota(jnp.int32, (CHUNK, CHUNK), 1)).astype(F32)

    def time_rows(first, count):
        return pl.ds(2 * first, count, stride=2)

    conv = []
    for idx, ref in enumerate((q_ref, k_ref, v_ref)):
        slabs = []
        for h in range(ref.shape[1] // LANES):
            lanes = slice(h * LANES, (h + 1) * LANES)
            cur = ref[:, lanes].astype(F32)
            hist = tail_ref.at[idx * heads + h]
            hist[time_rows(SUBLANES, steps), :] = cur
            cw = cw_ref[:, idx * ref.shape[1] + h * LANES:idx * ref.shape[1] + (h + 1) * LANES]
            acc = cur * cw[GDN_CONV - 1:GDN_CONV]
            for s in range(1, GDN_CONV):
                acc = acc + hist[time_rows(SUBLANES - s, steps), :] * cw[GDN_CONV - 1 - s:GDN_CONV - s]
            hist[time_rows(0, SUBLANES), :] = cur[steps - SUBLANES:]
            slabs.append(_silu(acc))
        conv.append(slabs)
    q_all, k_all, v_all = conv

    gc = gc_ref[...]
    beta_c = jax.nn.sigmoid(gc)
    g_c = -jnp.exp(alc_ref[...]) * _softplus(gc + dtc_ref[...])
    gcum_c = _cumsum_rows(tri_chunks, g_c)
    gr = gr_ref[...].reshape(cps * 2 * heads, CHUNK)
    g_r = -jnp.exp(alr_ref[...]) * _softplus(gr + dtr_ref[...])
    gcum_r = _cumsum_lanes(g_r, tri_u)

    items = [(c, h) for c in range(cps) for h in range(heads)]
    tok = {(c, h): slice(c * CHUNK, (c + 1) * CHUNK) for c, h in items}
    q_raw = {it: q_all[it[1]][tok[it]] for it in items}
    k_raw = {it: k_all[it[1]][tok[it]] for it in items}
    v = {it: v_all[it[1]][tok[it]] for it in items}
    q_ss = {it: jnp.sum(q_raw[it] * q_raw[it], axis=-1, keepdims=True) for it in items}
    k_ss = {it: jnp.sum(k_raw[it] * k_raw[it], axis=-1, keepdims=True) for it in items}
    q = {it: q_raw[it] * lax.rsqrt(q_ss[it] + EPS) * dk ** -0.5 for it in items}
    k = {it: k_raw[it] * lax.rsqrt(k_ss[it] + EPS) for it in items}
    beta = {it: beta_c[tok[it], it[1]:it[1] + 1] for it in items}
    gcol = {it: gcum_c[tok[it], heads + it[1]:heads + it[1] + 1] for it in items}
    grow = {(c, h): gcum_r[c * 2 * heads + heads + h:c * 2 * heads + heads + h + 1, :] for c, h in items}
    glast = {it: gcol[it][CHUNK - 1:CHUNK, :] for it in items}
    decay = {it: jnp.where(tri, jnp.exp(gcol[it] - grow[it]), 0.0) for it in items}
    egc = {it: jnp.exp(gcol[it]) for it in items}
    kb = {it: k[it] * beta[it] for it in items}
    lower = {it: jnp.where(strict, _mm_nt(kb[it], k[it]) * decay[it], 0.0) for it in items}
    rhs = {it: jnp.concatenate([v[it] * beta[it], kb[it] * egc[it]], axis=1).astype(BF16) for it in items}
    qk = {it: jnp.where(tri, _mm_nt(q[it], k[it]) * decay[it], 0.0).astype(BF16) for it in items}
    q_dec = {it: (q[it] * egc[it]).astype(BF16) for it in items}
    k_dec = {it: (k[it] * jnp.exp(glast[it] - gcol[it])).astype(BF16) for it in items}
    carry = {it: jnp.exp(glast[it]) for it in items}

    power = {it: -lower[it] for it in items}
    inv = {it: eye + power[it] for it in items}
    for _ in range(int(math.log2(CHUNK)) - 1):
        for it in items:
            power[it] = _mm(power[it], power[it])
        for it in items:
            inv[it] = inv[it] + _mm(inv[it], power[it])
    sol = {it: _mm(inv[it], rhs[it]) for it in items}

    out = {}
    for c in range(cps):
        state = [s_ref[h] for h in range(heads)]
        v_new = [sol[c, h][:, :dv] - _mm(sol[c, h][:, dv:], state[h]) for h in range(heads)]
        for h in range(heads):
            out[c, h] = _mm(q_dec[c, h], state[h]) + _mm(qk[c, h], v_new[h])
        for h in range(heads):
            s_ref[h] = state[h] * carry[c, h] + _mm_tn(k_dec[c, h], v_new[h])

    mean_sq = {it: jnp.mean(out[it] * out[it], axis=-1, keepdims=True) for it in items}
    for c, h in items:
        normed = out[c, h] * lax.rsqrt(mean_sq[c, h] + EPS) * ng_ref[...]
        gate = _silu(z_ref[tok[c, h], h * dv:(h + 1) * dv].astype(F32))
        o_ref[tok[c, h], h * dv:(h + 1) * dv] = (normed * gate).astype(BF16)


def _gate_rows(gates, count):
    b, l, _ = gates.shape
    return gates[:, :, :count].reshape(b, l // CHUNK, CHUNK, count).transpose(0, 1, 3, 2)


def _lane_row(vec, offset):
    return jnp.zeros((1, LANES), F32).at[0, offset:offset + vec.shape[0]].set(vec)


def _gdn_mixer(zin, gates, conv_w, a_log, dt_bias, norm_g, *, col0, cps=MIXER_CHUNKS_PER_STEP):
    batch, seq, _ = zin.shape
    heads = GDN_HEADS
    qk_w, v_w = heads * GDN_DK, heads * GDN_DV
    assert GDN_DK == GDN_DV == LANES and col0 % qk_w == 0
    c0 = col0 // qk_w
    steps = cps * CHUNK
    zeros = jnp.zeros((heads,), F32)
    al_c, dt_c = _lane_row(a_log, heads), _lane_row(dt_bias, heads)
    al_r = jnp.tile(jnp.concatenate([zeros, a_log]), cps).reshape(cps * 2 * heads, 1)
    dt_r = jnp.tile(jnp.concatenate([zeros, dt_bias]), cps).reshape(cps * 2 * heads, 1)
    tok = lambda blk: pl.BlockSpec((None, steps, qk_w), lambda b, n, blk=blk: (b, n, c0 + blk))
    const = lambda *shape: pl.BlockSpec(shape, lambda b, n: (0,) * len(shape))
    return pl.pallas_call(
        functools.partial(_gdn_kernel, cps=cps),
        grid=(batch, seq // steps),
        in_specs=[
            tok(0), tok(1), tok(2), tok(3),
            pl.BlockSpec((None, steps, LANES), lambda b, n: (b, n, 0)),
            pl.BlockSpec((None, cps, 2 * heads, CHUNK), lambda b, n: (b, n, 0, 0)),
            const(GDN_CONV, 2 * qk_w + v_w),
            const(1, LANES), const(1, LANES), const(cps * 2 * heads, 1), const(cps * 2 * heads, 1),
            const(1, GDN_DV),
        ],
        out_specs=pl.BlockSpec((None, steps, v_w), lambda b, n: (b, n, 0)),
        out_shape=jax.ShapeDtypeStruct((batch, seq, v_w), BF16),
        scratch_shapes=[
            pltpu.VMEM((heads, GDN_DK, GDN_DV), F32),
            pltpu.VMEM((3 * heads, 2 * (SUBLANES + steps), LANES), F32),
        ],
        compiler_params=_params("parallel", "arbitrary"),
        name="gdn",
    )(zin, zin, zin, zin, gates, _gate_rows(gates, 2 * heads), conv_w, al_c, dt_c, al_r, dt_r,
      norm_g.reshape(1, GDN_DV))


def _mlstm_block(q_ref, k_ref, v_ref, op_ref, gc_ref, gr_ref, bc_ref, br_ref, ng_ref, o_ref,
                 c_ref, n_ref, m_ref, *, cps):
    heads, dk, dv = MLSTM_HEADS, MLSTM_DK, MLSTM_DV
    steps = cps * CHUNK

    @pl.when(pl.program_id(1) == 0)
    def _():
        c_ref[...] = jnp.zeros_like(c_ref)
        n_ref[...] = jnp.zeros_like(n_ref)
        m_ref[...] = jnp.zeros_like(m_ref)

    tri, _, tri_u = _tri_masks(CHUNK)
    row = lax.broadcasted_iota(jnp.int32, (steps, steps), 0)
    col = lax.broadcasted_iota(jnp.int32, (steps, steps), 1)
    tri_chunks = (row >= col) & (row // CHUNK == col // CHUNK)
    pre_c = gc_ref[...] + bc_ref[...]
    bcum_c = _cumsum_rows(tri_chunks, -_softplus(-pre_c))
    pre_r = gr_ref[...].reshape(cps * 2 * heads, CHUNK) + br_ref[...]
    bcum_r = _cumsum_lanes(-_softplus(-pre_r), tri_u)

    items = [(c, h) for c in range(cps) for h in range(heads)]
    q_s = {it: q_ref[it[0] * CHUNK:(it[0] + 1) * CHUNK, it[1] * dk:(it[1] + 1) * dk].astype(F32) * dk ** -0.5
           for it in items}
    k_s = {it: k_ref[it[0] * CHUNK:(it[0] + 1) * CHUNK, it[1] * dk:(it[1] + 1) * dk].astype(F32) for it in items}
    v_s = {it: v_ref[it[0] * CHUNK:(it[0] + 1) * CHUNK, it[1] * dv:(it[1] + 1) * dv].astype(BF16) for it in items}
    qk = {it: _mm_nt(q_s[it], k_s[it]) for it in items}
    tok = {(c, h): slice(c * CHUNK, (c + 1) * CHUNK) for c, h in items}
    gl = {(c, h): c * 2 * heads + h for c, h in items}
    ig_col = {it: pre_c[tok[it], it[1]:it[1] + 1] for it in items}
    ig_row = {it: pre_r[gl[it]:gl[it] + 1, :] for it in items}
    b_col = {it: bcum_c[tok[it], heads + it[1]:heads + it[1] + 1] for it in items}
    b_row = {it: bcum_r[gl[it] + heads:gl[it] + heads + 1, :] for it in items}
    b_last = {it: b_col[it][CHUNK - 1:CHUNK, :] for it in items}
    intra_log = {it: jnp.where(tri, b_col[it] - b_row[it] + ig_row[it], -jnp.inf) for it in items}
    intra_max = {it: jnp.max(intra_log[it], axis=-1, keepdims=True) for it in items}
    upd_log = {it: b_last[it] - b_col[it] + ig_col[it] for it in items}
    upd_max = {it: jnp.max(upd_log[it], axis=0, keepdims=True) for it in items}
    gate = {it: dict(b_col=b_col[it], b_last=b_last[it], intra_log=intra_log[it], upd_log=upd_log[it],
                     intra_max=intra_max[it], upd_max=upd_max[it]) for it in items}

    m_run = [m_ref[h][0:1, 0:1] for h in range(heads)]
    for c, h in items:
        g = gate[c, h]
        m_new = jnp.maximum(g["b_last"] + m_run[h], g["upd_max"])
        g.update(m_in=m_run[h], m_out=m_new, carry=jnp.exp(g["b_last"] + m_run[h] - m_new))
        m_run[h] = m_new

    inter_log = {it: gate[it]["b_col"] + gate[it]["m_in"] for it in items}
    m_s = {it: jnp.maximum(inter_log[it], gate[it]["intra_max"]) for it in items}
    inter_w = {it: jnp.exp(inter_log[it] - m_s[it]) for it in items}
    s_mat = {it: qk[it] * jnp.exp(gate[it]["intra_log"] - m_s[it]) for it in items}
    s_v = {it: _mm(s_mat[it], v_s[it]) for it in items}
    s_sum = {it: jnp.sum(s_mat[it], axis=-1, keepdims=True) for it in items}
    kw = {it: k_s[it] * jnp.exp(gate[it]["upd_log"] - gate[it]["m_out"]) for it in items}
    kw_v = {it: _mm_tn(kw[it], v_s[it]) for it in items}
    kw_sum = {it: jnp.sum(kw[it], axis=0, keepdims=True) for it in items}

    c_run = [c_ref[h] for h in range(heads)]
    n_run = [n_ref[h][0:1, :] for h in range(heads)]
    c_in, n_in = {}, {}
    for c, h in items:
        c_in[c, h], n_in[c, h] = c_run[h], n_run[h]
        c_run[h] = gate[c, h]["carry"] * c_run[h] + kw_v[c, h]
        n_run[h] = gate[c, h]["carry"] * n_run[h] + kw_sum[c, h]
    for h in range(heads):
        c_ref[h] = c_run[h]
        n_ref[h] = jnp.broadcast_to(n_run[h], (SUBLANES, dk))
        m_ref[h] = jnp.broadcast_to(m_run[h], (SUBLANES, LANES))
    q_c = {it: _mm(q_s[it], c_in[it]) for it in items}
    den = {it: inter_w[it] * jnp.sum(q_s[it] * n_in[it], axis=-1, keepdims=True) + s_sum[it] for it in items}
    hid = {it: (inter_w[it] * q_c[it] + s_v[it]) / jnp.maximum(jnp.abs(den[it]), jnp.exp(-m_s[it])) for it in items}
    mean_sq = {it: jnp.mean(hid[it] * hid[it], axis=-1, keepdims=True) for it in items}
    for c, h in items:
        rows = slice(c * CHUNK, (c + 1) * CHUNK)
        out_gate = jax.nn.sigmoid(op_ref[rows, h * dv:(h + 1) * dv].astype(F32))
        normed = hid[c, h] * lax.rsqrt(mean_sq[c, h] + EPS) * ng_ref[...]
        o_ref[rows, h * dv:(h + 1) * dv] = (out_gate * normed).astype(BF16)


def _mlstm_operands(zin, gates, gate_bias, norm_g, cps):
    heads = MLSTM_HEADS
    qk_w, v_w = heads * MLSTM_DK, heads * MLSTM_DV
    steps = cps * CHUNK
    bias = gate_bias.reshape(2 * heads)
    tok = lambda w, blk: pl.BlockSpec((None, steps, w), lambda b, n: (b, n, blk))
    const = lambda *shape: pl.BlockSpec(shape, lambda b, n: (0,) * len(shape))
    in_specs = [
        tok(qk_w, 0), tok(qk_w, 1), tok(v_w, 1), tok(v_w, 2),
        pl.BlockSpec((None, steps, LANES), lambda b, n: (b, n, 0)),
        pl.BlockSpec((None, cps, 2 * heads, CHUNK), lambda b, n: (b, n, 0, 0)),
        const(1, LANES), const(cps * 2 * heads, 1), const(1, MLSTM_DV),
    ]
    operands = [zin, zin, zin, zin, gates, _gate_rows(gates, 2 * heads), _lane_row(bias, 0),
                jnp.tile(bias, cps).reshape(cps * 2 * heads, 1), norm_g.reshape(1, MLSTM_DV)]
    scratch = [
        pltpu.VMEM((heads, MLSTM_DK, MLSTM_DV), F32),
        pltpu.VMEM((heads, SUBLANES, MLSTM_DK), F32),
        pltpu.VMEM((heads, SUBLANES, LANES), F32),
    ]
    return in_specs, operands, scratch


def _ret_block(q_ref, k_ref, v_ref, g_ref, pos_ref, freq_ref, dmat_ref, xi_ref, zeta_ref, gam_ref, ng_ref,
               o_ref, s_ref, *, cps):
    heads, dk, dv = RET_HEADS, RET_DK, RET_DV
    half = dk // 2
    steps = cps * CHUNK

    @pl.when(pl.program_id(1) == 0)
    def _():
        s_ref[...] = jnp.zeros_like(s_ref)

    ang = pos_ref[...].reshape(steps, 1).astype(F32) * freq_ref[...]
    cos, sin = jnp.cos(ang), jnp.sin(ang)
    lane = lax.broadcasted_iota(jnp.int32, (steps, dk), 1)
    sin_signed = jnp.where(lane < half, -sin, sin)

    def rotary(x):
        return x * cos + pltpu.roll(x, half, 1) * sin_signed

    q_rot = [rotary(q_ref[:, h * dk:(h + 1) * dk].astype(F32)) * dk ** -0.5 for h in range(heads)]
    k_rot = [rotary(k_ref[:, h * dk:(h + 1) * dk].astype(F32)) for h in range(heads)]
    items = [(c, h) for c in range(cps) for h in range(heads)]
    rows = {c: slice(c * CHUNK, (c + 1) * CHUNK) for c in range(cps)}
    q_s = {(c, h): q_rot[h][rows[c]].astype(BF16) for c, h in items}
    k_s = {(c, h): k_rot[h][rows[c]] for c, h in items}
    v_s = {(c, h): v_ref[rows[c], h * dv:(h + 1) * dv].astype(BF16) for c, h in items}
    qk = {it: _mm_nt(q_s[it], k_s[it]) * dmat_ref[it[1]] for it in items}
    intra = {it: _mm(qk[it], v_s[it]) for it in items}
    k_v = {it: _mm_tn(k_s[it] * zeta_ref[it[1]], v_s[it]) for it in items}

    s_run = [s_ref[h] for h in range(heads)]
    s_in = {}
    for c, h in items:
        s_in[c, h] = s_run[h]
        s_run[h] = s_run[h] * gam_ref[h] + k_v[c, h]
    for h in range(heads):
        s_ref[h] = s_run[h]
    inter = {it: _mm(q_s[it], s_in[it]) for it in items}
    y = {it: intra[it] + inter[it] * xi_ref[it[1]] for it in items}
    centred = {it: y[it] - jnp.mean(y[it], axis=-1, keepdims=True) for it in items}
    var = {it: jnp.mean(centred[it] * centred[it], axis=-1, keepdims=True) for it in items}
    for c, h in items:
        normed = centred[c, h] * lax.rsqrt(var[c, h] + EPS) * ng_ref[...]
        gate = _silu(g_ref[rows[c], h * dv:(h + 1) * dv].astype(F32))
        o_ref[rows[c], h * dv:(h + 1) * dv] = (normed * gate).astype(BF16)


def _ret_operands(zin, positions, norm_g, col0, cps):
    batch, seq, _ = zin.shape
    heads, dk = RET_HEADS, RET_DK
    qk_w, v_w = heads * RET_DK, heads * RET_DV
    assert col0 % v_w == 0
    cq, cv = col0 // qk_w, col0 // v_w
    half = dk // 2
    steps = cps * CHUNK
    inv_freq = ROPE_BASE ** (-jnp.arange(half, dtype=F32) / half)
    freq = jnp.concatenate([inv_freq, inv_freq]).reshape(1, dk)
    log_gamma = jnp.log1p(-jnp.exp2(-5.0 - jnp.arange(heads, dtype=F32)))
    idx = jnp.arange(CHUNK, dtype=F32)
    tri = jnp.tril(jnp.ones((CHUNK, CHUNK), dtype=bool))
    diff = jnp.where(tri, idx[:, None] - idx[None, :], 0.0)
    dmat = jnp.where(tri, jnp.exp(diff * log_gamma[:, None, None]), 0.0)
    xi = jnp.exp((idx + 1.0) * log_gamma[:, None])[:, :, None]
    zeta = jnp.exp((CHUNK - 1.0 - idx) * log_gamma[:, None])[:, :, None]
    gamma_c = jnp.broadcast_to(jnp.exp(CHUNK * log_gamma)[:, None, None], (heads, 1, RET_DV))
    pos = positions.reshape(batch, seq // CHUNK, CHUNK, 1)
    tok = lambda w, blk: pl.BlockSpec((None, steps, w), lambda b, n: (b, n, blk))
    const = lambda *shape: pl.BlockSpec(shape, lambda b, n: (0,) * len(shape))
    in_specs = [
        tok(qk_w, cq), tok(qk_w, cq + 1), tok(v_w, cv + 1), tok(v_w, cv + 2),
        pl.BlockSpec((None, cps, CHUNK, 1), lambda b, n: (b, n, 0, 0)),
        const(1, dk), const(heads, CHUNK, CHUNK), const(heads, CHUNK, 1), const(heads, CHUNK, 1),
        const(heads, 1, RET_DV), const(1, RET_DV),
    ]
    operands = [zin, zin, zin, zin, pos, freq, dmat, xi, zeta, gamma_c, norm_g.reshape(1, RET_DV)]
    return in_specs, operands, [pltpu.VMEM((heads, RET_DK, RET_DV), F32)]


def _odd_kernel(*refs, n_mlstm, n_ret, cps):
    mlstm_in, ret_in = refs[:n_mlstm], refs[n_mlstm:n_mlstm + n_ret]
    yc_ref, yd_ref, c_ref, n_ref, m_ref, s_ref = refs[n_mlstm + n_ret:]
    _ret_block(*ret_in, yd_ref, s_ref, cps=cps)
    _mlstm_block(*mlstm_in, yc_ref, c_ref, n_ref, m_ref, cps=cps)


def _odd_mixers(zin, gates, positions, gate_bias, mlstm_g, ret_g, *, ret_col0, cps=MIXER_CHUNKS_PER_STEP):
    batch, seq, _ = zin.shape
    steps = cps * CHUNK
    m_specs, m_ops, m_scratch = _mlstm_operands(zin, gates, gate_bias, mlstm_g, cps)
    r_specs, r_ops, r_scratch = _ret_operands(zin, positions, ret_g, ret_col0, cps)
    widths = (MLSTM_HEADS * MLSTM_DV, RET_HEADS * RET_DV)
    return pl.pallas_call(
        functools.partial(_odd_kernel, n_mlstm=len(m_ops), n_ret=len(r_ops), cps=cps),
        grid=(batch, seq // steps),
        in_specs=m_specs + r_specs,
        out_specs=[pl.BlockSpec((None, steps, w), lambda b, n: (b, n, 0)) for w in widths],
        out_shape=[jax.ShapeDtypeStruct((batch, seq, w), BF16) for w in widths],
        scratch_shapes=m_scratch + r_scratch,
        compiler_params=_params("parallel", "arbitrary"),
        name="mlstm_retention",
    )(*m_ops, *r_ops)


def _pad_rows(w):
    return jnp.pad(w, ((0, LANES - w.shape[0]), (0, 0)))


def _even_mixer(x, batch, norm_g, w_in, w_out, a_re, a_im, log_step, b_re, b_im, c_re, c_im, d_skip, w_glu,
                b_glu, conv_w, a_log, dt_bias, gdn_g):
    s5_w = a_re.shape[0] * S5_GROUP
    main_w = s5_w + GDN_HEADS * (2 * GDN_DK + 2 * GDN_DV)
    w_t = w_in.T
    w_gate = _pad_rows(w_t[main_w:]).astype(BF16)
    zin, gates = _inproj(x, norm_g, w_t, [(0, main_w)], w_gate)
    zin = zin.reshape(batch, -1, main_w)
    gates = gates.reshape(batch, -1, LANES)
    ya = _s5_mixer(zin, a_re, a_im, log_step, b_re, b_im, c_re, c_im, d_skip, w_glu, b_glu)
    yb = _gdn_mixer(zin, gates, conv_w, a_log, dt_bias, gdn_g, col0=s5_w)
    m = x.shape[0]
    return _outproj(x, ya.reshape(m, -1), yb.reshape(m, -1), w_out)


def _odd_mixer(x, batch, positions, norm_g, w_in, w_out, gate_bias, mlstm_g, ret_g):
    c_main = MLSTM_HEADS * (2 * MLSTM_DK + 2 * MLSTM_DV)
    r_main = RET_HEADS * (2 * RET_DK + 2 * RET_DV)
    n_gate = 2 * MLSTM_HEADS
    w_t = w_in.T
    w_gate = _pad_rows(w_t[c_main:c_main + n_gate]).astype(BF16)
    zin, gates = _inproj(x, norm_g, w_t, [(0, c_main), (c_main + n_gate, r_main)], w_gate)
    zin = zin.reshape(batch, -1, c_main + r_main)
    gates = gates.reshape(batch, -1, LANES)
    yc, yd = _odd_mixers(zin, gates, positions, gate_bias, mlstm_g, ret_g, ret_col0=c_main)
    m = x.shape[0]
    return _outproj(x, yc.reshape(m, -1), yd.reshape(m, -1), w_out)


def kernel(x, positions, ffn_norm, ffn_w1, ffn_w3, ffn_w2, mix_norm, even_w_in, even_w_out, s5_a_re, s5_a_im, s5_log_step, s5_b_re, s5_b_im, s5_c_re, s5_c_im, s5_d, s5_w_glu, s5_b_glu, gdn_conv_w, gdn_a_log, gdn_dt_bias, gdn_norm, odd_w_in, odd_w_out, mlstm_gate_bias, mlstm_norm, ret_norm, final_norm):
    batch, seq, d = x.shape
    depth = ffn_norm.shape[0]
    x = x.reshape(batch * seq, d)

    def ffn(x, layer, which, final_g=None):
        return _ffn(x, ffn_norm[layer, which], ffn_w1, ffn_w3, ffn_w2, layer, which, final_g)

    for layer in range(depth):
        x = ffn(x, layer, 0)
        j = layer // 2
        if layer % 2 == 0:
            x = _even_mixer(x, batch, mix_norm[layer], even_w_in[j], even_w_out[j], s5_a_re[j], s5_a_im[j],
                            s5_log_step[j], s5_b_re[j], s5_b_im[j], s5_c_re[j], s5_c_im[j], s5_d[j].reshape(-1),
                            s5_w_glu[j], s5_b_glu[j], gdn_conv_w[j], gdn_a_log[j], gdn_dt_bias[j], gdn_norm[j])
        else:
            x = _odd_mixer(x, batch, positions, mix_norm[layer], odd_w_in[j], odd_w_out[j], mlstm_gate_bias[j],
                           mlstm_norm[j], ret_norm[j])
        x = ffn(x, layer, 1, final_norm if layer == depth - 1 else None)
    return x.reshape(batch, seq, d)
```

```python
import functools
import math

import jax
import jax.numpy as jnp
from jax import lax
from jax.experimental import pallas as pl
from jax.experimental.pallas import tpu as pltpu

F32 = jnp.float32
BF16 = jnp.bfloat16

EPS = 1e-6
CHUNK = 64
ROPE_BASE = 10000.0
LANES = 128
SUBLANES = 8
VMEM_LIMIT_BYTES = 60 * 1024 * 1024

S5_GROUP = 16
S5_TILE_GROUPS = LANES // S5_GROUP
S5_PAIRS = S5_TILE_GROUPS // 2
S5_TIME = 256
S5_PITCH = S5_TIME + 4
S5_SCAN_SLICE = 32

GDN_HEADS = 8
GDN_DK = 128
GDN_DV = 128
GDN_CONV = 4
MIXER_CHUNKS_PER_STEP = 4
MLSTM_HEADS = 4
MLSTM_DK = 128
MLSTM_DV = 256
RET_HEADS = 4
RET_DK = 128
RET_DV = 256


def _params(*semantics):
    return pltpu.CompilerParams(dimension_semantics=semantics, vmem_limit_bytes=VMEM_LIMIT_BYTES)


def _mm(a, b):
    return jnp.dot(a.astype(BF16), b.astype(BF16), preferred_element_type=F32)


def _mm_nt(a, b):
    return lax.dot_general(a.astype(BF16), b.astype(BF16), (((1,), (1,)), ((), ())), preferred_element_type=F32)


def _mm_tn(a, b):
    return lax.dot_general(a.astype(BF16), b.astype(BF16), (((0,), (0,)), ((), ())), preferred_element_type=F32)


def _split3(x):
    x1 = x.astype(BF16)
    r1 = x - x1.astype(F32)
    x2 = r1.astype(BF16)
    x3 = (r1 - x2.astype(F32)).astype(BF16)
    return x1, x2, x3


def _cumsum_rows(tri_lower, x):
    t = tri_lower.astype(BF16)
    return sum(jnp.dot(t, p, preferred_element_type=F32) for p in _split3(x))


def _cumsum_lanes(x, tri_upper):
    t = tri_upper.astype(BF16)
    return sum(jnp.dot(p, t, preferred_element_type=F32) for p in _split3(x))


def _rms_norm(x, g):
    return x * lax.rsqrt(jnp.mean(x * x, axis=-1, keepdims=True) + EPS) * g


def _silu(x):
    return x * jax.nn.sigmoid(x)


def _softplus(x):
    return jnp.maximum(x, 0.0) + jnp.log(1.0 + jnp.exp(-jnp.abs(x)))


def _tri_masks(n):
    r = lax.broadcasted_iota(jnp.int32, (n, n), 0)
    c = lax.broadcasted_iota(jnp.int32, (n, n), 1)
    return r >= c, r > c, r <= c


def _ffn_kernel(x_ref, g_ref, w1_ref, w3_ref, w2_ref, fg_ref, o_ref, h_ref, *, final_norm):
    j = pl.program_id(1)

    @pl.when(j == 0)
    def _():
        x = x_ref[...]
        h_ref[...] = _rms_norm(x, g_ref[...]).astype(BF16)
        o_ref[...] = x

    h = h_ref[...]
    a = jnp.dot(h, w1_ref[...].astype(BF16), preferred_element_type=F32)
    b = jnp.dot(h, w3_ref[...].astype(BF16), preferred_element_type=F32)
    o_ref[...] += 0.5 * jnp.dot((_silu(a) * b).astype(BF16), w2_ref[...].astype(BF16), preferred_element_type=F32)

    if final_norm:
        @pl.when(j == pl.num_programs(1) - 1)
        def _():
            o_ref[...] = _rms_norm(o_ref[...], fg_ref[...])


def _ffn(x, g, w1, w3, w2, layer, which, final_g=None, *, tm=1024, tf=256):
    m, d = x.shape
    f = w1.shape[-1]
    final_norm = final_g is not None
    fg = final_g if final_norm else g
    return pl.pallas_call(
        functools.partial(_ffn_kernel, final_norm=final_norm),
        grid=(m // tm, f // tf),
        in_specs=[
            pl.BlockSpec((tm, d), lambda i, j: (i, 0)),
            pl.BlockSpec((1, d), lambda i, j: (0, 0)),
            pl.BlockSpec((None, None, d, tf), lambda i, j: (layer, which, 0, j)),
            pl.BlockSpec((None, None, d, tf), lambda i, j: (layer, which, 0, j)),
            pl.BlockSpec((None, None, tf, d), lambda i, j: (layer, which, j, 0)),
            pl.BlockSpec((1, d), lambda i, j: (0, 0)),
        ],
        out_specs=pl.BlockSpec((tm, d), lambda i, j: (i, 0)),
        out_shape=jax.ShapeDtypeStruct((m, d), F32),
        scratch_shapes=[pltpu.VMEM((tm, d), BF16)],
        compiler_params=_params("parallel", "arbitrary"),
        name="ffn",
    )(x, g.reshape(1, d), w1, w3, w2, fg.reshape(1, d))


def _inproj_kernel(x_ref, g_ref, w_ref, wg_ref, o_ref, og_ref, h_ref, wbf_ref):
    i, j = pl.program_id(0), pl.program_id(1)

    @pl.when(j == 0)
    def _():
        h = _rms_norm(x_ref[...], g_ref[...]).astype(BF16)
        h_ref[...] = h
        og_ref[...] = _mm_nt(h, wg_ref[...])

    @pl.when(i == 0)
    def _():
        wbf_ref[j] = w_ref[...].astype(BF16)

    o_ref[...] = _mm_nt(h_ref[...], wbf_ref[j]).astype(BF16)


def _inproj(x, g, w_t, segments, w_gate, *, tm=1024, tn=512):
    m, d = x.shape
    starts = []
    for row0, width in segments:
        assert row0 % SUBLANES == 0 and width % tn == 0
        starts += [row0 + tn * k for k in range(width // tn)]
    nj = len(starts)

    def block_row(j):
        row, base = 0, 0
        for row0, width in segments:
            count = width // tn
            row = row + jnp.where((j >= base) & (j < base + count), row0 + tn * (j - base), 0)
            base += count
        return row

    def w_index(i, j):
        return pl.multiple_of(block_row(jnp.where(i == 0, j, nj - 1)), SUBLANES), 0

    return pl.pallas_call(
        _inproj_kernel,
        grid=(m // tm, nj),
        in_specs=[
            pl.BlockSpec((tm, d), lambda i, j: (i, 0)),
            pl.BlockSpec((1, d), lambda i, j: (0, 0)),
            pl.BlockSpec((pl.Element(tn), pl.Element(d)), w_index),
            pl.BlockSpec((LANES, d), lambda i, j: (0, 0)),
        ],
        out_specs=[
            pl.BlockSpec((tm, tn), lambda i, j: (i, j)),
            pl.BlockSpec((tm, LANES), lambda i, j: (i, 0)),
        ],
        out_shape=[jax.ShapeDtypeStruct((m, nj * tn), BF16), jax.ShapeDtypeStruct((m, LANES), F32)],
        scratch_shapes=[pltpu.VMEM((tm, d), BF16), pltpu.VMEM((nj, tn, d), BF16)],
        compiler_params=_params("arbitrary", "arbitrary"),
        name="inproj",
    )(x, g.reshape(1, d), w_t, w_gate)


def _outproj_kernel(x_ref, ya_ref, yb_ref, w_ref, o_ref, wbf_ref):
    @pl.when(pl.program_id(1) == 0)
    def _():
        wbf_ref[...] = w_ref[...].astype(BF16)

    ka = ya_ref.shape[1]
    o_ref[...] = (x_ref[...] + jnp.dot(ya_ref[...], wbf_ref[:ka, :], preferred_element_type=F32)
                  + jnp.dot(yb_ref[...], wbf_ref[ka:, :], preferred_element_type=F32))


def _outproj(x, ya, yb, w, *, tm=1024, tn=1024):
    m, d = x.shape
    ka, kb = ya.shape[1], yb.shape[1]
    return pl.pallas_call(
        _outproj_kernel,
        grid=(d // tn, m // tm),
        in_specs=[
            pl.BlockSpec((tm, tn), lambda j, i: (i, j)),
            pl.BlockSpec((tm, ka), lambda j, i: (i, 0)),
            pl.BlockSpec((tm, kb), lambda j, i: (i, 0)),
            pl.BlockSpec((ka + kb, tn), lambda j, i: (0, j)),
        ],
        out_specs=pl.BlockSpec((tm, tn), lambda j, i: (i, j)),
        out_shape=jax.ShapeDtypeStruct((m, d), F32),
        scratch_shapes=[pltpu.VMEM((ka + kb, tn), BF16)],
        compiler_params=_params("parallel", "arbitrary"),
        name="outproj",
    )(x, ya, yb, w)


def _s5_disc_kernel(are_ref, aim_ref, step_ref, bre_ref, bim_ref, lr_ref, li_ref, bbr_ref, bbi_ref):
    ar, ai = are_ref[...], aim_ref[...]
    step = jnp.exp(step_ref[...])
    mag = jnp.exp(ar * step)
    lr, li = mag * jnp.cos(ai * step), mag * jnp.sin(ai * step)
    den = ar * ar + ai * ai
    fr = ((lr - 1.0) * ar + li * ai) / den
    fi = (li * ar - (lr - 1.0) * ai) / den
    lr_ref[...] = lr
    li_ref[...] = li
    br, bi = bre_ref[...], bim_ref[...]
    bbr_ref[...] = fr[None] * br - fi[None] * bi
    bbi_ref[...] = fr[None] * bi + fi[None] * br


def _s5_discretise(a_re, a_im, log_step, b_re, b_im):
    g, p = a_re.shape
    h = b_re.shape[-1]
    sd = jax.ShapeDtypeStruct
    return pl.pallas_call(
        _s5_disc_kernel,
        out_shape=[sd((g, p), F32), sd((g, p), F32), sd((h, g, p), F32), sd((h, g, p), F32)],
        name="s5_discretise",
    )(a_re, a_im, log_step.reshape(g, 1), b_re.transpose(2, 0, 1), b_im.transpose(2, 0, 1))


def _s5_kernel(u_ref, wb_ref, lr_ref, li_ref, cs_ref, dsk_ref, wglu_ref, bglu_ref, o_ref,
               rre_ref, rim_ref, xr_ref, xi_ref, y_ref, *, batch, tiles):
    per = tiles * S5_PAIRS

    @pl.when(pl.program_id(0) == 0)
    def _():
        xr_ref[...] = jnp.zeros_like(xr_ref)
        xi_ref[...] = jnp.zeros_like(xi_ref)

    def chain_rows(j, pair, b):
        return pl.ds((b * per + j * S5_PAIRS + pair) * S5_PITCH, S5_TIME)

    def drive(b):
        for j in range(tiles):
            d = jnp.dot(u_ref[b, :, j * LANES:(j + 1) * LANES].astype(BF16), wb_ref[j], preferred_element_type=F32)
            for pair in range(S5_PAIRS):
                rre_ref[chain_rows(j, pair, b), :] = d[:, pair * LANES:(pair + 1) * LANES]
                rim_ref[chain_rows(j, pair, b), :] = d[:, (S5_PAIRS + pair) * LANES:(S5_PAIRS + pair + 1) * LANES]
            yield

    def scan(b):
        lr, li = lr_ref[...], li_ref[...]
        own = pl.ds(b * per, per)
        xr, xi = xr_ref[own, :], xi_ref[own, :]
        for t in range(S5_TIME):
            rows = pl.ds(b * per * S5_PITCH + t, per, stride=S5_PITCH)
            xr, xi = lr * xr - li * xi + rre_ref[rows, :], lr * xi + li * xr + rim_ref[rows, :]
            rre_ref[rows, :] = xr
            rim_ref[rows, :] = xi
            if t % S5_SCAN_SLICE == S5_SCAN_SLICE - 1:
                yield
        xr_ref[own, :] = xr
        xi_ref[own, :] = xi

    def readout(b):
        for j in range(tiles):
            states = jnp.concatenate(
                [rre_ref[chain_rows(j, pair, b), :].astype(BF16) for pair in range(S5_PAIRS)]
                + [rim_ref[chain_rows(j, pair, b), :].astype(BF16) for pair in range(S5_PAIRS)], axis=1)
            cols = slice(j * LANES, (j + 1) * LANES)
            y = jnp.dot(states, cs_ref[j], preferred_element_type=F32) + dsk_ref[:, cols] * u_ref[b, :, cols].astype(F32)
            y_ref[b, :, cols] = jax.nn.gelu(y)
            yield
        y = y_ref[b]
        gate = jnp.dot(y.astype(BF16), wglu_ref[...], preferred_element_type=F32) + bglu_ref[...]
        o_ref[b] = (y * jax.nn.sigmoid(gate)).astype(BF16)
        yield

    def alongside(vector_work, matrix_work):
        done = object()
        while next(vector_work, done) is not done:
            next(matrix_work, done)
        for _ in matrix_work:
            pass

    for _ in drive(0):
        pass
    for b in range(batch):
        nxt = drive(b + 1) if b + 1 < batch else iter(())
        prev = readout(b - 1) if b > 0 else iter(())
        alongside(scan(b), (x for gen in (prev, nxt) for x in gen))
    for _ in readout(batch - 1):
        pass


def _s5_mixer(zin, a_re, a_im, log_step, b_re, b_im, c_re, c_im, d_skip, w_glu, b_glu):
    batch, seq, _ = zin.shape
    groups, states = a_re.shape
    width = groups * S5_GROUP
    tiles = width // LANES
    chains = tiles * S5_PAIRS * batch
    lr, li, bbr, bbi = _s5_discretise(a_re, a_im, log_step, b_re, b_im)

    eye = jnp.eye(S5_TILE_GROUPS, dtype=F32)
    bb = jnp.stack([bbr, bbi]).reshape(2, S5_GROUP, tiles, S5_TILE_GROUPS, states)
    wb = jnp.einsum("rhjgp,gk->jghrkp", bb, eye).reshape(tiles, LANES, 2 * S5_TILE_GROUPS * states).astype(BF16)
    cc = jnp.stack([c_re, -c_im]).reshape(2, tiles, S5_TILE_GROUPS, S5_GROUP, states)
    cs = jnp.einsum("rjghp,gk->jrkpgh", cc, eye).reshape(tiles, 2 * S5_TILE_GROUPS * states, LANES).astype(BF16)

    def per_chain(lam):
        return lam.reshape(tiles * S5_PAIRS, 2 * states)

    const = lambda *shape: pl.BlockSpec(shape, lambda i: (0,) * len(shape))
    return pl.pallas_call(
        functools.partial(_s5_kernel, batch=batch, tiles=tiles),
        grid=(seq // S5_TIME,),
        in_specs=[
            pl.BlockSpec((batch, S5_TIME, width), lambda i: (0, i, 0)),
            const(tiles, LANES, 2 * S5_TILE_GROUPS * states),
            const(chains // batch, LANES),
            const(chains // batch, LANES),
            const(tiles, 2 * S5_TILE_GROUPS * states, LANES),
            const(1, width),
            const(width, width),
            const(1, width),
        ],
        out_specs=pl.BlockSpec((batch, S5_TIME, width), lambda i: (0, i, 0)),
        out_shape=jax.ShapeDtypeStruct((batch, seq, width), BF16),
        scratch_shapes=[
            pltpu.VMEM((chains * S5_PITCH, LANES), F32),
            pltpu.VMEM((chains * S5_PITCH, LANES), F32),
            pltpu.VMEM((chains, LANES), F32),
            pltpu.VMEM((chains, LANES), F32),
            pltpu.VMEM((batch, S5_TIME, width), F32),
        ],
        compiler_params=_params("arbitrary"),
        name="s5",
    )(zin, wb, per_chain(lr), per_chain(li), cs, d_skip.reshape(1, width), w_glu.astype(BF16),
      b_glu.reshape(1, width))


def _gdn_kernel(q_ref, k_ref, v_ref, z_ref, gc_ref, gr_ref, cw_ref, alc_ref, dtc_ref, alr_ref, dtr_ref,
                ng_ref, o_ref, s_ref, tail_ref, *, cps):
    heads, dk, dv = GDN_HEADS, GDN_DK, GDN_DV
    steps = cps * CHUNK

    @pl.when(pl.program_id(1) == 0)
    def _():
        s_ref[...] = jnp.zeros_like(s_ref)
        tail_ref[:, :2 * SUBLANES, :] = jnp.zeros((tail_ref.shape[0], 2 * SUBLANES, LANES), F32)

    tri, strict, tri_u = _tri_masks(CHUNK)
    row = lax.broadcasted_iota(jnp.int32, (steps, steps), 0)
    col = lax.broadcasted_iota(jnp.int32, (steps, steps), 1)
    tri_chunks = (row >= col) & (row // CHUNK == col // CHUNK)
    eye = (lax.broadcasted_iota(jnp.int32, (CHUNK, CHUNK), 0)
           == lax.broadcasted_iota(jnp.int32, (CHUNK, CHUNK), 1)).astype(F32)

    def time_rows(first, count):
        return pl.ds(2 * first, count, stride=2)

    conv = []
    for idx, ref in enumerate((q_ref, k_ref, v_ref)):
        slabs = []
        for h in range(ref.shape[1] // LANES):
            lanes = slice(h * LANES, (h + 1) * LANES)
            cur = ref[:, lanes].astype(F32)
            hist = tail_ref.at[idx * heads + h]
            hist[time_rows(SUBLANES, steps), :] = cur
            cw = cw_ref[:, idx * ref.shape[1] + h * LANES:idx * ref.shape[1] + (h + 1) * LANES]
            acc = cur * cw[GDN_CONV - 1:GDN_CONV]
            for s in range(1, GDN_CONV):
                acc = acc + hist[time_rows(SUBLANES - s, steps), :] * cw[GDN_CONV - 1 - s:GDN_CONV - s]
            hist[time_rows(0, SUBLANES), :] = cur[steps - SUBLANES:]
            slabs.append(_silu(acc))
        conv.append(slabs)
    q_all, k_all, v_all = conv

    gc = gc_ref[...]
    beta_c = jax.nn.sigmoid(gc)
    g_c = -jnp.exp(alc_ref[...]) * _softplus(gc + dtc_ref[...])
    gcum_c = _cumsum_rows(tri_chunks, g_c)
    gr = gr_ref[...].reshape(cps * 2 * heads, CHUNK)
    g_r = -jnp.exp(alr_ref[...]) * _softplus(gr + dtr_ref[...])
    gcum_r = _cumsum_lanes(g_r, tri_u)

    items = [(c, h) for c in range(cps) for h in range(heads)]
    tok = {(c, h): slice(c * CHUNK, (c + 1) * CHUNK) for c, h in items}
    q_raw = {it: q_all[it[1]][tok[it]] for it in items}
    k_raw = {it: k_all[it[1]][tok[it]] for it in items}
    v = {it: v_all[it[1]][tok[it]] for it in items}
    q_ss = {it: jnp.sum(q_raw[it] * q_raw[it], axis=-1, keepdims=True) for it in items}
    k_ss = {it: jnp.sum(k_raw[it] * k_raw[it], axis=-1, keepdims=True) for it in items}
    q = {it: q_raw[it] * lax.rsqrt(q_ss[it] + EPS) * dk ** -0.5 for it in items}
    k = {it: k_raw[it] * lax.rsqrt(k_ss[it] + EPS) for it in items}
    beta = {it: beta_c[tok[it], it[1]:it[1] + 1] for it in items}
    gcol = {it: gcum_c[tok[it], heads + it[1]:heads + it[1] + 1] for it in items}
    grow = {(c, h): gcum_r[c * 2 * heads + heads + h:c * 2 * heads + heads + h + 1, :] for c, h in items}
    glast = {it: gcol[it][CHUNK - 1:CHUNK, :] for it in items}
    decay = {it: jnp.where(tri, jnp.exp(gcol[it] - grow[it]), 0.0) for it in items}
    egc = {it: jnp.exp(gcol[it]) for it in items}
    kb = {it: k[it] * beta[it] for it in items}
    lower = {it: jnp.where(strict, _mm_nt(kb[it], k[it]) * decay[it], 0.0) for it in items}
    rhs = {it: jnp.concatenate([v[it] * beta[it], kb[it] * egc[it]], axis=1).astype(BF16) for it in items}
    qk = {it: jnp.where(tri, _mm_nt(q[it], k[it]) * decay[it], 0.0).astype(BF16) for it in items}
    q_dec = {it: (q[it] * egc[it]).astype(BF16) for it in items}
    k_dec = {it: (k[it] * jnp.exp(glast[it] - gcol[it])).astype(BF16) for it in items}
    carry = {it: jnp.exp(glast[it]) for it in items}

    power = {it: -lower[it] for it in items}
    inv = {it: eye + power[it] for it in items}
    for _ in range(int(math.log2(CHUNK)) - 1):
        for it in items:
            power[it] = _mm(power[it], power[it])
        for it in items:
            inv[it] = inv[it] + _mm(inv[it], power[it])
    sol = {it: _mm(inv[it], rhs[it]) for it in items}

    out = {}
    for c in range(cps):
        state = [s_ref[h] for h in range(heads)]
        v_new = [sol[c, h][:, :dv] - _mm(sol[c, h][:, dv:], state[h]) for h in range(heads)]
        for h in range(heads):
            out[c, h] = _mm(q_dec[c, h], state[h]) + _mm(qk[c, h], v_new[h])
        for h in range(heads):
            s_ref[h] = state[h] * carry[c, h] + _mm_tn(k_dec[c, h], v_new[h])

    mean_sq = {it: jnp.mean(out[it] * out[it], axis=-1, keepdims=True) for it in items}
    for c, h in items:
        normed = out[c, h] * lax.rsqrt(mean_sq[c, h] + EPS) * ng_ref[...]
        gate = _silu(z_ref[tok[c, h], h * dv:(h + 1) * dv].astype(F32))
        o_ref[tok[c, h], h * dv:(h + 1) * dv] = (normed * gate).astype(BF16)


def _gate_rows(gates, count):
    b, l, _ = gates.shape
    return gates[:, :, :count].reshape(b, l // CHUNK, CHUNK, count).transpose(0, 1, 3, 2)


def _lane_row(vec, offset):
    return jnp.zeros((1, LANES), F32).at[0, offset:offset + vec.shape[0]].set(vec)


def _gdn_mixer(zin, gates, conv_w, a_log, dt_bias, norm_g, *, col0, cps=MIXER_CHUNKS_PER_STEP):
    batch, seq, _ = zin.shape
    heads = GDN_HEADS
    qk_w, v_w = heads * GDN_DK, heads * GDN_DV
    assert GDN_DK == GDN_DV == LANES and col0 % qk_w == 0
    c0 = col0 // qk_w
    steps = cps * CHUNK
    zeros = jnp.zeros((heads,), F32)
    al_c, dt_c = _lane_row(a_log, heads), _lane_row(dt_bias, heads)
    al_r = jnp.tile(jnp.concatenate([zeros, a_log]), cps).reshape(cps * 2 * heads, 1)
    dt_r = jnp.tile(jnp.concatenate([zeros, dt_bias]), cps).reshape(cps * 2 * heads, 1)
    tok = lambda blk: pl.BlockSpec((None, steps, qk_w), lambda b, n, blk=blk: (b, n, c0 + blk))
    const = lambda *shape: pl.BlockSpec(shape, lambda b, n: (0,) * len(shape))
    return pl.pallas_call(
        functools.partial(_gdn_kernel, cps=cps),
        grid=(batch, seq // steps),
        in_specs=[
            tok(0), tok(1), tok(2), tok(3),
            pl.BlockSpec((None, steps, LANES), lambda b, n: (b, n, 0)),
            pl.BlockSpec((None, cps, 2 * heads, CHUNK), lambda b, n: (b, n, 0, 0)),
            const(GDN_CONV, 2 * qk_w + v_w),
            const(1, LANES), const(1, LANES), const(cps * 2 * heads, 1), const(cps * 2 * heads, 1),
            const(1, GDN_DV),
        ],
        out_specs=pl.BlockSpec((None, steps, v_w), lambda b, n: (b, n, 0)),
        out_shape=jax.ShapeDtypeStruct((batch, seq, v_w), BF16),
        scratch_shapes=[
            pltpu.VMEM((heads, GDN_DK, GDN_DV), F32),
            pltpu.VMEM((3 * heads, 2 * (SUBLANES + steps), LANES), F32),
        ],
        compiler_params=_params("parallel", "arbitrary"),
        name="gdn",
    )(zin, zin, zin, zin, gates, _gate_rows(gates, 2 * heads), conv_w, al_c, dt_c, al_r, dt_r,
      norm_g.reshape(1, GDN_DV))


def _mlstm_block(q_ref, k_ref, v_ref, op_ref, gc_ref, gr_ref, bc_ref, br_ref, ng_ref, o_ref,
                 c_ref, n_ref, m_ref, *, cps):
    heads, dk, dv = MLSTM_HEADS, MLSTM_DK, MLSTM_DV
    steps = cps * CHUNK

    @pl.when(pl.program_id(1) == 0)
    def _():
        c_ref[...] = jnp.zeros_like(c_ref)
        n_ref[...] = jnp.zeros_like(n_ref)
        m_ref[...] = jnp.zeros_like(m_ref)

    tri, _, tri_u = _tri_masks(CHUNK)
    row = lax.broadcasted_iota(jnp.int32, (steps, steps), 0)
    col = lax.broadcasted_iota(jnp.int32, (steps, steps), 1)
    tri_chunks = (row >= col) & (row // CHUNK == col // CHUNK)
    pre_c = gc_ref[...] + bc_ref[...]
    bcum_c = _cumsum_rows(tri_chunks, -_softplus(-pre_c))
    pre_r = gr_ref[...].reshape(cps * 2 * heads, CHUNK) + br_ref[...]
    bcum_r = _cumsum_lanes(-_softplus(-pre_r), tri_u)

    items = [(c, h) for c in range(cps) for h in range(heads)]
    q_s = {it: q_ref[it[0] * CHUNK:(it[0] + 1) * CHUNK, it[1] * dk:(it[1] + 1) * dk].astype(F32) * dk ** -0.5
           for it in items}
    k_s = {it: k_ref[it[0] * CHUNK:(it[0] + 1) * CHUNK, it[1] * dk:(it[1] + 1) * dk].astype(F32) for it in items}
    v_s = {it: v_ref[it[0] * CHUNK:(it[0] + 1) * CHUNK, it[1] * dv:(it[1] + 1) * dv].astype(BF16) for it in items}
    qk = {it: _mm_nt(q_s[it], k_s[it]) for it in items}
    tok = {(c, h): slice(c * CHUNK, (c + 1) * CHUNK) for c, h in items}
    gl = {(c, h): c * 2 * heads + h for c, h in items}
    ig_col = {it: pre_c[tok[it], it[1]:it[1] + 1] for it in items}
    ig_row = {it: pre_r[gl[it]:gl[it] + 1, :] for it in items}
    b_col = {it: bcum_c[tok[it], heads + it[1]:heads + it[1] + 1] for it in items}
    b_row = {it: bcum_r[gl[it] + heads:gl[it] + heads + 1, :] for it in items}
    b_last = {it: b_col[it][CHUNK - 1:CHUNK, :] for it in items}
    intra_log = {it: jnp.where(tri, b_col[it] - b_row[it] + ig_row[it], -jnp.inf) for it in items}
    intra_max = {it: jnp.max(intra_log[it], axis=-1, keepdims=True) for it in items}
    upd_log = {it: b_last[it] - b_col[it] + ig_col[it] for it in items}
    upd_max = {it: jnp.max(upd_log[it], axis=0, keepdims=True) for it in items}
    gate = {it: dict(b_col=b_col[it], b_last=b_last[it], intra_log=intra_log[it], upd_log=upd_log[it],
                     intra_max=intra_max[it], upd_max=upd_max[it]) for it in items}

    m_run = [m_ref[h][0:1, 0:1] for h in range(heads)]
    for c, h in items:
        g = gate[c, h]
        m_new = jnp.maximum(g["b_last"] + m_run[h], g["upd_max"])
        g.update(m_in=m_run[h], m_out=m_new, carry=jnp.exp(g["b_last"] + m_run[h] - m_new))
        m_run[h] = m_new

    inter_log = {it: gate[it]["b_col"] + gate[it]["m_in"] for it in items}
    m_s = {it: jnp.maximum(inter_log[it], gate[it]["intra_max"]) for it in items}
    inter_w = {it: jnp.exp(inter_log[it] - m_s[it]) for it in items}
    s_mat = {it: qk[it] * jnp.exp(gate[it]["intra_log"] - m_s[it]) for it in items}
    s_v = {it: _mm(s_mat[it], v_s[it]) for it in items}
    s_sum = {it: jnp.sum(s_mat[it], axis=-1, keepdims=True) for it in items}
    kw = {it: k_s[it] * jnp.exp(gate[it]["upd_log"] - gate[it]["m_out"]) for it in items}
    kw_v = {it: _mm_tn(kw[it], v_s[it]) for it in items}
    kw_sum = {it: jnp.sum(kw[it], axis=0, keepdims=True) for it in items}

    c_run = [c_ref[h] for h in range(heads)]
    n_run = [n_ref[h][0:1, :] for h in range(heads)]
    c_in, n_in = {}, {}
    for c, h in items:
        c_in[c, h], n_in[c, h] = c_run[h], n_run[h]
        c_run[h] = gate[c, h]["carry"] * c_run[h] + kw_v[c, h]
        n_run[h] = gate[c, h]["carry"] * n_run[h] + kw_sum[c, h]
    for h in range(heads):
        c_ref[h] = c_run[h]
        n_ref[h] = jnp.broadcast_to(n_run[h], (SUBLANES, dk))
        m_ref[h] = jnp.broadcast_to(m_run[h], (SUBLANES, LANES))
    q_c = {it: _mm(q_s[it], c_in[it]) for it in items}
    den = {it: inter_w[it] * jnp.sum(q_s[it] * n_in[it], axis=-1, keepdims=True) + s_sum[it] for it in items}
    hid = {it: (inter_w[it] * q_c[it] + s_v[it]) / jnp.maximum(jnp.abs(den[it]), jnp.exp(-m_s[it])) for it in items}
    mean_sq = {it: jnp.mean(hid[it] * hid[it], axis=-1, keepdims=True) for it in items}
    for c, h in items:
        rows = slice(c * CHUNK, (c + 1) * CHUNK)
        out_gate = jax.nn.sigmoid(op_ref[rows, h * dv:(h + 1) * dv].astype(F32))
        normed = hid[c, h] * lax.rsqrt(mean_sq[c, h] + EPS) * ng_ref[...]
        o_ref[rows, h * dv:(h + 1) * dv] = (out_gate * normed).astype(BF16)


def _mlstm_operands(zin, gates, gate_bias, norm_g, cps):
    heads = MLSTM_HEADS
    qk_w, v_w = heads * MLSTM_DK, heads * MLSTM_DV
    steps = cps * CHUNK
    bias = gate_bias.reshape(2 * heads)
    tok = lambda w, blk: pl.BlockSpec((None, steps, w), lambda b, n: (b, n, blk))
    const = lambda *shape: pl.BlockSpec(shape, lambda b, n: (0,) * len(shape))
    in_specs = [
        tok(qk_w, 0), tok(qk_w, 1), tok(v_w, 1), tok(v_w, 2),
        pl.BlockSpec((None, steps, LANES), lambda b, n: (b, n, 0)),
        pl.BlockSpec((None, cps, 2 * heads, CHUNK), lambda b, n: (b, n, 0, 0)),
        const(1, LANES), const(cps * 2 * heads, 1), const(1, MLSTM_DV),
    ]
    operands = [zin, zin, zin, zin, gates, _gate_rows(gates, 2 * heads), _lane_row(bias, 0),
                jnp.tile(bias, cps).reshape(cps * 2 * heads, 1), norm_g.reshape(1, MLSTM_DV)]
    scratch = [
        pltpu.VMEM((heads, MLSTM_DK, MLSTM_DV), F32),
        pltpu.VMEM((heads, SUBLANES, MLSTM_DK), F32),
        pltpu.VMEM((heads, SUBLANES, LANES), F32),
    ]
    return in_specs, operands, scratch


def _ret_block(q_ref, k_ref, v_ref, g_ref, pos_ref, freq_ref, dmat_ref, xi_ref, zeta_ref, gam_ref, ng_ref,
               o_ref, s_ref, *, cps):
    heads, dk, dv = RET_HEADS, RET_DK, RET_DV
    half = dk // 2
    steps = cps * CHUNK

    @pl.when(pl.program_id(1) == 0)
    def _():
        s_ref[...] = jnp.zeros_like(s_ref)

    ang = pos_ref[...].reshape(steps, 1).astype(F32) * freq_ref[...]
    cos, sin = jnp.cos(ang), jnp.sin(ang)
    lane = lax.broadcasted_iota(jnp.int32, (steps, dk), 1)
    sin_signed = jnp.where(lane < half, -sin, sin)

    def rotary(x):
        return x * cos + pltpu.roll(x, half, 1) * sin_signed

    q_rot = [rotary(q_ref[:, h * dk:(h + 1) * dk].astype(F32)) * dk ** -0.5 for h in range(heads)]
    k_rot = [rotary(k_ref[:, h * dk:(h + 1) * dk].astype(F32)) for h in range(heads)]
    items = [(c, h) for c in range(cps) for h in range(heads)]
    rows = {c: slice(c * CHUNK, (c + 1) * CHUNK) for c in range(cps)}
    q_s = {(c, h): q_rot[h][rows[c]].astype(BF16) for c, h in items}
    k_s = {(c, h): k_rot[h][rows[c]] for c, h in items}
    v_s = {(c, h): v_ref[rows[c], h * dv:(h + 1) * dv].astype(BF16) for c, h in items}
    qk = {it: _mm_nt(q_s[it], k_s[it]) * dmat_ref[it[1]] for it in items}
    intra = {it: _mm(qk[it], v_s[it]) for it in items}
    k_v = {it: _mm_tn(k_s[it] * zeta_ref[it[1]], v_s[it]) for it in items}

    s_run = [s_ref[h] for h in range(heads)]
    s_in = {}
    for c, h in items:
        s_in[c, h] = s_run[h]
        s_run[h] = s_run[h] * gam_ref[h] + k_v[c, h]
    for h in range(heads):
        s_ref[h] = s_run[h]
    inter = {it: _mm(q_s[it], s_in[it]) for it in items}
    y = {it: intra[it] + inter[it] * xi_ref[it[1]] for it in items}
    centred = {it: y[it] - jnp.mean(y[it], axis=-1, keepdims=True) for it in items}
    var = {it: jnp.mean(centred[it] * centred[it], axis=-1, keepdims=True) for it in items}
    for c, h in items:
        normed = centred[c, h] * lax.rsqrt(var[c, h] + EPS) * ng_ref[...]
        gate = _silu(g_ref[rows[c], h * dv:(h + 1) * dv].astype(F32))
        o_ref[rows[c], h * dv:(h + 1) * dv] = (normed * gate).astype(BF16)


def _ret_operands(zin, positions, norm_g, col0, cps):
    batch, seq, _ = zin.shape
    heads, dk = RET_HEADS, RET_DK
    qk_w, v_w = heads * RET_DK, heads * RET_DV
    assert col0 % v_w == 0
    cq, cv = col0 // qk_w, col0 // v_w
    half = dk // 2
    steps = cps * CHUNK
    inv_freq = ROPE_BASE ** (-jnp.arange(half, dtype=F32) / half)
    freq = jnp.concatenate([inv_freq, inv_freq]).reshape(1, dk)
    log_gamma = jnp.log1p(-jnp.exp2(-5.0 - jnp.arange(heads, dtype=F32)))
    idx = jnp.arange(CHUNK, dtype=F32)
    tri = jnp.tril(jnp.ones((CHUNK, CHUNK), dtype=bool))
    diff = jnp.where(tri, idx[:, None] - idx[None, :], 0.0)
    dmat = jnp.where(tri, jnp.exp(diff * log_gamma[:, None, None]), 0.0)
    xi = jnp.exp((idx + 1.0) * log_gamma[:, None])[:, :, None]
    zeta = jnp.exp((CHUNK - 1.0 - idx) * log_gamma[:, None])[:, :, None]
    gamma_c = jnp.broadcast_to(jnp.exp(CHUNK * log_gamma)[:, None, None], (heads, 1, RET_DV))
    pos = positions.reshape(batch, seq // CHUNK, CHUNK, 1)
    tok = lambda w, blk: pl.BlockSpec((None, steps, w), lambda b, n: (b, n, blk))
    const = lambda *shape: pl.BlockSpec(shape, lambda b, n: (0,) * len(shape))
    in_specs = [
        tok(qk_w, cq), tok(qk_w, cq + 1), tok(v_w, cv + 1), tok(v_w, cv + 2),
        pl.BlockSpec((None, cps, CHUNK, 1), lambda b, n: (b, n, 0, 0)),
        const(1, dk), const(heads, CHUNK, CHUNK), const(heads, CHUNK, 1), const(heads, CHUNK, 1),
        const(heads, 1, RET_DV), const(1, RET_DV),
    ]
    operands = [zin, zin, zin, zin, pos, freq, dmat, xi, zeta, gamma_c, norm_g.reshape(1, RET_DV)]
    return in_specs, operands, [pltpu.VMEM((heads, RET_DK, RET_DV), F32)]


def _odd_kernel(*refs, n_mlstm, n_ret, cps):
    mlstm_in, ret_in = refs[:n_mlstm], refs[n_mlstm:n_mlstm + n_ret]
    yc_ref, yd_ref, c_ref, n_ref, m_ref, s_ref = refs[n_mlstm + n_ret:]
    _ret_block(*ret_in, yd_ref, s_ref, cps=cps)
    _mlstm_block(*mlstm_in, yc_ref, c_ref, n_ref, m_ref, cps=cps)


def _odd_mixers(zin, gates, positions, gate_bias, mlstm_g, ret_g, *, ret_col0, cps=MIXER_CHUNKS_PER_STEP):
    batch, seq, _ = zin.shape
    steps = cps * CHUNK
    m_specs, m_ops, m_scratch = _mlstm_operands(zin, gates, gate_bias, mlstm_g, cps)
    r_specs, r_ops, r_scratch = _ret_operands(zin, positions, ret_g, ret_col0, cps)
    widths = (MLSTM_HEADS * MLSTM_DV, RET_HEADS * RET_DV)
    return pl.pallas_call(
        functools.partial(_odd_kernel, n_mlstm=len(m_ops), n_ret=len(r_ops), cps=cps),
        grid=(batch, seq // steps),
        in_specs=m_specs + r_specs,
        out_specs=[pl.BlockSpec((None, steps, w), lambda b, n: (b, n, 0)) for w in widths],
        out_shape=[jax.ShapeDtypeStruct((batch, seq, w), BF16) for w in widths],
        scratch_shapes=m_scratch + r_scratch,
        compiler_params=_params("parallel", "arbitrary"),
        name="mlstm_retention",
    )(*m_ops, *r_ops)


def _pad_rows(w):
    return jnp.pad(w, ((0, LANES - w.shape[0]), (0, 0)))


def _even_mixer(x, batch, norm_g, w_in, w_out, a_re, a_im, log_step, b_re, b_im, c_re, c_im, d_skip, w_glu,
                b_glu, conv_w, a_log, dt_bias, gdn_g):
    s5_w = a_re.shape[0] * S5_GROUP
    main_w = s5_w + GDN_HEADS * (2 * GDN_DK + 2 * GDN_DV)
    w_t = w_in.T
    w_gate = _pad_rows(w_t[main_w:]).astype(BF16)
    zin, gates = _inproj(x, norm_g, w_t, [(0, main_w)], w_gate)
    zin = zin.reshape(batch, -1, main_w)
    gates = gates.reshape(batch, -1, LANES)
    ya = _s5_mixer(zin, a_re, a_im, log_step, b_re, b_im, c_re, c_im, d_skip, w_glu, b_glu)
    yb = _gdn_mixer(zin, gates, conv_w, a_log, dt_bias, gdn_g, col0=s5_w)
    m = x.shape[0]
    return _outproj(x, ya.reshape(m, -1), yb.reshape(m, -1), w_out)


def _odd_mixer(x, batch, positions, norm_g, w_in, w_out, gate_bias, mlstm_g, ret_g):
    c_main = MLSTM_HEADS * (2 * MLSTM_DK + 2 * MLSTM_DV)
    r_main = RET_HEADS * (2 * RET_DK + 2 * RET_DV)
    n_gate = 2 * MLSTM_HEADS
    w_t = w_in.T
    w_gate = _pad_rows(w_t[c_main:c_main + n_gate]).astype(BF16)
    zin, gates = _inproj(x, norm_g, w_t, [(0, c_main), (c_main + n_gate, r_main)], w_gate)
    zin = zin.reshape(batch, -1, c_main + r_main)
    gates = gates.reshape(batch, -1, LANES)
    yc, yd = _odd_mixers(zin, gates, positions, gate_bias, mlstm_g, ret_g, ret_col0=c_main)
    m = x.shape[0]
    return _outproj(x, yc.reshape(m, -1), yd.reshape(m, -1), w_out)


def kernel(x, positions, ffn_norm, ffn_w1, ffn_w3, ffn_w2, mix_norm, even_w_in, even_w_out, s5_a_re, s5_a_im, s5_log_step, s5_b_re, s5_b_im, s5_c_re, s5_c_im, s5_d, s5_w_glu, s5_b_glu, gdn_conv_w, gdn_a_log, gdn_dt_bias, gdn_norm, odd_w_in, odd_w_out, mlstm_gate_bias, mlstm_norm, ret_norm, final_norm):
    batch, seq, d = x.shape
    depth = ffn_norm.shape[0]
    x = x.reshape(batch * seq, d)

    def ffn(x, layer, which, final_g=None):
        return _ffn(x, ffn_norm[layer, which], ffn_w1, ffn_w3, ffn_w2, layer, which, final_g)

    for layer in range(depth):
        x = ffn(x, layer, 0)
        j = layer // 2
        if layer % 2 == 0:
            x = _even_mixer(x, batch, mix_norm[layer], even_w_in[j], even_w_out[j], s5_a_re[j], s5_a_im[j],
                            s5_log_step[j], s5_b_re[j], s5_b_im[j], s5_c_re[j], s5_c_im[j], s5_d[j].reshape(-1),
                            s5_w_glu[j], s5_b_glu[j], gdn_conv_w[j], gdn_a_log[j], gdn_dt_bias[j], gdn_norm[j])
        else:
            x = _odd_mixer(x, batch, positions, mix_norm[layer], odd_w_in[j], odd_w_out[j], mlstm_gate_bias[j],
                           mlstm_norm[j], ret_norm[j])
        x = ffn(x, layer, 1, final_norm if layer == depth - 1 else None)
    return x.reshape(batch, seq, d)
```

```python
import functools
import math

import jax
import jax.numpy as jnp
from jax import lax
from jax.experimental import pallas as pl
from jax.experimental.pallas import tpu as pltpu

F32 = jnp.float32
BF16 = jnp.bfloat16

EPS = 1e-6
CHUNK = 64
ROPE_BASE = 10000.0
LANES = 128
SUBLANES = 8
VMEM_LIMIT_BYTES = 60 * 1024 * 1024

S5_GROUP = 16
S5_TILE_GROUPS = LANES // S5_GROUP
S5_PAIRS = S5_TILE_GROUPS // 2
S5_TIME = 256
S5_PITCH = S5_TIME + 4
S5_SCAN_SLICE = 32

GDN_HEADS = 8
GDN_DK = 128
GDN_DV = 128
GDN_CONV = 4
MIXER_CHUNKS_PER_STEP = 4
MLSTM_HEADS = 4
MLSTM_DK = 128
MLSTM_DV = 256
RET_HEADS = 4
RET_DK = 128
RET_DV = 256


def _params(*semantics):
    return pltpu.CompilerParams(dimension_semantics=semantics, vmem_limit_bytes=VMEM_LIMIT_BYTES)


def _mm(a, b):
    return jnp.dot(a.astype(BF16), b.astype(BF16), preferred_element_type=F32)


def _mm_nt(a, b):
    return lax.dot_general(a.astype(BF16), b.astype(BF16), (((1,), (1,)), ((), ())), preferred_element_type=F32)


def _mm_tn(a, b):
    return lax.dot_general(a.astype(BF16), b.astype(BF16), (((0,), (0,)), ((), ())), preferred_element_type=F32)


def _split3(x):
    x1 = x.astype(BF16)
    r1 = x - x1.astype(F32)
    x2 = r1.astype(BF16)
    x3 = (r1 - x2.astype(F32)).astype(BF16)
    return x1, x2, x3


def _cumsum_rows(tri_lower, x):
    t = tri_lower.astype(BF16)
    return sum(jnp.dot(t, p, preferred_element_type=F32) for p in _split3(x))


def _cumsum_lanes(x, tri_upper):
    t = tri_upper.astype(BF16)
    return sum(jnp.dot(p, t, preferred_element_type=F32) for p in _split3(x))


def _rms_norm(x, g):
    return x * lax.rsqrt(jnp.mean(x * x, axis=-1, keepdims=True) + EPS) * g


def _silu(x):
    return x * jax.nn.sigmoid(x)


def _softplus(x):
    return jnp.maximum(x, 0.0) + jnp.log(1.0 + jnp.exp(-jnp.abs(x)))


def _tri_masks(n):
    r = lax.broadcasted_iota(jnp.int32, (n, n), 0)
    c = lax.broadcasted_iota(jnp.int32, (n, n), 1)
    return r >= c, r > c, r <= c


FFN_ROW_CHUNKS = 4


def _ffn_kernel(x_ref, g_ref, w1_ref, w3_ref, w2_ref, fg_ref, o_ref, h_ref, *, final_norm):
    j = pl.program_id(1)

    def half_update(h):
        a = jnp.dot(h, w1_ref[...].astype(BF16), preferred_element_type=F32)
        b = jnp.dot(h, w3_ref[...].astype(BF16), preferred_element_type=F32)
        return 0.5 * jnp.dot((_silu(a) * b).astype(BF16), w2_ref[...].astype(BF16), preferred_element_type=F32)

    @pl.when(j == 0)
    def _():
        chunk = x_ref.shape[0] // FFN_ROW_CHUNKS
        for c in range(FFN_ROW_CHUNKS):
            rows = pl.ds(c * chunk, chunk)
            x = x_ref[rows, :]
            h = _rms_norm(x, g_ref[...]).astype(BF16)
            h_ref[rows, :] = h
            o_ref[rows, :] = x + half_update(h)

    @pl.when(j > 0)
    def _():
        o_ref[...] += half_update(h_ref[...])

    if final_norm:
        @pl.when(j == pl.num_programs(1) - 1)
        def _():
            o_ref[...] = _rms_norm(o_ref[...], fg_ref[...])


def _ffn(x, g, w1, w3, w2, layer, which, final_g=None, *, tm=1024, tf=256):
    m, d = x.shape
    f = w1.shape[-1]
    final_norm = final_g is not None
    fg = final_g if final_norm else g
    return pl.pallas_call(
        functools.partial(_ffn_kernel, final_norm=final_norm),
        grid=(m // tm, f // tf),
        in_specs=[
            pl.BlockSpec((tm, d), lambda i, j: (i, 0)),
            pl.BlockSpec((1, d), lambda i, j: (0, 0)),
            pl.BlockSpec((None, None, d, tf), lambda i, j: (layer, which, 0, j)),
            pl.BlockSpec((None, None, d, tf), lambda i, j: (layer, which, 0, j)),
            pl.BlockSpec((None, None, tf, d), lambda i, j: (layer, which, j, 0)),
            pl.BlockSpec((1, d), lambda i, j: (0, 0)),
        ],
        out_specs=pl.BlockSpec((tm, d), lambda i, j: (i, 0)),
        out_shape=jax.ShapeDtypeStruct((m, d), F32),
        scratch_shapes=[pltpu.VMEM((tm, d), BF16)],
        compiler_params=_params("parallel", "arbitrary"),
        name="ffn",
    )(x, g.reshape(1, d), w1, w3, w2, fg.reshape(1, d))


def _inproj_kernel(x_ref, g_ref, w_ref, wg_ref, o_ref, og_ref, h_ref, wbf_ref):
    i, j = pl.program_id(0), pl.program_id(1)

    @pl.when(j == 0)
    def _():
        h = _rms_norm(x_ref[...], g_ref[...]).astype(BF16)
        h_ref[...] = h
        og_ref[...] = _mm_nt(h, wg_ref[...])

    @pl.when(i == 0)
    def _():
        wbf_ref[j] = w_ref[...].astype(BF16)

    o_ref[...] = _mm_nt(h_ref[...], wbf_ref[j]).astype(BF16)


def _inproj(x, g, w_t, segments, w_gate, *, tm=1024, tn=512):
    m, d = x.shape
    starts = []
    for row0, width in segments:
        assert row0 % SUBLANES == 0 and width % tn == 0
        starts += [row0 + tn * k for k in range(width // tn)]
    nj = len(starts)

    def block_row(j):
        row, base = 0, 0
        for row0, width in segments:
            count = width // tn
            row = row + jnp.where((j >= base) & (j < base + count), row0 + tn * (j - base), 0)
            base += count
        return row

    def w_index(i, j):
        return pl.multiple_of(block_row(jnp.where(i == 0, j, nj - 1)), SUBLANES), 0

    return pl.pallas_call(
        _inproj_kernel,
        grid=(m // tm, nj),
        in_specs=[
            pl.BlockSpec((tm, d), lambda i, j: (i, 0)),
            pl.BlockSpec((1, d), lambda i, j: (0, 0)),
            pl.BlockSpec((pl.Element(tn), pl.Element(d)), w_index),
            pl.BlockSpec((LANES, d), lambda i, j: (0, 0)),
        ],
        out_specs=[
            pl.BlockSpec((tm, tn), lambda i, j: (i, j)),
            pl.BlockSpec((tm, LANES), lambda i, j: (i, 0)),
        ],
        out_shape=[jax.ShapeDtypeStruct((m, nj * tn), BF16), jax.ShapeDtypeStruct((m, LANES), F32)],
        scratch_shapes=[pltpu.VMEM((tm, d), BF16), pltpu.VMEM((nj, tn, d), BF16)],
        compiler_params=_params("arbitrary", "arbitrary"),
        name="inproj",
    )(x, g.reshape(1, d), w_t, w_gate)


def _outproj_kernel(x_ref, ya_ref, yb_ref, w_ref, o_ref, wbf_ref):
    @pl.when(pl.program_id(1) == 0)
    def _():
        wbf_ref[...] = w_ref[...].astype(BF16)

    ka = ya_ref.shape[1]
    o_ref[...] = (x_ref[...] + jnp.dot(ya_ref[...], wbf_ref[:ka, :], preferred_element_type=F32)
                  + jnp.dot(yb_ref[...], wbf_ref[ka:, :], preferred_element_type=F32))


def _outproj(x, ya, yb, w, *, tm=1024, tn=1024):
    m, d = x.shape
    ka, kb = ya.shape[1], yb.shape[1]
    return pl.pallas_call(
        _outproj_kernel,
        grid=(d // tn, m // tm),
        in_specs=[
            pl.BlockSpec((tm, tn), lambda j, i: (i, j)),
            pl.BlockSpec((tm, ka), lambda j, i: (i, 0)),
            pl.BlockSpec((tm, kb), lambda j, i: (i, 0)),
            pl.BlockSpec((ka + kb, tn), lambda j, i: (0, j)),
        ],
        out_specs=pl.BlockSpec((tm, tn), lambda j, i: (i, j)),
        out_shape=jax.ShapeDtypeStruct((m, d), F32),
        scratch_shapes=[pltpu.VMEM((ka + kb, tn), BF16)],
        compiler_params=_params("parallel", "arbitrary"),
        name="outproj",
    )(x, ya, yb, w)


def _s5_disc_kernel(are_ref, aim_ref, step_ref, bre_ref, bim_ref, lr_ref, li_ref, bbr_ref, bbi_ref):
    ar, ai = are_ref[...], aim_ref[...]
    step = jnp.exp(step_ref[...])
    mag = jnp.exp(ar * step)
    lr, li = mag * jnp.cos(ai * step), mag * jnp.sin(ai * step)
    den = ar * ar + ai * ai
    fr = ((lr - 1.0) * ar + li * ai) / den
    fi = (li * ar - (lr - 1.0) * ai) / den
    lr_ref[...] = lr
    li_ref[...] = li
    br, bi = bre_ref[...], bim_ref[...]
    bbr_ref[...] = fr[None] * br - fi[None] * bi
    bbi_ref[...] = fr[None] * bi + fi[None] * br


def _s5_discretise(a_re, a_im, log_step, b_re, b_im):
    g, p = a_re.shape
    h = b_re.shape[-1]
    sd = jax.ShapeDtypeStruct
    return pl.pallas_call(
        _s5_disc_kernel,
        out_shape=[sd((g, p), F32), sd((g, p), F32), sd((h, g, p), F32), sd((h, g, p), F32)],
        name="s5_discretise",
    )(a_re, a_im, log_step.reshape(g, 1), b_re.transpose(2, 0, 1), b_im.transpose(2, 0, 1))


def _s5_kernel(u_ref, wb_ref, lr_ref, li_ref, cs_ref, dsk_ref, wglu_ref, bglu_ref, o_ref,
               rre_ref, rim_ref, xr_ref, xi_ref, y_ref, *, batch, tiles):
    per = tiles * S5_PAIRS

    @pl.when(pl.program_id(0) == 0)
    def _():
        xr_ref[...] = jnp.zeros_like(xr_ref)
        xi_ref[...] = jnp.zeros_like(xi_ref)

    def chain_rows(j, pair, b):
        return pl.ds((b * per + j * S5_PAIRS + pair) * S5_PITCH, S5_TIME)

    def drive(b):
        for j in range(tiles):
            d = jnp.dot(u_ref[b, :, j * LANES:(j + 1) * LANES].astype(BF16), wb_ref[j], preferred_element_type=F32)
            for pair in range(S5_PAIRS):
                rre_ref[chain_rows(j, pair, b), :] = d[:, pair * LANES:(pair + 1) * LANES]
                rim_ref[chain_rows(j, pair, b), :] = d[:, (S5_PAIRS + pair) * LANES:(S5_PAIRS + pair + 1) * LANES]
            yield

    def scan(b):
        lr, li = lr_ref[...], li_ref[...]
        own = pl.ds(b * per, per)
        xr, xi = xr_ref[own, :], xi_ref[own, :]
        for t in range(S5_TIME):
            rows = pl.ds(b * per * S5_PITCH + t, per, stride=S5_PITCH)
            xr, xi = lr * xr - li * xi + rre_ref[rows, :], lr * xi + li * xr + rim_ref[rows, :]
            rre_ref[rows, :] = xr
            rim_ref[rows, :] = xi
            if t % S5_SCAN_SLICE == S5_SCAN_SLICE - 1:
                yield
        xr_ref[own, :] = xr
        xi_ref[own, :] = xi

    def readout(b):
        for j in range(tiles):
            states = jnp.concatenate(
                [rre_ref[chain_rows(j, pair, b), :].astype(BF16) for pair in range(S5_PAIRS)]
                + [rim_ref[chain_rows(j, pair, b), :].astype(BF16) for pair in range(S5_PAIRS)], axis=1)
            cols = slice(j * LANES, (j + 1) * LANES)
            y = jnp.dot(states, cs_ref[j], preferred_element_type=F32) + dsk_ref[:, cols] * u_ref[b, :, cols].astype(F32)
            y_ref[b, :, cols] = jax.nn.gelu(y)
            yield
        y = y_ref[b]
        gate = jnp.dot(y.astype(BF16), wglu_ref[...], preferred_element_type=F32) + bglu_ref[...]
        o_ref[b] = (y * jax.nn.sigmoid(gate)).astype(BF16)
        yield

    def alongside(vector_work, matrix_work):
        done = object()
        while next(vector_work, done) is not done:
            next(matrix_work, done)
        for _ in matrix_work:
            pass

    for _ in drive(0):
        pass
    for b in range(batch):
        nxt = drive(b + 1) if b + 1 < batch else iter(())
        prev = readout(b - 1) if b > 0 else iter(())
        alongside(scan(b), (x for gen in (prev, nxt) for x in gen))
    for _ in readout(batch - 1):
        pass


def _s5_mixer(zin, a_re, a_im, log_step, b_re, b_im, c_re, c_im, d_skip, w_glu, b_glu):
    batch, seq, _ = zin.shape
    groups, states = a_re.shape
    width = groups * S5_GROUP
    tiles = width // LANES
    chains = tiles * S5_PAIRS * batch
    lr, li, bbr, bbi = _s5_discretise(a_re, a_im, log_step, b_re, b_im)

    eye = jnp.eye(S5_TILE_GROUPS, dtype=F32)
    bb = jnp.stack([bbr, bbi]).reshape(2, S5_GROUP, tiles, S5_TILE_GROUPS, states)
    wb = jnp.einsum("rhjgp,gk->jghrkp", bb, eye).reshape(tiles, LANES, 2 * S5_TILE_GROUPS * states).astype(BF16)
    cc = jnp.stack([c_re, -c_im]).reshape(2, tiles, S5_TILE_GROUPS, S5_GROUP, states)
    cs = jnp.einsum("rjghp,gk->jrkpgh", cc, eye).reshape(tiles, 2 * S5_TILE_GROUPS * states, LANES).astype(BF16)

    def per_chain(lam):
        return lam.reshape(tiles * S5_PAIRS, 2 * states)

    const = lambda *shape: pl.BlockSpec(shape, lambda i: (0,) * len(shape))
    return pl.pallas_call(
        functools.partial(_s5_kernel, batch=batch, tiles=tiles),
        grid=(seq // S5_TIME,),
        in_specs=[
            pl.BlockSpec((batch, S5_TIME, width), lambda i: (0, i, 0)),
            const(tiles, LANES, 2 * S5_TILE_GROUPS * states),
            const(chains // batch, LANES),
            const(chains // batch, LANES),
            const(tiles, 2 * S5_TILE_GROUPS * states, LANES),
            const(1, width),
            const(width, width),
            const(1, width),
        ],
        out_specs=pl.BlockSpec((batch, S5_TIME, width), lambda i: (0, i, 0)),
        out_shape=jax.ShapeDtypeStruct((batch, seq, width), BF16),
        scratch_shapes=[
            pltpu.VMEM((chains * S5_PITCH, LANES), F32),
            pltpu.VMEM((chains * S5_PITCH, LANES), F32),
            pltpu.VMEM((chains, LANES), F32),
            pltpu.VMEM((chains, LANES), F32),
            pltpu.VMEM((batch, S5_TIME, width), F32),
        ],
        compiler_params=_params("arbitrary"),
        name="s5",
    )(zin, wb, per_chain(lr), per_chain(li), cs, d_skip.reshape(1, width), w_glu.astype(BF16),
      b_glu.reshape(1, width))


def _gdn_kernel(q_ref, k_ref, v_ref, z_ref, gc_ref, gr_ref, cw_ref, alc_ref, dtc_ref, alr_ref, dtr_ref,
                ng_ref, o_ref, s_ref, tail_ref, *, cps):
    heads, dk, dv = GDN_HEADS, GDN_DK, GDN_DV
    steps = cps * CHUNK

    @pl.when(pl.program_id(1) == 0)
    def _():
        s_ref[...] = jnp.zeros_like(s_ref)
        tail_ref[:, :2 * SUBLANES, :] = jnp.zeros((tail_ref.shape[0], 2 * SUBLANES, LANES), F32)

    tri, strict, tri_u = _tri_masks(CHUNK)
    row = lax.broadcasted_iota(jnp.int32, (steps, steps), 0)
    col = lax.broadcasted_iota(jnp.int32, (steps, steps), 1)
    tri_chunks = (row >= col) & (row // CHUNK == col // CHUNK)
    eye = (lax.broadcasted_iota(jnp.int32, (CHUNK, CHUNK), 0)
           == lax.broadcasted_iota(jnp.int32, (CHUNK, CHUNK), 1)).astype(F32)

    def time_rows(first, count):
        return pl.ds(2 * first, count, stride=2)

    conv = []
    for idx, ref in enumerate((q_ref, k_ref, v_ref)):
        slabs = []
        for h in range(ref.shape[1] // LANES):
            lanes = slice(h * LANES, (h + 1) * LANES)
            cur = ref[:, lanes].astype(F32)
            hist = tail_ref.at[idx * heads + h]
            hist[time_rows(SUBLANES, steps), :] = cur
            cw = cw_ref[:, idx * ref.shape[1] + h * LANES:idx * ref.shape[1] + (h + 1) * LANES]
            acc = cur * cw[GDN_CONV - 1:GDN_CONV]
            for s in range(1, GDN_CONV):
                acc = acc + hist[time_rows(SUBLANES - s, steps), :] * cw[GDN_CONV - 1 - s:GDN_CONV - s]
            hist[time_rows(0, SUBLANES), :] = cur[steps - SUBLANES:]
            slabs.append(_silu(acc))
        conv.append(slabs)
    q_all, k_all, v_all = conv

    gc = gc_ref[...]
    beta_c = jax.nn.sigmoid(gc)
    g_c = -jnp.exp(alc_ref[...]) * _softplus(gc + dtc_ref[...])
    gcum_c = _cumsum_rows(tri_chunks, g_c)
    gr = gr_ref[...].reshape(cps * 2 * heads, CHUNK)
    g_r = -jnp.exp(alr_ref[...]) * _softplus(gr + dtr_ref[...])
    gcum_r = _cumsum_lanes(g_r, tri_u)

    items = [(c, h) for c in range(cps) for h in range(heads)]
    tok = {(c, h): slice(c * CHUNK, (c + 1) * CHUNK) for c, h in items}
    q_raw = {it: q_all[it[1]][tok[it]] for it in items}
    k_raw = {it: k_all[it[1]][tok[it]] for it in items}
    v = {it: v_all[it[1]][tok[it]] for it in items}
    q_ss = {it: jnp.sum(q_raw[it] * q_raw[it], axis=-1, keepdims=True) for it in items}
    k_ss = {it: jnp.sum(k_raw[it] * k_raw[it], axis=-1, keepdims=True) for it in items}
    q = {it: q_raw[it] * lax.rsqrt(q_ss[it] + EPS) * dk ** -0.5 for it in items}
    k = {it: k_raw[it] * lax.rsqrt(k_ss[it] + EPS) for it in items}
    beta = {it: beta_c[tok[it], it[1]:it[1] + 1] for it in items}
    gcol = {it: gcum_c[tok[it], heads + it[1]:heads + it[1] + 1] for it in items}
    grow = {(c, h): gcum_r[c * 2 * heads + heads + h:c * 2 * heads + heads + h + 1, :] for c, h in items}
    glast = {it: gcol[it][CHUNK - 1:CHUNK, :] for it in items}
    decay = {it: jnp.where(tri, jnp.exp(gcol[it] - grow[it]), 0.0) for it in items}
    egc = {it: jnp.exp(gcol[it]) for it in items}
    kb = {it: k[it] * beta[it] for it in items}
    lower = {it: jnp.where(strict, _mm_nt(kb[it], k[it]) * decay[it], 0.0) for it in items}
    rhs = {it: jnp.concatenate([v[it] * beta[it], kb[it] * egc[it]], axis=1).astype(BF16) for it in items}
    qk = {it: jnp.where(tri, _mm_nt(q[it], k[it]) * decay[it], 0.0).astype(BF16) for it in items}
    q_dec = {it: (q[it] * egc[it]).astype(BF16) for it in items}
    k_dec = {it: (k[it] * jnp.exp(glast[it] - gcol[it])).astype(BF16) for it in items}
    carry = {it: jnp.exp(glast[it]) for it in items}

    power = {it: -lower[it] for it in items}
    inv = {it: eye + power[it] for it in items}
    for _ in range(int(math.log2(CHUNK)) - 1):
        for it in items:
            power[it] = _mm(power[it], power[it])
        for it in items:
            inv[it] = inv[it] + _mm(inv[it], power[it])
    sol = {it: _mm(inv[it], rhs[it]) for it in items}

    out = {}
    for c in range(cps):
        state = [s_ref[h] for h in range(heads)]
        v_new = [sol[c, h][:, :dv] - _mm(sol[c, h][:, dv:], state[h]) for h in range(heads)]
        for h in range(heads):
            out[c, h] = _mm(q_dec[c, h], state[h]) + _mm(qk[c, h], v_new[h])
        for h in range(heads):
            s_ref[h] = state[h] * carry[c, h] + _mm_tn(k_dec[c, h], v_new[h])

    mean_sq = {it: jnp.mean(out[it] * out[it], axis=-1, keepdims=True) for it in items}
    for c, h in items:
        normed = out[c, h] * lax.rsqrt(mean_sq[c, h] + EPS) * ng_ref[...]
        gate = _silu(z_ref[tok[c, h], h * dv:(h + 1) * dv].astype(F32))
        o_ref[tok[c, h], h * dv:(h + 1) * dv] = (normed * gate).astype(BF16)


def _gate_rows(gates, count):
    b, l, _ = gates.shape
    return gates[:, :, :count].reshape(b, l // CHUNK, CHUNK, count).transpose(0, 1, 3, 2)


def _lane_row(vec, offset):
    return jnp.zeros((1, LANES), F32).at[0, offset:offset + vec.shape[0]].set(vec)


def _gdn_mixer(zin, gates, conv_w, a_log, dt_bias, norm_g, *, col0, cps=MIXER_CHUNKS_PER_STEP):
    batch, seq, _ = zin.shape
    heads = GDN_HEADS
    qk_w, v_w = heads * GDN_DK, heads * GDN_DV
    assert GDN_DK == GDN_DV == LANES and col0 % qk_w == 0
    c0 = col0 // qk_w
    steps = cps * CHUNK
    zeros = jnp.zeros((heads,), F32)
    al_c, dt_c = _lane_row(a_log, heads), _lane_row(dt_bias, heads)
    al_r = jnp.tile(jnp.concatenate([zeros, a_log]), cps).reshape(cps * 2 * heads, 1)
    dt_r = jnp.tile(jnp.concatenate([zeros, dt_bias]), cps).reshape(cps * 2 * heads, 1)
    tok = lambda blk: pl.BlockSpec((None, steps, qk_w), lambda b, n, blk=blk: (b, n, c0 + blk))
    const = lambda *shape: pl.BlockSpec(shape, lambda b, n: (0,) * len(shape))
    return pl.pallas_call(
        functools.partial(_gdn_kernel, cps=cps),
        grid=(batch, seq // steps),
        in_specs=[
            tok(0), tok(1), tok(2), tok(3),
            pl.BlockSpec((None, steps, LANES), lambda b, n: (b, n, 0)),
            pl.BlockSpec((None, cps, 2 * heads, CHUNK), lambda b, n: (b, n, 0, 0)),
            const(GDN_CONV, 2 * qk_w + v_w),
            const(1, LANES), const(1, LANES), const(cps * 2 * heads, 1), const(cps * 2 * heads, 1),
            const(1, GDN_DV),
        ],
        out_specs=pl.BlockSpec((None, steps, v_w), lambda b, n: (b, n, 0)),
        out_shape=jax.ShapeDtypeStruct((batch, seq, v_w), BF16),
        scratch_shapes=[
            pltpu.VMEM((heads, GDN_DK, GDN_DV), F32),
            pltpu.VMEM((3 * heads, 2 * (SUBLANES + steps), LANES), F32),
        ],
        compiler_params=_params("parallel", "arbitrary"),
        name="gdn",
    )(zin, zin, zin, zin, gates, _gate_rows(gates, 2 * heads), conv_w, al_c, dt_c, al_r, dt_r,
      norm_g.reshape(1, GDN_DV))


def _mlstm_block(q_ref, k_ref, v_ref, op_ref, gc_ref, gr_ref, bc_ref, br_ref, ng_ref, o_ref,
                 c_ref, n_ref, m_ref, *, cps):
    heads, dk, dv = MLSTM_HEADS, MLSTM_DK, MLSTM_DV
    steps = cps * CHUNK

    @pl.when(pl.program_id(1) == 0)
    def _():
        c_ref[...] = jnp.zeros_like(c_ref)
        n_ref[...] = jnp.zeros_like(n_ref)
        m_ref[...] = jnp.zeros_like(m_ref)

    tri, _, tri_u = _tri_masks(CHUNK)
    row = lax.broadcasted_iota(jnp.int32, (steps, steps), 0)
    col = lax.broadcasted_iota(jnp.int32, (steps, steps), 1)
    tri_chunks = (row >= col) & (row // CHUNK == col // CHUNK)
    pre_c = gc_ref[...] + bc_ref[...]
    bcum_c = _cumsum_rows(tri_chunks, -_softplus(-pre_c))
    pre_r = gr_ref[...].reshape(cps * 2 * heads, CHUNK) + br_ref[...]
    bcum_r = _cumsum_lanes(-_softplus(-pre_r), tri_u)

    items = [(c, h) for c in range(cps) for h in range(heads)]
    q_s = {it: q_ref[it[0] * CHUNK:(it[0] + 1) * CHUNK, it[1] * dk:(it[1] + 1) * dk].astype(F32) * dk ** -0.5
           for it in items}
    k_s = {it: k_ref[it[0] * CHUNK:(it[0] + 1) * CHUNK, it[1] * dk:(it[1] + 1) * dk].astype(F32) for it in items}
    v_s = {it: v_ref[it[0] * CHUNK:(it[0] + 1) * CHUNK, it[1] * dv:(it[1] + 1) * dv].astype(BF16) for it in items}
    qk = {it: _mm_nt(q_s[it], k_s[it]) for it in items}
    tok = {(c, h): slice(c * CHUNK, (c + 1) * CHUNK) for c, h in items}
    gl = {(c, h): c * 2 * heads + h for c, h in items}
    ig_col = {it: pre_c[tok[it], it[1]:it[1] + 1] for it in items}
    ig_row = {it: pre_r[gl[it]:gl[it] + 1, :] for it in items}
    b_col = {it: bcum_c[tok[it], heads + it[1]:heads + it[1] + 1] for it in items}
    b_row = {it: bcum_r[gl[it] + heads:gl[it] + heads + 1, :] for it in items}
    b_last = {it: b_col[it][CHUNK - 1:CHUNK, :] for it in items}
    intra_log = {it: jnp.where(tri, b_col[it] - b_row[it] + ig_row[it], -jnp.inf) for it in items}
    intra_max = {it: jnp.max(intra_log[it], axis=-1, keepdims=True) for it in items}
    upd_log = {it: b_last[it] - b_col[it] + ig_col[it] for it in items}
    upd_max = {it: jnp.max(upd_log[it], axis=0, keepdims=True) for it in items}
    gate = {it: dict(b_col=b_col[it], b_last=b_last[it], intra_log=intra_log[it], upd_log=upd_log[it],
                     intra_max=intra_max[it], upd_max=upd_max[it]) for it in items}

    m_run = [m_ref[h][0:1, 0:1] for h in range(heads)]
    for c, h in items:
        g = gate[c, h]
        m_new = jnp.maximum(g["b_last"] + m_run[h], g["upd_max"])
        g.update(m_in=m_run[h], m_out=m_new, carry=jnp.exp(g["b_last"] + m_run[h] - m_new))
        m_run[h] = m_new

    inter_log = {it: gate[it]["b_col"] + gate[it]["m_in"] for it in items}
    m_s = {it: jnp.maximum(inter_log[it], gate[it]["intra_max"]) for it in items}
    inter_w = {it: jnp.exp(inter_log[it] - m_s[it]) for it in items}
    s_mat = {it: qk[it] * jnp.exp(gate[it]["intra_log"] - m_s[it]) for it in items}
    s_v = {it: _mm(s_mat[it], v_s[it]) for it in items}
    s_sum = {it: jnp.sum(s_mat[it], axis=-1, keepdims=True) for it in items}
    kw = {it: k_s[it] * jnp.exp(gate[it]["upd_log"] - gate[it]["m_out"]) for it in items}
    kw_v = {it: _mm_tn(kw[it], v_s[it]) for it in items}
    kw_sum = {it: jnp.sum(kw[it], axis=0, keepdims=True) for it in items}

    c_run = [c_ref[h] for h in range(heads)]
    n_run = [n_ref[h][0:1, :] for h in range(heads)]
    c_in, n_in = {}, {}
    for c, h in items:
        c_in[c, h], n_in[c, h] = c_run[h], n_run[h]
        c_run[h] = gate[c, h]["carry"] * c_run[h] + kw_v[c, h]
        n_run[h] = gate[c, h]["carry"] * n_run[h] + kw_sum[c, h]
    for h in range(heads):
        c_ref[h] = c_run[h]
        n_ref[h] = jnp.broadcast_to(n_run[h], (SUBLANES, dk))
        m_ref[h] = jnp.broadcast_to(m_run[h], (SUBLANES, LANES))
    q_c = {it: _mm(q_s[it], c_in[it]) for it in items}
    den = {it: inter_w[it] * jnp.sum(q_s[it] * n_in[it], axis=-1, keepdims=True) + s_sum[it] for it in items}
    hid = {it: (inter_w[it] * q_c[it] + s_v[it]) / jnp.maximum(jnp.abs(den[it]), jnp.exp(-m_s[it])) for it in items}
    mean_sq = {it: jnp.mean(hid[it] * hid[it], axis=-1, keepdims=True) for it in items}
    for c, h in items:
        rows = slice(c * CHUNK, (c + 1) * CHUNK)
        out_gate = jax.nn.sigmoid(op_ref[rows, h * dv:(h + 1) * dv].astype(F32))
        normed = hid[c, h] * lax.rsqrt(mean_sq[c, h] + EPS) * ng_ref[...]
        o_ref[rows, h * dv:(h + 1) * dv] = (out_gate * normed).astype(BF16)


def _mlstm_operands(zin, gates, gate_bias, norm_g, cps):
    heads = MLSTM_HEADS
    qk_w, v_w = heads * MLSTM_DK, heads * MLSTM_DV
    steps = cps * CHUNK
    bias = gate_bias.reshape(2 * heads)
    tok = lambda w, blk: pl.BlockSpec((None, steps, w), lambda b, n: (b, n, blk))
    const = lambda *shape: pl.BlockSpec(shape, lambda b, n: (0,) * len(shape))
    in_specs = [
        tok(qk_w, 0), tok(qk_w, 1), tok(v_w, 1), tok(v_w, 2),
        pl.BlockSpec((None, steps, LANES), lambda b, n: (b, n, 0)),
        pl.BlockSpec((None, cps, 2 * heads, CHUNK), lambda b, n: (b, n, 0, 0)),
        const(1, LANES), const(cps * 2 * heads, 1), const(1, MLSTM_DV),
    ]
    operands = [zin, zin, zin, zin, gates, _gate_rows(gates, 2 * heads), _lane_row(bias, 0),
                jnp.tile(bias, cps).reshape(cps * 2 * heads, 1), norm_g.reshape(1, MLSTM_DV)]
    scratch = [
        pltpu.VMEM((heads, MLSTM_DK, MLSTM_DV), F32),
        pltpu.VMEM((heads, SUBLANES, MLSTM_DK), F32),
        pltpu.VMEM((heads, SUBLANES, LANES), F32),
    ]
    return in_specs, operands, scratch


def _ret_block(q_ref, k_ref, v_ref, g_ref, pos_ref, freq_ref, dmat_ref, xi_ref, zeta_ref, gam_ref, ng_ref,
               o_ref, s_ref, *, cps):
    heads, dk, dv = RET_HEADS, RET_DK, RET_DV
    half = dk // 2
    steps = cps * CHUNK

    @pl.when(pl.program_id(1) == 0)
    def _():
        s_ref[...] = jnp.zeros_like(s_ref)

    ang = pos_ref[...].reshape(steps, 1).astype(F32) * freq_ref[...]
    cos, sin = jnp.cos(ang), jnp.sin(ang)
    lane = lax.broadcasted_iota(jnp.int32, (steps, dk), 1)
    sin_signed = jnp.where(lane < half, -sin, sin)

    def rotary(x):
        return x * cos + pltpu.roll(x, half, 1) * sin_signed

    q_rot = [rotary(q_ref[:, h * dk:(h + 1) * dk].astype(F32)) * dk ** -0.5 for h in range(heads)]
    k_rot = [rotary(k_ref[:, h * dk:(h + 1) * dk].astype(F32)) for h in range(heads)]
    items = [(c, h) for c in range(cps) for h in range(heads)]
    rows = {c: slice(c * CHUNK, (c + 1) * CHUNK) for c in range(cps)}
    q_s = {(c, h): q_rot[h][rows[c]].astype(BF16) for c, h in items}
    k_s = {(c, h): k_rot[h][rows[c]] for c, h in items}
    v_s = {(c, h): v_ref[rows[c], h * dv:(h + 1) * dv].astype(BF16) for c, h in items}
    qk = {it: _mm_nt(q_s[it], k_s[it]) * dmat_ref[it[1]] for it in items}
    intra = {it: _mm(qk[it], v_s[it]) for it in items}
    k_v = {it: _mm_tn(k_s[it] * zeta_ref[it[1]], v_s[it]) for it in items}

    s_run = [s_ref[h] for h in range(heads)]
    s_in = {}
    for c, h in items:
        s_in[c, h] = s_run[h]
        s_run[h] = s_run[h] * gam_ref[h] + k_v[c, h]
    for h in range(heads):
        s_ref[h] = s_run[h]
    inter = {it: _mm(q_s[it], s_in[it]) for it in items}
    y = {it: intra[it] + inter[it] * xi_ref[it[1]] for it in items}
    centred = {it: y[it] - jnp.mean(y[it], axis=-1, keepdims=True) for it in items}
    var = {it: jnp.mean(centred[it] * centred[it], axis=-1, keepdims=True) for it in items}
    for c, h in items:
        normed = centred[c, h] * lax.rsqrt(var[c, h] + EPS) * ng_ref[...]
        gate = _silu(g_ref[rows[c], h * dv:(h + 1) * dv].astype(F32))
        o_ref[rows[c], h * dv:(h + 1) * dv] = (normed * gate).astype(BF16)


def _ret_operands(zin, positions, norm_g, col0, cps):
    batch, seq, _ = zin.shape
    heads, dk = RET_HEADS, RET_DK
    qk_w, v_w = heads * RET_DK, heads * RET_DV
    assert col0 % v_w == 0
    cq, cv = col0 // qk_w, col0 // v_w
    half = dk // 2
    steps = cps * CHUNK
    inv_freq = ROPE_BASE ** (-jnp.arange(half, dtype=F32) / half)
    freq = jnp.concatenate([inv_freq, inv_freq]).reshape(1, dk)
    log_gamma = jnp.log1p(-jnp.exp2(-5.0 - jnp.arange(heads, dtype=F32)))
    idx = jnp.arange(CHUNK, dtype=F32)
    tri = jnp.tril(jnp.ones((CHUNK, CHUNK), dtype=bool))
    diff = jnp.where(tri, idx[:, None] - idx[None, :], 0.0)
    dmat = jnp.where(tri, jnp.exp(diff * log_gamma[:, None, None]), 0.0)
    xi = jnp.exp((idx + 1.0) * log_gamma[:, None])[:, :, None]
    zeta = jnp.exp((CHUNK - 1.0 - idx) * log_gamma[:, None])[:, :, None]
    gamma_c = jnp.broadcast_to(jnp.exp(CHUNK * log_gamma)[:, None, None], (heads, 1, RET_DV))
    pos = positions.reshape(batch, seq // CHUNK, CHUNK, 1)
    tok = lambda w, blk: pl.BlockSpec((None, steps, w), lambda b, n: (b, n, blk))
    const = lambda *shape: pl.BlockSpec(shape, lambda b, n: (0,) * len(shape))
    in_specs = [
        tok(qk_w, cq), tok(qk_w, cq + 1), tok(v_w, cv + 1), tok(v_w, cv + 2),
        pl.BlockSpec((None, cps, CHUNK, 1), lambda b, n: (b, n, 0, 0)),
        const(1, dk), const(heads, CHUNK, CHUNK), const(heads, CHUNK, 1), const(heads, CHUNK, 1),
        const(heads, 1, RET_DV), const(1, RET_DV),
    ]
    operands = [zin, zin, zin, zin, pos, freq, dmat, xi, zeta, gamma_c, norm_g.reshape(1, RET_DV)]
    return in_specs, operands, [pltpu.VMEM((heads, RET_DK, RET_DV), F32)]


def _odd_kernel(*refs, n_mlstm, n_ret, cps):
    mlstm_in, ret_in = refs[:n_mlstm], refs[n_mlstm:n_mlstm + n_ret]
    yc_ref, yd_ref, c_ref, n_ref, m_ref, s_ref = refs[n_mlstm + n_ret:]
    _ret_block(*ret_in, yd_ref, s_ref, cps=cps)
    _mlstm_block(*mlstm_in, yc_ref, c_ref, n_ref, m_ref, cps=cps)


def _odd_mixers(zin, gates, positions, gate_bias, mlstm_g, ret_g, *, ret_col0, cps=MIXER_CHUNKS_PER_STEP):
    batch, seq, _ = zin.shape
    steps = cps * CHUNK
    m_specs, m_ops, m_scratch = _mlstm_operands(zin, gates, gate_bias, mlstm_g, cps)
    r_specs, r_ops, r_scratch = _ret_operands(zin, positions, ret_g, ret_col0, cps)
    widths = (MLSTM_HEADS * MLSTM_DV, RET_HEADS * RET_DV)
    return pl.pallas_call(
        functools.partial(_odd_kernel, n_mlstm=len(m_ops), n_ret=len(r_ops), cps=cps),
        grid=(batch, seq // steps),
        in_specs=m_specs + r_specs,
        out_specs=[pl.BlockSpec((None, steps, w), lambda b, n: (b, n, 0)) for w in widths],
        out_shape=[jax.ShapeDtypeStruct((batch, seq, w), BF16) for w in widths],
        scratch_shapes=m_scratch + r_scratch,
        compiler_params=_params("parallel", "arbitrary"),
        name="mlstm_retention",
    )(*m_ops, *r_ops)


def _pad_rows(w):
    return jnp.pad(w, ((0, LANES - w.shape[0]), (0, 0)))


def _even_mixer(x, batch, norm_g, w_in, w_out, a_re, a_im, log_step, b_re, b_im, c_re, c_im, d_skip, w_glu,
                b_glu, conv_w, a_log, dt_bias, gdn_g):
    s5_w = a_re.shape[0] * S5_GROUP
    main_w = s5_w + GDN_HEADS * (2 * GDN_DK + 2 * GDN_DV)
    w_t = w_in.T
    w_gate = _pad_rows(w_t[main_w:]).astype(BF16)
    zin, gates = _inproj(x, norm_g, w_t, [(0, main_w)], w_gate)
    zin = zin.reshape(batch, -1, main_w)
    gates = gates.reshape(batch, -1, LANES)
    ya = _s5_mixer(zin, a_re, a_im, log_step, b_re, b_im, c_re, c_im, d_skip, w_glu, b_glu)
    yb = _gdn_mixer(zin, gates, conv_w, a_log, dt_bias, gdn_g, col0=s5_w)
    m = x.shape[0]
    return _outproj(x, ya.reshape(m, -1), yb.reshape(m, -1), w_out)


def _odd_mixer(x, batch, positions, norm_g, w_in, w_out, gate_bias, mlstm_g, ret_g):
    c_main = MLSTM_HEADS * (2 * MLSTM_DK + 2 * MLSTM_DV)
    r_main = RET_HEADS * (2 * RET_DK + 2 * RET_DV)
    n_gate = 2 * MLSTM_HEADS
    w_t = w_in.T
    w_gate = _pad_rows(w_t[c_main:c_main + n_gate]).astype(BF16)
    zin, gates = _inproj(x, norm_g, w_t, [(0, c_main), (c_main + n_gate, r_main)], w_gate)
    zin = zin.reshape(batch, -1, c_main + r_main)
    gates = gates.reshape(batch, -1, LANES)
    yc, yd = _odd_mixers(zin, gates, positions, gate_bias, mlstm_g, ret_g, ret_col0=c_main)
    m = x.shape[0]
    return _outproj(x, yc.reshape(m, -1), yd.reshape(m, -1), w_out)


def kernel(x, positions, ffn_norm, ffn_w1, ffn_w3, ffn_w2, mix_norm, even_w_in, even_w_out, s5_a_re, s5_a_im, s5_log_step, s5_b_re, s5_b_im, s5_c_re, s5_c_im, s5_d, s5_w_glu, s5_b_glu, gdn_conv_w, gdn_a_log, gdn_dt_bias, gdn_norm, odd_w_in, odd_w_out, mlstm_gate_bias, mlstm_norm, ret_norm, final_norm):
    batch, seq, d = x.shape
    depth = ffn_norm.shape[0]
    x = x.reshape(batch * seq, d)

    def ffn(x, layer, which, final_g=None):
        return _ffn(x, ffn_norm[layer, which], ffn_w1, ffn_w3, ffn_w2, layer, which, final_g)

    for layer in range(depth):
        x = ffn(x, layer, 0)
        j = layer // 2
        if layer % 2 == 0:
            x = _even_mixer(x, batch, mix_norm[layer], even_w_in[j], even_w_out[j], s5_a_re[j], s5_a_im[j],
                            s5_log_step[j], s5_b_re[j], s5_b_im[j], s5_c_re[j], s5_c_im[j], s5_d[j].reshape(-1),
                            s5_w_glu[j], s5_b_glu[j], gdn_conv_w[j], gdn_a_log[j], gdn_dt_bias[j], gdn_norm[j])
        else:
            x = _odd_mixer(x, batch, positions, mix_norm[layer], odd_w_in[j], odd_w_out[j], mlstm_gate_bias[j],
                           mlstm_norm[j], ret_norm[j])
        x = ffn(x, layer, 1, final_norm if layer == depth - 1 else None)
    return x.reshape(batch, seq, d)
```

```python
import functools
import math

import jax
import jax.numpy as jnp
from jax import lax
from jax.experimental import pallas as pl
from jax.experimental.pallas import tpu as pltpu

F32 = jnp.float32
BF16 = jnp.bfloat16

EPS = 1e-6
CHUNK = 64
ROPE_BASE = 10000.0
LANES = 128
SUBLANES = 8
VMEM_LIMIT_BYTES = 60 * 1024 * 1024

S5_GROUP = 16
S5_TILE_GROUPS = LANES // S5_GROUP
S5_PAIRS = S5_TILE_GROUPS // 2
S5_TIME = 256
S5_PITCH = S5_TIME + 4
S5_SCAN_SLICE = 32

GDN_HEADS = 8
GDN_DK = 128
GDN_DV = 128
GDN_CONV = 4
MIXER_CHUNKS_PER_STEP = 4
MLSTM_HEADS = 4
MLSTM_DK = 128
MLSTM_DV = 256
RET_HEADS = 4
RET_DK = 128
RET_DV = 256


def _params(*semantics):
    return pltpu.CompilerParams(dimension_semantics=semantics, vmem_limit_bytes=VMEM_LIMIT_BYTES)


def _mm(a, b):
    return jnp.dot(a.astype(BF16), b.astype(BF16), preferred_element_type=F32)


def _mm_nt(a, b):
    return lax.dot_general(a.astype(BF16), b.astype(BF16), (((1,), (1,)), ((), ())), preferred_element_type=F32)


def _mm_tn(a, b):
    return lax.dot_general(a.astype(BF16), b.astype(BF16), (((0,), (0,)), ((), ())), preferred_element_type=F32)


def _split3(x):
    x1 = x.astype(BF16)
    r1 = x - x1.astype(F32)
    x2 = r1.astype(BF16)
    x3 = (r1 - x2.astype(F32)).astype(BF16)
    return x1, x2, x3


def _cumsum_rows(tri_lower, x):
    t = tri_lower.astype(BF16)
    return sum(jnp.dot(t, p, preferred_element_type=F32) for p in _split3(x))


def _cumsum_lanes(x, tri_upper):
    t = tri_upper.astype(BF16)
    return sum(jnp.dot(p, t, preferred_element_type=F32) for p in _split3(x))


def _rms_norm(x, g):
    return x * lax.rsqrt(jnp.mean(x * x, axis=-1, keepdims=True) + EPS) * g


def _silu(x):
    return x * jax.nn.sigmoid(x)


def _softplus(x):
    return jnp.maximum(x, 0.0) + jnp.log(1.0 + jnp.exp(-jnp.abs(x)))


def _tri_masks(n):
    r = lax.broadcasted_iota(jnp.int32, (n, n), 0)
    c = lax.broadcasted_iota(jnp.int32, (n, n), 1)
    return r >= c, r > c, r <= c


def _ffn_kernel(x_ref, g_ref, w1_ref, w3_ref, w2_ref, fg_ref, o_ref, h_ref, *, final_norm):
    j = pl.program_id(1)

    @pl.when(j == 0)
    def _():
        x = x_ref[...]
        h_ref[...] = _rms_norm(x, g_ref[...]).astype(BF16)
        o_ref[...] = x

    h = h_ref[...]
    a = jnp.dot(h, w1_ref[...].astype(BF16), preferred_element_type=F32)
    b = jnp.dot(h, w3_ref[...].astype(BF16), preferred_element_type=F32)
    o_ref[...] += 0.5 * jnp.dot((_silu(a) * b).astype(BF16), w2_ref[...].astype(BF16), preferred_element_type=F32)

    if final_norm:
        @pl.when(j == pl.num_programs(1) - 1)
        def _():
            o_ref[...] = _rms_norm(o_ref[...], fg_ref[...])


def _ffn(x, g, w1, w3, w2, layer, which, final_g=None, *, tm=1024, tf=256):
    m, d = x.shape
    f = w1.shape[-1]
    final_norm = final_g is not None
    fg = final_g if final_norm else g
    return pl.pallas_call(
        functools.partial(_ffn_kernel, final_norm=final_norm),
        grid=(m // tm, f // tf),
        in_specs=[
            pl.BlockSpec((tm, d), lambda i, j: (i, 0)),
            pl.BlockSpec((1, d), lambda i, j: (0, 0)),
            pl.BlockSpec((None, None, d, tf), lambda i, j: (layer, which, 0, j)),
            pl.BlockSpec((None, None, d, tf), lambda i, j: (layer, which, 0, j)),
            pl.BlockSpec((None, None, tf, d), lambda i, j: (layer, which, j, 0)),
            pl.BlockSpec((1, d), lambda i, j: (0, 0)),
        ],
        out_specs=pl.BlockSpec((tm, d), lambda i, j: (i, 0)),
        out_shape=jax.ShapeDtypeStruct((m, d), F32),
        scratch_shapes=[pltpu.VMEM((tm, d), BF16)],
        compiler_params=_params("parallel", "arbitrary"),
        name="ffn",
    )(x, g.reshape(1, d), w1, w3, w2, fg.reshape(1, d))


def _inproj_kernel(x_ref, g_ref, w_ref, wg_ref, o_ref, og_ref, h_ref, wbf_ref):
    i, j = pl.program_id(0), pl.program_id(1)

    @pl.when(j == 0)
    def _():
        h = _rms_norm(x_ref[...], g_ref[...]).astype(BF16)
        h_ref[...] = h
        og_ref[...] = _mm_nt(h, wg_ref[...])

    @pl.when(i == 0)
    def _():
        wbf_ref[j] = w_ref[...].astype(BF16)

    o_ref[...] = _mm_nt(h_ref[...], wbf_ref[j]).astype(BF16)


def _inproj(x, g, w_t, segments, w_gate, *, tm=1024, tn=512):
    m, d = x.shape
    starts = []
    for row0, width in segments:
        assert row0 % SUBLANES == 0 and width % tn == 0
        starts += [row0 + tn * k for k in range(width // tn)]
    nj = len(starts)

    def block_row(j):
        row, base = 0, 0
        for row0, width in segments:
            count = width // tn
            row = row + jnp.where((j >= base) & (j < base + count), row0 + tn * (j - base), 0)
            base += count
        return row

    def w_index(i, j):
        return pl.multiple_of(block_row(jnp.where(i == 0, j, nj - 1)), SUBLANES), 0

    return pl.pallas_call(
        _inproj_kernel,
        grid=(m // tm, nj),
        in_specs=[
            pl.BlockSpec((tm, d), lambda i, j: (i, 0)),
            pl.BlockSpec((1, d), lambda i, j: (0, 0)),
            pl.BlockSpec((pl.Element(tn), pl.Element(d)), w_index),
            pl.BlockSpec((LANES, d), lambda i, j: (0, 0)),
        ],
        out_specs=[
            pl.BlockSpec((tm, tn), lambda i, j: (i, j)),
            pl.BlockSpec((tm, LANES), lambda i, j: (i, 0)),
        ],
        out_shape=[jax.ShapeDtypeStruct((m, nj * tn), BF16), jax.ShapeDtypeStruct((m, LANES), F32)],
        scratch_shapes=[pltpu.VMEM((tm, d), BF16), pltpu.VMEM((nj, tn, d), BF16)],
        compiler_params=_params("arbitrary", "arbitrary"),
        name="inproj",
    )(x, g.reshape(1, d), w_t, w_gate)


def _outproj_kernel(x_ref, ya_ref, yb_ref, w_ref, o_ref, wbf_ref):
    @pl.when(pl.program_id(1) == 0)
    def _():
        wbf_ref[...] = w_ref[...].astype(BF16)

    ka = ya_ref.shape[1]
    o_ref[...] = (x_ref[...] + jnp.dot(ya_ref[...], wbf_ref[:ka, :], preferred_element_type=F32)
                  + jnp.dot(yb_ref[...], wbf_ref[ka:, :], preferred_element_type=F32))


def _outproj(x, ya, yb, w, *, tm=1024, tn=1024):
    m, d = x.shape
    ka, kb = ya.shape[1], yb.shape[1]
    return pl.pallas_call(
        _outproj_kernel,
        grid=(d // tn, m // tm),
        in_specs=[
            pl.BlockSpec((tm, tn), lambda j, i: (i, j)),
            pl.BlockSpec((tm, ka), lambda j, i: (i, 0)),
            pl.BlockSpec((tm, kb), lambda j, i: (i, 0)),
            pl.BlockSpec((ka + kb, tn), lambda j, i: (0, j)),
        ],
        out_specs=pl.BlockSpec((tm, tn), lambda j, i: (i, j)),
        out_shape=jax.ShapeDtypeStruct((m, d), F32),
        scratch_shapes=[pltpu.VMEM((ka + kb, tn), BF16)],
        compiler_params=_params("parallel", "arbitrary"),
        name="outproj",
    )(x, ya, yb, w)


def _s5_disc_kernel(are_ref, aim_ref, step_ref, bre_ref, bim_ref, lr_ref, li_ref, bbr_ref, bbi_ref):
    ar, ai = are_ref[...], aim_ref[...]
    step = jnp.exp(step_ref[...])
    mag = jnp.exp(ar * step)
    lr, li = mag * jnp.cos(ai * step), mag * jnp.sin(ai * step)
    den = ar * ar + ai * ai
    fr = ((lr - 1.0) * ar + li * ai) / den
    fi = (li * ar - (lr - 1.0) * ai) / den
    lr_ref[...] = lr
    li_ref[...] = li
    br, bi = bre_ref[...], bim_ref[...]
    bbr_ref[...] = fr[None] * br - fi[None] * bi
    bbi_ref[...] = fr[None] * bi + fi[None] * br


def _s5_discretise(a_re, a_im, log_step, b_re, b_im):
    g, p = a_re.shape
    h = b_re.shape[-1]
    sd = jax.ShapeDtypeStruct
    return pl.pallas_call(
        _s5_disc_kernel,
        out_shape=[sd((g, p), F32), sd((g, p), F32), sd((h, g, p), F32), sd((h, g, p), F32)],
        name="s5_discretise",
    )(a_re, a_im, log_step.reshape(g, 1), b_re.transpose(2, 0, 1), b_im.transpose(2, 0, 1))


def _s5_kernel(u_ref, wb_ref, lr_ref, li_ref, cs_ref, dsk_ref, wglu_ref, bglu_ref, o_ref,
               rre_ref, rim_ref, xr_ref, xi_ref, y_ref, *, batch, tiles):
    per = tiles * S5_PAIRS

    @pl.when(pl.program_id(0) == 0)
    def _():
        xr_ref[...] = jnp.zeros_like(xr_ref)
        xi_ref[...] = jnp.zeros_like(xi_ref)

    def chain_rows(j, pair, b):
        return pl.ds((b * per + j * S5_PAIRS + pair) * S5_PITCH, S5_TIME)

    def drive(b):
        for j in range(tiles):
            d = jnp.dot(u_ref[b, :, j * LANES:(j + 1) * LANES].astype(BF16), wb_ref[j], preferred_element_type=F32)
            for pair in range(S5_PAIRS):
                rre_ref[chain_rows(j, pair, b), :] = d[:, pair * LANES:(pair + 1) * LANES]
                rim_ref[chain_rows(j, pair, b), :] = d[:, (S5_PAIRS + pair) * LANES:(S5_PAIRS + pair + 1) * LANES]
            yield

    def scan(b):
        lr, li = lr_ref[...], li_ref[...]
        own = pl.ds(b * per, per)
        xr, xi = xr_ref[own, :], xi_ref[own, :]
        for t in range(S5_TIME):
            rows = pl.ds(b * per * S5_PITCH + t, per, stride=S5_PITCH)
            xr, xi = lr * xr - li * xi + rre_ref[rows, :], lr * xi + li * xr + rim_ref[rows, :]
            rre_ref[rows, :] = xr
            rim_ref[rows, :] = xi
            if t % S5_SCAN_SLICE == S5_SCAN_SLICE - 1:
                yield
        xr_ref[own, :] = xr
        xi_ref[own, :] = xi

    def readout(b):
        for j in range(tiles):
            states = jnp.concatenate(
                [rre_ref[chain_rows(j, pair, b), :].astype(BF16) for pair in range(S5_PAIRS)]
                + [rim_ref[chain_rows(j, pair, b), :].astype(BF16) for pair in range(S5_PAIRS)], axis=1)
            cols = slice(j * LANES, (j + 1) * LANES)
            y = jnp.dot(states, cs_ref[j], preferred_element_type=F32) + dsk_ref[:, cols] * u_ref[b, :, cols].astype(F32)
            y_ref[b, :, cols] = jax.nn.gelu(y)
            yield
        y = y_ref[b]
        gate = jnp.dot(y.astype(BF16), wglu_ref[...], preferred_element_type=F32) + bglu_ref[...]
        o_ref[b] = (y * jax.nn.sigmoid(gate)).astype(BF16)
        yield

    def alongside(vector_work, matrix_work):
        done = object()
        while next(vector_work, done) is not done:
            next(matrix_work, done)
        for _ in matrix_work:
            pass

    for _ in drive(0):
        pass
    for b in range(batch):
        nxt = drive(b + 1) if b + 1 < batch else iter(())
        prev = readout(b - 1) if b > 0 else iter(())
        alongside(scan(b), (x for gen in (prev, nxt) for x in gen))
    for _ in readout(batch - 1):
        pass


def _s5_mixer(zin, a_re, a_im, log_step, b_re, b_im, c_re, c_im, d_skip, w_glu, b_glu):
    batch, seq, _ = zin.shape
    groups, states = a_re.shape
    width = groups * S5_GROUP
    tiles = width // LANES
    chains = tiles * S5_PAIRS * batch
    lr, li, bbr, bbi = _s5_discretise(a_re, a_im, log_step, b_re, b_im)

    eye = jnp.eye(S5_TILE_GROUPS, dtype=F32)
    bb = jnp.stack([bbr, bbi]).reshape(2, S5_GROUP, tiles, S5_TILE_GROUPS, states)
    wb = jnp.einsum("rhjgp,gk->jghrkp", bb, eye).reshape(tiles, LANES, 2 * S5_TILE_GROUPS * states).astype(BF16)
    cc = jnp.stack([c_re, -c_im]).reshape(2, tiles, S5_TILE_GROUPS, S5_GROUP, states)
    cs = jnp.einsum("rjghp,gk->jrkpgh", cc, eye).reshape(tiles, 2 * S5_TILE_GROUPS * states, LANES).astype(BF16)

    def per_chain(lam):
        return lam.reshape(tiles * S5_PAIRS, 2 * states)

    const = lambda *shape: pl.BlockSpec(shape, lambda i: (0,) * len(shape))
    return pl.pallas_call(
        functools.partial(_s5_kernel, batch=batch, tiles=tiles),
        grid=(seq // S5_TIME,),
        in_specs=[
            pl.BlockSpec((batch, S5_TIME, width), lambda i: (0, i, 0)),
            const(tiles, LANES, 2 * S5_TILE_GROUPS * states),
            const(chains // batch, LANES),
            const(chains // batch, LANES),
            const(tiles, 2 * S5_TILE_GROUPS * states, LANES),
            const(1, width),
            const(width, width),
            const(1, width),
        ],
        out_specs=pl.BlockSpec((batch, S5_TIME, width), lambda i: (0, i, 0)),
        out_shape=jax.ShapeDtypeStruct((batch, seq, width), BF16),
        scratch_shapes=[
            pltpu.VMEM((chains * S5_PITCH, LANES), F32),
            pltpu.VMEM((chains * S5_PITCH, LANES), F32),
            pltpu.VMEM((chains, LANES), F32),
            pltpu.VMEM((chains, LANES), F32),
            pltpu.VMEM((batch, S5_TIME, width), F32),
        ],
        compiler_params=_params("arbitrary"),
        name="s5",
    )(zin, wb, per_chain(lr), per_chain(li), cs, d_skip.reshape(1, width), w_glu.astype(BF16),
      b_glu.reshape(1, width))


def _gdn_kernel(q_ref, k_ref, v_ref, z_ref, gc_ref, gr_ref, cw_ref, alc_ref, dtc_ref, alr_ref, dtr_ref,
                ng_ref, o_ref, s_ref, tail_ref, *, cps):
    heads, dk, dv = GDN_HEADS, GDN_DK, GDN_DV
    steps = cps * CHUNK

    @pl.when(pl.program_id(1) == 0)
    def _():
        s_ref[...] = jnp.zeros_like(s_ref)
        tail_ref[:, :2 * SUBLANES, :] = jnp.zeros((tail_ref.shape[0], 2 * SUBLANES, LANES), F32)

    tri, strict, tri_u = _tri_masks(CHUNK)
    row = lax.broadcasted_iota(jnp.int32, (steps, steps), 0)
    col = lax.broadcasted_iota(jnp.int32, (steps, steps), 1)
    tri_chunks = (row >= col) & (row // CHUNK == col // CHUNK)
    eye = (lax.broadcasted_iota(jnp.int32, (CHUNK, CHUNK), 0)
           == lax.broadcasted_iota(jnp.int32, (CHUNK, CHUNK), 1)).astype(F32)

    def time_rows(first, count):
        return pl.ds(2 * first, count, stride=2)

    conv = []
    for idx, ref in enumerate((q_ref, k_ref, v_ref)):
        slabs = []
        for h in range(ref.shape[1] // LANES):
            lanes = slice(h * LANES, (h + 1) * LANES)
            cur = ref[:, lanes].astype(F32)
            hist = tail_ref.at[idx * heads + h]
            hist[time_rows(SUBLANES, steps), :] = cur
            cw = cw_ref[:, idx * ref.shape[1] + h * LANES:idx * ref.shape[1] + (h + 1) * LANES]
            acc = cur * cw[GDN_CONV - 1:GDN_CONV]
            for s in range(1, GDN_CONV):
                acc = acc + hist[time_rows(SUBLANES - s, steps), :] * cw[GDN_CONV - 1 - s:GDN_CONV - s]
            hist[time_rows(0, SUBLANES), :] = cur[steps - SUBLANES:]
            slabs.append(_silu(acc))
        conv.append(slabs)
    q_all, k_all, v_all = conv

    gc = gc_ref[...]
    beta_c = jax.nn.sigmoid(gc)
    g_c = -jnp.exp(alc_ref[...]) * _softplus(gc + dtc_ref[...])
    gcum_c = _cumsum_rows(tri_chunks, g_c)
    gr = gr_ref[...].reshape(cps * 2 * heads, CHUNK)
    g_r = -jnp.exp(alr_ref[...]) * _softplus(gr + dtr_ref[...])
    gcum_r = _cumsum_lanes(g_r, tri_u)

    items = [(c, h) for c in range(cps) for h in range(heads)]
    tok = {(c, h): slice(c * CHUNK, (c + 1) * CHUNK) for c, h in items}
    q_raw = {it: q_all[it[1]][tok[it]] for it in items}
    k_raw = {it: k_all[it[1]][tok[it]] for it in items}
    v = {it: v_all[it[1]][tok[it]] for it in items}
    q_ss = {it: jnp.sum(q_raw[it] * q_raw[it], axis=-1, keepdims=True) for it in items}
    k_ss = {it: jnp.sum(k_raw[it] * k_raw[it], axis=-1, keepdims=True) for it in items}
    q = {it: q_raw[it] * (lax.rsqrt(q_ss[it] + EPS) * dk ** -0.5) for it in items}
    k = {it: k_raw[it] * lax.rsqrt(k_ss[it] + EPS) for it in items}
    beta = {it: beta_c[tok[it], it[1]:it[1] + 1] for it in items}
    gcol = {it: gcum_c[tok[it], heads + it[1]:heads + it[1] + 1] for it in items}
    grow = {(c, h): gcum_r[c * 2 * heads + heads + h:c * 2 * heads + heads + h + 1, :] for c, h in items}
    glast = {it: gcol[it][CHUNK - 1:CHUNK, :] for it in items}
    decay = {it: jnp.where(tri, jnp.exp(gcol[it] - grow[it]), 0.0) for it in items}
    egc = {it: jnp.exp(gcol[it]) for it in items}
    kb = {it: k[it] * beta[it] for it in items}
    lower = {it: jnp.where(strict, _mm_nt(kb[it], k[it]) * decay[it], 0.0) for it in items}
    rhs = {it: jnp.concatenate([v[it] * beta[it], kb[it] * egc[it]], axis=1).astype(BF16) for it in items}
    qk = {it: jnp.where(tri, _mm_nt(q[it], k[it]) * decay[it], 0.0).astype(BF16) for it in items}
    q_dec = {it: (q[it] * egc[it]).astype(BF16) for it in items}
    k_dec = {it: (k[it] * jnp.exp(glast[it] - gcol[it])).astype(BF16) for it in items}
    carry = {it: jnp.exp(glast[it]) for it in items}

    power = {it: -lower[it] for it in items}
    inv = {it: eye + power[it] for it in items}
    for _ in range(int(math.log2(CHUNK)) - 1):
        for it in items:
            power[it] = _mm(power[it], power[it])
        for it in items:
            inv[it] = inv[it] + _mm(inv[it], power[it])
    sol = {it: _mm(inv[it], rhs[it]) for it in items}

    out = {}
    for c in range(cps):
        state = [s_ref[h] for h in range(heads)]
        v_new = [sol[c, h][:, :dv] - _mm(sol[c, h][:, dv:], state[h]) for h in range(heads)]
        for h in range(heads):
            out[c, h] = _mm(q_dec[c, h], state[h]) + _mm(qk[c, h], v_new[h])
        for h in range(heads):
            s_ref[h] = state[h] * carry[c, h] + _mm_tn(k_dec[c, h], v_new[h])

    mean_sq = {it: jnp.mean(out[it] * out[it], axis=-1, keepdims=True) for it in items}
    for c, h in items:
        normed = out[c, h] * lax.rsqrt(mean_sq[c, h] + EPS) * ng_ref[...]
        gate = _silu(z_ref[tok[c, h], h * dv:(h + 1) * dv].astype(F32))
        o_ref[tok[c, h], h * dv:(h + 1) * dv] = (normed * gate).astype(BF16)


def _gate_rows(gates, count):
    b, l, _ = gates.shape
    return gates[:, :, :count].reshape(b, l // CHUNK, CHUNK, count).transpose(0, 1, 3, 2)


def _lane_row(vec, offset):
    return jnp.zeros((1, LANES), F32).at[0, offset:offset + vec.shape[0]].set(vec)


def _gdn_mixer(zin, gates, conv_w, a_log, dt_bias, norm_g, *, col0, cps=MIXER_CHUNKS_PER_STEP):
    batch, seq, _ = zin.shape
    heads = GDN_HEADS
    qk_w, v_w = heads * GDN_DK, heads * GDN_DV
    assert GDN_DK == GDN_DV == LANES and col0 % qk_w == 0
    c0 = col0 // qk_w
    steps = cps * CHUNK
    zeros = jnp.zeros((heads,), F32)
    al_c, dt_c = _lane_row(a_log, heads), _lane_row(dt_bias, heads)
    al_r = jnp.tile(jnp.concatenate([zeros, a_log]), cps).reshape(cps * 2 * heads, 1)
    dt_r = jnp.tile(jnp.concatenate([zeros, dt_bias]), cps).reshape(cps * 2 * heads, 1)
    tok = lambda blk: pl.BlockSpec((None, steps, qk_w), lambda b, n, blk=blk: (b, n, c0 + blk))
    const = lambda *shape: pl.BlockSpec(shape, lambda b, n: (0,) * len(shape))
    return pl.pallas_call(
        functools.partial(_gdn_kernel, cps=cps),
        grid=(batch, seq // steps),
        in_specs=[
            tok(0), tok(1), tok(2), tok(3),
            pl.BlockSpec((None, steps, LANES), lambda b, n: (b, n, 0)),
            pl.BlockSpec((None, cps, 2 * heads, CHUNK), lambda b, n: (b, n, 0, 0)),
            const(GDN_CONV, 2 * qk_w + v_w),
            const(1, LANES), const(1, LANES), const(cps * 2 * heads, 1), const(cps * 2 * heads, 1),
            const(1, GDN_DV),
        ],
        out_specs=pl.BlockSpec((None, steps, v_w), lambda b, n: (b, n, 0)),
        out_shape=jax.ShapeDtypeStruct((batch, seq, v_w), BF16),
        scratch_shapes=[
            pltpu.VMEM((heads, GDN_DK, GDN_DV), F32),
            pltpu.VMEM((3 * heads, 2 * (SUBLANES + steps), LANES), F32),
        ],
        compiler_params=_params("parallel", "arbitrary"),
        name="gdn",
    )(zin, zin, zin, zin, gates, _gate_rows(gates, 2 * heads), conv_w, al_c, dt_c, al_r, dt_r,
      norm_g.reshape(1, GDN_DV))


def _mlstm_block(q_ref, k_ref, v_ref, op_ref, gc_ref, gr_ref, bc_ref, br_ref, ng_ref, o_ref,
                 c_ref, n_ref, m_ref, *, cps):
    heads, dk, dv = MLSTM_HEADS, MLSTM_DK, MLSTM_DV
    steps = cps * CHUNK

    @pl.when(pl.program_id(1) == 0)
    def _():
        c_ref[...] = jnp.zeros_like(c_ref)
        n_ref[...] = jnp.zeros_like(n_ref)
        m_ref[...] = jnp.zeros_like(m_ref)

    tri, _, tri_u = _tri_masks(CHUNK)
    row = lax.broadcasted_iota(jnp.int32, (steps, steps), 0)
    col = lax.broadcasted_iota(jnp.int32, (steps, steps), 1)
    tri_chunks = (row >= col) & (row // CHUNK == col // CHUNK)
    pre_c = gc_ref[...] + bc_ref[...]
    bcum_c = _cumsum_rows(tri_chunks, -_softplus(-pre_c))
    pre_r = gr_ref[...].reshape(cps * 2 * heads, CHUNK) + br_ref[...]
    bcum_r = _cumsum_lanes(-_softplus(-pre_r), tri_u)

    items = [(c, h) for c in range(cps) for h in range(heads)]
    q_s = {it: q_ref[it[0] * CHUNK:(it[0] + 1) * CHUNK, it[1] * dk:(it[1] + 1) * dk].astype(F32) * dk ** -0.5
           for it in items}
    k_s = {it: k_ref[it[0] * CHUNK:(it[0] + 1) * CHUNK, it[1] * dk:(it[1] + 1) * dk].astype(F32) for it in items}
    v_s = {it: v_ref[it[0] * CHUNK:(it[0] + 1) * CHUNK, it[1] * dv:(it[1] + 1) * dv].astype(BF16) for it in items}
    qk = {it: _mm_nt(q_s[it], k_s[it]) for it in items}
    tok = {(c, h): slice(c * CHUNK, (c + 1) * CHUNK) for c, h in items}
    gl = {(c, h): c * 2 * heads + h for c, h in items}
    ig_col = {it: pre_c[tok[it], it[1]:it[1] + 1] for it in items}
    ig_row = {it: pre_r[gl[it]:gl[it] + 1, :] for it in items}
    b_col = {it: bcum_c[tok[it], heads + it[1]:heads + it[1] + 1] for it in items}
    b_row = {it: bcum_r[gl[it] + heads:gl[it] + heads + 1, :] for it in items}
    b_last = {it: b_col[it][CHUNK - 1:CHUNK, :] for it in items}
    intra_log = {it: jnp.where(tri, b_col[it] - b_row[it] + ig_row[it], -jnp.inf) for it in items}
    intra_max = {it: jnp.max(intra_log[it], axis=-1, keepdims=True) for it in items}
    upd_log = {it: b_last[it] - b_col[it] + ig_col[it] for it in items}
    upd_max = {it: jnp.max(upd_log[it], axis=0, keepdims=True) for it in items}
    gate = {it: dict(b_col=b_col[it], b_last=b_last[it], intra_log=intra_log[it], upd_log=upd_log[it],
                     intra_max=intra_max[it], upd_max=upd_max[it]) for it in items}

    m_run = [m_ref[h][0:1, 0:1] for h in range(heads)]
    for c, h in items:
        g = gate[c, h]
        m_new = jnp.maximum(g["b_last"] + m_run[h], g["upd_max"])
        g.update(m_in=m_run[h], m_out=m_new, carry=jnp.exp(g["b_last"] + m_run[h] - m_new))
        m_run[h] = m_new

    inter_log = {it: gate[it]["b_col"] + gate[it]["m_in"] for it in items}
    m_s = {it: jnp.maximum(inter_log[it], gate[it]["intra_max"]) for it in items}
    inter_w = {it: jnp.exp(inter_log[it] - m_s[it]) for it in items}
    s_mat = {it: qk[it] * jnp.exp(gate[it]["intra_log"] - m_s[it]) for it in items}
    s_v = {it: _mm(s_mat[it], v_s[it]) for it in items}
    s_sum = {it: jnp.sum(s_mat[it], axis=-1, keepdims=True) for it in items}
    kw = {it: k_s[it] * jnp.exp(gate[it]["upd_log"] - gate[it]["m_out"]) for it in items}
    kw_v = {it: _mm_tn(kw[it], v_s[it]) for it in items}
    kw_sum = {it: jnp.sum(kw[it], axis=0, keepdims=True) for it in items}

    c_run = [c_ref[h] for h in range(heads)]
    n_run = [n_ref[h][0:1, :] for h in range(heads)]
    c_in, n_in = {}, {}
    for c, h in items:
        c_in[c, h], n_in[c, h] = c_run[h], n_run[h]
        c_run[h] = gate[c, h]["carry"] * c_run[h] + kw_v[c, h]
        n_run[h] = gate[c, h]["carry"] * n_run[h] + kw_sum[c, h]
    for h in range(heads):
        c_ref[h] = c_run[h]
        n_ref[h] = jnp.broadcast_to(n_run[h], (SUBLANES, dk))
        m_ref[h] = jnp.broadcast_to(m_run[h], (SUBLANES, LANES))
    q_c = {it: _mm(q_s[it], c_in[it]) for it in items}
    den = {it: inter_w[it] * jnp.sum(q_s[it] * n_in[it], axis=-1, keepdims=True) + s_sum[it] for it in items}
    hid = {it: (inter_w[it] * q_c[it] + s_v[it]) / jnp.maximum(jnp.abs(den[it]), jnp.exp(-m_s[it])) for it in items}
    mean_sq = {it: jnp.mean(hid[it] * hid[it], axis=-1, keepdims=True) for it in items}
    for c, h in items:
        rows = slice(c * CHUNK, (c + 1) * CHUNK)
        out_gate = jax.nn.sigmoid(op_ref[rows, h * dv:(h + 1) * dv].astype(F32))
        normed = hid[c, h] * lax.rsqrt(mean_sq[c, h] + EPS) * ng_ref[...]
        o_ref[rows, h * dv:(h + 1) * dv] = (out_gate * normed).astype(BF16)


def _mlstm_operands(zin, gates, gate_bias, norm_g, cps):
    heads = MLSTM_HEADS
    qk_w, v_w = heads * MLSTM_DK, heads * MLSTM_DV
    steps = cps * CHUNK
    bias = gate_bias.reshape(2 * heads)
    tok = lambda w, blk: pl.BlockSpec((None, steps, w), lambda b, n: (b, n, blk))
    const = lambda *shape: pl.BlockSpec(shape, lambda b, n: (0,) * len(shape))
    in_specs = [
        tok(qk_w, 0), tok(qk_w, 1), tok(v_w, 1), tok(v_w, 2),
        pl.BlockSpec((None, steps, LANES), lambda b, n: (b, n, 0)),
        pl.BlockSpec((None, cps, 2 * heads, CHUNK), lambda b, n: (b, n, 0, 0)),
        const(1, LANES), const(cps * 2 * heads, 1), const(1, MLSTM_DV),
    ]
    operands = [zin, zin, zin, zin, gates, _gate_rows(gates, 2 * heads), _lane_row(bias, 0),
                jnp.tile(bias, cps).reshape(cps * 2 * heads, 1), norm_g.reshape(1, MLSTM_DV)]
    scratch = [
        pltpu.VMEM((heads, MLSTM_DK, MLSTM_DV), F32),
        pltpu.VMEM((heads, SUBLANES, MLSTM_DK), F32),
        pltpu.VMEM((heads, SUBLANES, LANES), F32),
    ]
    return in_specs, operands, scratch


def _ret_block(q_ref, k_ref, v_ref, g_ref, pos_ref, freq_ref, dmat_ref, xi_ref, zeta_ref, gam_ref, ng_ref,
               o_ref, s_ref, *, cps):
    heads, dk, dv = RET_HEADS, RET_DK, RET_DV
    half = dk // 2
    steps = cps * CHUNK

    @pl.when(pl.program_id(1) == 0)
    def _():
        s_ref[...] = jnp.zeros_like(s_ref)

    ang = pos_ref[...].reshape(steps, 1).astype(F32) * freq_ref[...]
    cos, sin = jnp.cos(ang), jnp.sin(ang)
    lane = lax.broadcasted_iota(jnp.int32, (steps, dk), 1)
    sin_signed = jnp.where(lane < half, -sin, sin)

    def rotary(x):
        return x * cos + pltpu.roll(x, half, 1) * sin_signed

    q_rot = [rotary(q_ref[:, h * dk:(h + 1) * dk].astype(F32)) * dk ** -0.5 for h in range(heads)]
    k_rot = [rotary(k_ref[:, h * dk:(h + 1) * dk].astype(F32)) for h in range(heads)]
    items = [(c, h) for c in range(cps) for h in range(heads)]
    rows = {c: slice(c * CHUNK, (c + 1) * CHUNK) for c in range(cps)}
    q_s = {(c, h): q_rot[h][rows[c]].astype(BF16) for c, h in items}
    k_s = {(c, h): k_rot[h][rows[c]] for c, h in items}
    v_s = {(c, h): v_ref[rows[c], h * dv:(h + 1) * dv].astype(BF16) for c, h in items}
    qk = {it: _mm_nt(q_s[it], k_s[it]) * dmat_ref[it[1]] for it in items}
    intra = {it: _mm(qk[it], v_s[it]) for it in items}
    k_v = {it: _mm_tn(k_s[it] * zeta_ref[it[1]], v_s[it]) for it in items}

    s_run = [s_ref[h] for h in range(heads)]
    s_in = {}
    for c, h in items:
        s_in[c, h] = s_run[h]
        s_run[h] = s_run[h] * gam_ref[h] + k_v[c, h]
    for h in range(heads):
        s_ref[h] = s_run[h]
    inter = {it: _mm(q_s[it], s_in[it]) for it in items}
    y = {it: intra[it] + inter[it] * xi_ref[it[1]] for it in items}
    centred = {it: y[it] - jnp.mean(y[it], axis=-1, keepdims=True) for it in items}
    var = {it: jnp.mean(centred[it] * centred[it], axis=-1, keepdims=True) for it in items}
    for c, h in items:
        normed = centred[c, h] * lax.rsqrt(var[c, h] + EPS) * ng_ref[...]
        gate = _silu(g_ref[rows[c], h * dv:(h + 1) * dv].astype(F32))
        o_ref[rows[c], h * dv:(h + 1) * dv] = (normed * gate).astype(BF16)


def _ret_operands(zin, positions, norm_g, col0, cps):
    batch, seq, _ = zin.shape
    heads, dk = RET_HEADS, RET_DK
    qk_w, v_w = heads * RET_DK, heads * RET_DV
    assert col0 % v_w == 0
    cq, cv = col0 // qk_w, col0 // v_w
    half = dk // 2
    steps = cps * CHUNK
    inv_freq = ROPE_BASE ** (-jnp.arange(half, dtype=F32) / half)
    freq = jnp.concatenate([inv_freq, inv_freq]).reshape(1, dk)
    log_gamma = jnp.log1p(-jnp.exp2(-5.0 - jnp.arange(heads, dtype=F32)))
    idx = jnp.arange(CHUNK, dtype=F32)
    tri = jnp.tril(jnp.ones((CHUNK, CHUNK), dtype=bool))
    diff = jnp.where(tri, idx[:, None] - idx[None, :], 0.0)
    dmat = jnp.where(tri, jnp.exp(diff * log_gamma[:, None, None]), 0.0)
    xi = jnp.exp((idx + 1.0) * log_gamma[:, None])[:, :, None]
    zeta = jnp.exp((CHUNK - 1.0 - idx) * log_gamma[:, None])[:, :, None]
    gamma_c = jnp.broadcast_to(jnp.exp(CHUNK * log_gamma)[:, None, None], (heads, 1, RET_DV))
    pos = positions.reshape(batch, seq // CHUNK, CHUNK, 1)
    tok = lambda w, blk: pl.BlockSpec((None, steps, w), lambda b, n: (b, n, blk))
    const = lambda *shape: pl.BlockSpec(shape, lambda b, n: (0,) * len(shape))
    in_specs = [
        tok(qk_w, cq), tok(qk_w, cq + 1), tok(v_w, cv + 1), tok(v_w, cv + 2),
        pl.BlockSpec((None, cps, CHUNK, 1), lambda b, n: (b, n, 0, 0)),
        const(1, dk), const(heads, CHUNK, CHUNK), const(heads, CHUNK, 1), const(heads, CHUNK, 1),
        const(heads, 1, RET_DV), const(1, RET_DV),
    ]
    operands = [zin, zin, zin, zin, pos, freq, dmat, xi, zeta, gamma_c, norm_g.reshape(1, RET_DV)]
    return in_specs, operands, [pltpu.VMEM((heads, RET_DK, RET_DV), F32)]


def _odd_kernel(*refs, n_mlstm, n_ret, cps):
    mlstm_in, ret_in = refs[:n_mlstm], refs[n_mlstm:n_mlstm + n_ret]
    yc_ref, yd_ref, c_ref, n_ref, m_ref, s_ref = refs[n_mlstm + n_ret:]
    _ret_block(*ret_in, yd_ref, s_ref, cps=cps)
    _mlstm_block(*mlstm_in, yc_ref, c_ref, n_ref, m_ref, cps=cps)


def _odd_mixers(zin, gates, positions, gate_bias, mlstm_g, ret_g, *, ret_col0, cps=MIXER_CHUNKS_PER_STEP):
    batch, seq, _ = zin.shape
    steps = cps * CHUNK
    m_specs, m_ops, m_scratch = _mlstm_operands(zin, gates, gate_bias, mlstm_g, cps)
    r_specs, r_ops, r_scratch = _ret_operands(zin, positions, ret_g, ret_col0, cps)
    widths = (MLSTM_HEADS * MLSTM_DV, RET_HEADS * RET_DV)
    return pl.pallas_call(
        functools.partial(_odd_kernel, n_mlstm=len(m_ops), n_ret=len(r_ops), cps=cps),
        grid=(batch, seq // steps),
        in_specs=m_specs + r_specs,
        out_specs=[pl.BlockSpec((None, steps, w), lambda b, n: (b, n, 0)) for w in widths],
        out_shape=[jax.ShapeDtypeStruct((batch, seq, w), BF16) for w in widths],
        scratch_shapes=m_scratch + r_scratch,
        compiler_params=_params("parallel", "arbitrary"),
        name="mlstm_retention",
    )(*m_ops, *r_ops)


def _pad_rows(w):
    return jnp.pad(w, ((0, LANES - w.shape[0]), (0, 0)))


def _even_mixer(x, batch, norm_g, w_in, w_out, a_re, a_im, log_step, b_re, b_im, c_re, c_im, d_skip, w_glu,
                b_glu, conv_w, a_log, dt_bias, gdn_g):
    s5_w = a_re.shape[0] * S5_GROUP
    main_w = s5_w + GDN_HEADS * (2 * GDN_DK + 2 * GDN_DV)
    w_t = w_in.T
    w_gate = _pad_rows(w_t[main_w:]).astype(BF16)
    zin, gates = _inproj(x, norm_g, w_t, [(0, main_w)], w_gate)
    zin = zin.reshape(batch, -1, main_w)
    gates = gates.reshape(batch, -1, LANES)
    ya = _s5_mixer(zin, a_re, a_im, log_step, b_re, b_im, c_re, c_im, d_skip, w_glu, b_glu)
    yb = _gdn_mixer(zin, gates, conv_w, a_log, dt_bias, gdn_g, col0=s5_w)
    m = x.shape[0]
    return _outproj(x, ya.reshape(m, -1), yb.reshape(m, -1), w_out)


def _odd_mixer(x, batch, positions, norm_g, w_in, w_out, gate_bias, mlstm_g, ret_g):
    c_main = MLSTM_HEADS * (2 * MLSTM_DK + 2 * MLSTM_DV)
    r_main = RET_HEADS * (2 * RET_DK + 2 * RET_DV)
    n_gate = 2 * MLSTM_HEADS
    w_t = w_in.T
    w_gate = _pad_rows(w_t[c_main:c_main + n_gate]).astype(BF16)
    zin, gates = _inproj(x, norm_g, w_t, [(0, c_main), (c_main + n_gate, r_main)], w_gate)
    zin = zin.reshape(batch, -1, c_main + r_main)
    gates = gates.reshape(batch, -1, LANES)
    yc, yd = _odd_mixers(zin, gates, positions, gate_bias, mlstm_g, ret_g, ret_col0=c_main)
    m = x.shape[0]
    return _outproj(x, yc.reshape(m, -1), yd.reshape(m, -1), w_out)


def kernel(x, positions, ffn_norm, ffn_w1, ffn_w3, ffn_w2, mix_norm, even_w_in, even_w_out, s5_a_re, s5_a_im, s5_log_step, s5_b_re, s5_b_im, s5_c_re, s5_c_im, s5_d, s5_w_glu, s5_b_glu, gdn_conv_w, gdn_a_log, gdn_dt_bias, gdn_norm, odd_w_in, odd_w_out, mlstm_gate_bias, mlstm_norm, ret_norm, final_norm):
    batch, seq, d = x.shape
    depth = ffn_norm.shape[0]
    x = x.reshape(batch * seq, d)

    def ffn(x, layer, which, final_g=None):
        return _ffn(x, ffn_norm[layer, which], ffn_w1, ffn_w3, ffn_w2, layer, which, final_g)

    for layer in range(depth):
        x = ffn(x, layer, 0)
        j = layer // 2
        if layer % 2 == 0:
            x = _even_mixer(x, batch, mix_norm[layer], even_w_in[j], even_w_out[j], s5_a_re[j], s5_a_im[j],
                            s5_log_step[j], s5_b_re[j], s5_b_im[j], s5_c_re[j], s5_c_im[j], s5_d[j].reshape(-1),
                            s5_w_glu[j], s5_b_glu[j], gdn_conv_w[j], gdn_a_log[j], gdn_dt_bias[j], gdn_norm[j])
        else:
            x = _odd_mixer(x, batch, positions, mix_norm[layer], odd_w_in[j], odd_w_out[j], mlstm_gate_bias[j],
                           mlstm_norm[j], ret_norm[j])
        x = ffn(x, layer, 1, final_norm if layer == depth - 1 else None)
    return x.reshape(batch, seq, d)
```
